```python
import jax
import jax.numpy as jnp
from jax import lax
import numpy as np

D_MODEL = 2048
BATCH = 4
SEQ = 2048
DEPTH = 2
DEC_BATCH = 32
DEC_SEQ = 4
PAST_LEN = 16384
PAGE_SIZE = 128

CONV_W = 4
CHUNK = 64
DN_HEADS = 8
DN_DK = 128
DN_DV = 128
DN_KEY = DN_HEADS * DN_DK
DN_VAL = DN_HEADS * DN_DV
DN_CONV_DIM = 2 * DN_KEY + DN_VAL
SSM_HEADS = 16
SSM_HEADDIM = 64
SSM_INNER = SSM_HEADS * SSM_HEADDIM
SSM_GROUPS = 2
SSM_STATE = 128
SSM_CONV_DIM = SSM_INNER + 2 * SSM_GROUPS * SSM_STATE
SWA_HEADS = 16
SWA_KV_HEADS = 4
SWA_HEAD_DIM = 64
WINDOW = 128
ROPE_DIM = SWA_HEAD_DIM // 4
ROPE_THETA = 500000.0
N_BRANCH = 3
BRANCH_WIDTH = 1024
N_EXPERTS = 64
EXPERT_FF = 512
TOP_K = 8
N_GROUPS = 8
TOPK_GROUPS = 4
ROUTED_SCALE = 2.5
SHARED_FF = 512
MOE_BLOCK = 1024
ALPHA = (2 * DEPTH) ** 0.25
BETA_INIT = (8 * DEPTH) ** -0.25
LN_EPS = 1e-5
NORM_EPS = 1e-6
IN_SPLITS = (DN_CONV_DIM, DN_VAL, DN_HEADS, DN_HEADS, SSM_INNER, SSM_CONV_DIM, SSM_HEADS,
             SWA_HEADS * SWA_HEAD_DIM, SWA_KV_HEADS * SWA_HEAD_DIM, SWA_KV_HEADS * SWA_HEAD_DIM,
             N_BRANCH * D_MODEL)
IN_DIM = sum(IN_SPLITS)
F32 = jnp.float32

kernel_name = 'hybrid_delta_ssd_swa_moe_step'


def _split_points(widths):
    pts, acc = [], 0
    for w in widths[:-1]:
        acc += w
        pts.append(acc)
    return pts


def _layer_norm(x, g, b):
    xf = x.astype(F32)
    mu = jnp.mean(xf, -1, keepdims=True)
    var = jnp.mean(jnp.square(xf - mu), -1, keepdims=True)
    return ((xf - mu) * lax.rsqrt(var + LN_EPS) * g + b).astype(x.dtype)


def _rms_norm(x, w):
    xf = x.astype(F32)
    return (xf * lax.rsqrt(jnp.mean(xf * xf, -1, keepdims=True) + NORM_EPS) * w).astype(x.dtype)


def _l2norm(x):
    xf = x.astype(F32)
    return xf * lax.rsqrt(jnp.sum(xf * xf, -1, keepdims=True) + NORM_EPS)


def _causal_conv_silu(x, prefix, w, b):
    xp = jnp.concatenate([prefix.astype(x.dtype), x], axis=1)
    y = lax.conv_general_dilated(xp, w[:, None, :].astype(x.dtype), window_strides=(1,), padding='VALID',
                                 dimension_numbers=('NWC', 'WIO', 'NWC'), feature_group_count=x.shape[-1])
    if b is not None:
        y = y + b
    return jax.nn.silu(y), xp[:, -(CONV_W - 1):]


def _rope_partial(x, pos):
    half = ROPE_DIM // 2
    inv_freq = ROPE_THETA ** (-jnp.arange(half, dtype=F32) / half)
    ang = pos.astype(F32)[:, None] * inv_freq[None, :]
    cos = jnp.cos(ang)[None, :, None, :]
    sin = jnp.sin(ang)[None, :, None, :]
    xr = x[..., :ROPE_DIM].astype(F32)
    x1, x2 = xr[..., :half], xr[..., half:]
    rot = jnp.concatenate([x1 * cos - x2 * sin, x2 * cos + x1 * sin], -1)
    return jnp.concatenate([rot.astype(x.dtype), x[..., ROPE_DIM:]], -1)


def _pad_len(t, c):
    pad = (-t.shape[1]) % c
    return jnp.pad(t, [(0, 0), (0, pad)] + [(0, 0)] * (t.ndim - 2))


def _to_chunks(t, c):
    b, l = t.shape[:2]
    t = t.reshape((b, l // c, c) + t.shape[2:])
    return jnp.swapaxes(jnp.moveaxis(t, 1, 0), 2, 3)


def _from_chunks(t):
    t = jnp.moveaxis(jnp.swapaxes(t, 2, 3), 0, 1)
    return t.reshape((t.shape[0], -1) + t.shape[3:])


def _chunk_decay(lc):
    c = lc.shape[-1]
    causal = jnp.arange(c)[:, None] >= jnp.arange(c)[None, :]
    return jnp.exp(jnp.where(causal, lc[..., :, None] - lc[..., None, :], -jnp.inf))


def _gated_delta_rule(q, k, v, g, beta, s0):
    L = q.shape[1]
    c = min(CHUNK, L)
    q, k, v, g, beta = [_to_chunks(_pad_len(t.astype(F32), c), c) for t in (q, k, v, g, beta)]
    lc = jnp.cumsum(g, axis=-1)
    decay = _chunk_decay(lc)
    kb = k * beta[..., None]
    strict = jnp.arange(c)[:, None] > jnp.arange(c)[None, :]
    a_mat = jnp.where(strict, jnp.einsum('zbhid,zbhjd->zbhij', kb, k) * decay, 0.0) + jnp.eye(c, dtype=F32)
    rhs = jnp.concatenate([v * beta[..., None], kb * jnp.exp(lc)[..., None]], axis=-1)
    sol = lax.linalg.triangular_solve(a_mat, rhs, left_side=True, lower=True, unit_diagonal=True)
    u, w = sol[..., :DN_DV], sol[..., DN_DV:]
    qk = jnp.einsum('zbhid,zbhjd->zbhij', q, k) * decay

    def step(s, xs):
        qc, kc, uc, wc, qkc, lcc = xs
        v_new = uc - jnp.einsum('bhck,bhkv->bhcv', wc, s)
        o = (jnp.einsum('bhck,bhkv->bhcv', qc * jnp.exp(lcc)[..., None], s)
             + jnp.einsum('bhij,bhjv->bhiv', qkc, v_new))
        last = lcc[..., -1:]
        s = s * jnp.exp(last)[..., None] + jnp.einsum('bhck,bhcv->bhkv', kc * jnp.exp(last - lcc)[..., None], v_new)
        return s, o

    s, o = lax.scan(step, s0.astype(F32), (q, k, u, w, qk, lc))
    return _from_chunks(o)[:, :L], s


def _ssd_scan(x, dt, a, bm, cm, h0):
    L = x.shape[1]
    c = min(CHUNK, L)
    xd = _to_chunks(_pad_len(x.astype(F32) * dt[..., None], c), c)
    la = _to_chunks(_pad_len(dt * a, c), c)
    bm = _to_chunks(_pad_len(bm.astype(F32), c), c)
    cm = _to_chunks(_pad_len(cm.astype(F32), c), c)
    lc = jnp.cumsum(la, axis=-1)
    cb = jnp.einsum('zbhis,zbhjs->zbhij', cm, bm) * _chunk_decay(lc)

    def step(h, xs):
        xc, bc, cc, cbc, lcc = xs
        y = (jnp.einsum('bhij,bhjp->bhip', cbc, xc)
             + jnp.einsum('bhis,bhps->bhip', cc * jnp.exp(lcc)[..., None], h))
        last = lcc[..., -1:]
        h = h * jnp.exp(last)[..., None] + jnp.einsum('bhjs,bhjp->bhps', bc * jnp.exp(last - lcc)[..., None], xc)
        return h, y

    h, y = lax.scan(step, h0.astype(F32), (xd, bm, cm, cb, lc))
    return _from_chunks(y)[:, :L], h


def _sink_softmax(s, sinks):
    sk = jnp.broadcast_to(sinks.astype(F32).reshape(SWA_KV_HEADS, -1)[:, :, None, None], s.shape[:-1] + (1,))
    return jax.nn.softmax(jnp.concatenate([s, sk], axis=-1), axis=-1)[..., :-1]


def _swa_prompt(q, k, v, sinks):
    bsz, L = q.shape[:2]
    nb = L // WINDOW
    grp = SWA_HEADS // SWA_KV_HEADS
    qb = q.reshape(bsz, nb, WINDOW, SWA_KV_HEADS, grp, SWA_HEAD_DIM)

    def band(t):
        tp = jnp.pad(t, ((0, 0), (WINDOW, 0), (0, 0), (0, 0))).reshape(bsz, nb + 1, WINDOW, SWA_KV_HEADS, SWA_HEAD_DIM)
        return jnp.concatenate([tp[:, :-1], tp[:, 1:]], axis=2)

    kb, vb = band(k), band(v)
    qi = jnp.arange(WINDOW)[:, None] + WINDOW
    kj = jnp.arange(2 * WINDOW)[None, :]
    kpos = jnp.arange(nb)[:, None, None] * WINDOW - WINDOW + kj[None]
    valid = (kj <= qi) & (qi - kj < WINDOW) & (kpos >= 0)
    s = jnp.einsum('bnqkgd,bnskd->bnkgqs', qb, kb, preferred_element_type=F32) * SWA_HEAD_DIM ** -0.5
    s = jnp.where(valid[None, :, None, None], s, -jnp.inf)
    p = _sink_softmax(s, sinks)
    o = jnp.einsum('bnkgqs,bnskd->bnqkgd', p.astype(v.dtype), vb)
    return o.reshape(bsz, L, SWA_HEADS * SWA_HEAD_DIM)


def _swa_sample(q, k, v, k_buf, v_buf, sinks):
    bsz, L = q.shape[:2]
    buf_len = k_buf.shape[1]
    grp = SWA_HEADS // SWA_KV_HEADS
    kall = jnp.concatenate([k_buf.astype(k.dtype), k], axis=1)
    vall = jnp.concatenate([v_buf.astype(v.dtype), v], axis=1)
    qi = jnp.arange(L)[:, None] + buf_len
    kj = jnp.arange(buf_len + L)[None, :]
    valid = (kj <= qi) & (qi - kj < WINDOW)
    qg = q.reshape(bsz, L, SWA_KV_HEADS, grp, SWA_HEAD_DIM)
    s = jnp.einsum('bqkgd,bskd->bkgqs', qg, kall, preferred_element_type=F32) * SWA_HEAD_DIM ** -0.5
    s = jnp.where(valid, s, -jnp.inf)
    p = _sink_softmax(s, sinks)
    o = jnp.einsum('bkgqs,bskd->bqkgd', p.astype(v.dtype), vall).reshape(bsz, L, SWA_HEADS * SWA_HEAD_DIM)
    return o, kall[:, -WINDOW:], vall[:, -WINDOW:]


def _token_mixer(h, pos, dn_conv0, dn_s0, ssm_conv0, ssm_h0, kv_buf, p):
    bsz, L, _ = h.shape
    proj = h @ p['w_in']
    (dn_qkv, dn_gate, dn_a, dn_b, ssm_z, ssm_xbc, ssm_dt, sw_q, sw_k, sw_v, merge_g) = jnp.split(
        proj, _split_points(IN_SPLITS), axis=-1)
    qkv, dn_conv = _causal_conv_silu(dn_qkv, dn_conv0, p['dn_conv_w'], None)
    dq, dk, dv = jnp.split(qkv, [DN_KEY, 2 * DN_KEY], axis=-1)
    dq = _l2norm(dq.reshape(bsz, L, DN_HEADS, DN_DK)) * DN_DK ** -0.5
    dk = _l2norm(dk.reshape(bsz, L, DN_HEADS, DN_DK))
    dv = dv.reshape(bsz, L, DN_HEADS, DN_DV)
    beta = jax.nn.sigmoid(dn_b.astype(F32))
    g = -jnp.exp(p['dn_a_log'].astype(F32)) * jax.nn.softplus(dn_a.astype(F32) + p['dn_dt_bias'])
    o_a, dn_s = _gated_delta_rule(dq, dk, dv, g, beta, dn_s0)
    o_a = (_rms_norm(o_a, p['dn_norm_w'])
           * jax.nn.silu(dn_gate.reshape(bsz, L, DN_HEADS, DN_DV).astype(F32))).reshape(bsz, L, DN_VAL)
    xbc, ssm_conv = _causal_conv_silu(ssm_xbc, ssm_conv0, p['ssm_conv_w'], p['ssm_conv_b'])
    sx, sb, sc = jnp.split(xbc, [SSM_INNER, SSM_INNER + SSM_GROUPS * SSM_STATE], axis=-1)
    rep = SSM_HEADS // SSM_GROUPS
    sx = sx.reshape(bsz, L, SSM_HEADS, SSM_HEADDIM)
    sb = jnp.repeat(sb.reshape(bsz, L, SSM_GROUPS, SSM_STATE), rep, axis=2)
    sc = jnp.repeat(sc.reshape(bsz, L, SSM_GROUPS, SSM_STATE), rep, axis=2)
    dt = jax.nn.softplus(ssm_dt.astype(F32) + p['ssm_dt_bias'])
    a = -jnp.exp(p['ssm_a_log'].astype(F32))
    y, ssm_h = _ssd_scan(sx, dt, a, sb, sc, ssm_h0)
    y = y + sx.astype(F32) * p['ssm_d'][:, None]
    y = y.reshape(bsz, L, SSM_INNER) * jax.nn.silu(ssm_z.astype(F32))
    o_b = _rms_norm(y.reshape(bsz, L, SSM_GROUPS, -1),
                    p['ssm_norm_w'].reshape(SSM_GROUPS, -1)).reshape(bsz, L, SSM_INNER)
    aq = _rope_partial(sw_q.reshape(bsz, L, SWA_HEADS, SWA_HEAD_DIM), pos)
    ak = _rope_partial(sw_k.reshape(bsz, L, SWA_KV_HEADS, SWA_HEAD_DIM), pos)
    av = sw_v.reshape(bsz, L, SWA_KV_HEADS, SWA_HEAD_DIM)
    if kv_buf is None:
        o_c = _swa_prompt(aq, ak, av, p['swa_sinks'])
        k_new, v_new = ak[:, -WINDOW:], av[:, -WINDOW:]
    else:
        o_c, k_new, v_new = _swa_sample(aq, ak, av, kv_buf[0], kv_buf[1], p['swa_sinks'])
    branches = jnp.stack([o_a.astype(h.dtype), o_b.astype(h.dtype), o_c.astype(h.dtype)], axis=2)
    u = jnp.einsum('blkc,kcd->blkd', branches, p['w_branch'])
    gates = jax.nn.sigmoid(merge_g.astype(F32)).reshape(bsz, L, N_BRANCH, D_MODEL).astype(h.dtype)
    merged = jnp.einsum('blkd,blkd->bld', gates, u)
    return merged @ p['w_out'], (dn_conv, dn_s, ssm_conv, ssm_h, k_new, v_new)


def _moe(h, p):
    bsz, L, d = h.shape
    t = h.reshape(-1, d)
    n_tok = t.shape[0]
    scores = jax.nn.sigmoid((t @ p['w_router']).astype(F32))
    choice = scores + p['router_bias'].astype(F32)
    grp_score = lax.top_k(choice.reshape(n_tok, N_GROUPS, -1), 2)[0].sum(-1)
    _, grp_idx = lax.top_k(grp_score, TOPK_GROUPS)
    grp_keep = jax.nn.one_hot(grp_idx, N_GROUPS, dtype=F32).sum(1) > 0
    keep = jnp.repeat(grp_keep, N_EXPERTS // N_GROUPS, axis=1)
    _, exp_idx = lax.top_k(jnp.where(keep, choice, -jnp.inf), TOP_K)
    w_sel = jnp.take_along_axis(scores, exp_idx, axis=1)
    w_sel = w_sel / jnp.sum(w_sel, -1, keepdims=True) * ROUTED_SCALE
    combine = jnp.einsum('tke,tk->te', jax.nn.one_hot(exp_idx, N_EXPERTS, dtype=F32), w_sel)
    blk = min(MOE_BLOCK, n_tok)
    pad = (-n_tok) % blk
    tb = jnp.pad(t, ((0, pad), (0, 0))).reshape(-1, blk, d)
    cb = jnp.pad(combine, ((0, pad), (0, 0))).reshape(-1, blk, N_EXPERTS).astype(t.dtype)

    def expert_block(args):
        tt, cc = args
        act = (jax.nn.silu(jnp.einsum('td,edf->tef', tt, p['w_exp_gate']))
               * jnp.einsum('td,edf->tef', tt, p['w_exp_up']))
        return jnp.einsum('tef,te,efd->td', act, cc, p['w_exp_down'])

    routed = lax.map(expert_block, (tb, cb)).reshape(-1, d)[:n_tok]
    shared = (jax.nn.silu(t @ p['w_sh_gate']) * (t @ p['w_sh_up'])) @ p['w_sh_down']
    return (routed + shared).reshape(bsz, L, d)


def _layer(x, c, pos, states, p):
    mod = jax.nn.silu(c) @ p['w_ada'] + p['b_ada']
    sh1, sc1, g1, sh2, sc2, g2 = jnp.split(mod[:, None, :], 6, axis=-1)
    h = x * (1 + sc1) + sh1
    mix, new_states = _token_mixer(h, pos, *states, p)
    x = _layer_norm(ALPHA * x + g1 * mix, p['ln1_g'], p['ln1_b'])
    h = x * (1 + sc2) + sh2
    x = _layer_norm(ALPHA * x + g2 * _moe(h, p), p['ln2_g'], p['ln2_b'])
    return x, new_states


def setup_inputs(seed: int = 0) -> dict:
    key = jax.random.key(seed)
    keys = jax.random.split(key, 48)
    kit = iter([keys[i] for i in range(48)])

    def nrm(shape, scale):
        return jax.random.normal(next(kit), shape, F32) * scale

    def dt_bias_init(shape):
        dt = jnp.exp(jax.random.uniform(next(kit), shape, F32, np.log(1e-3), np.log(1e-1)))
        return dt + jnp.log(-jnp.expm1(-dt))

    def a_log_init(shape):
        return jnp.log(jax.random.uniform(next(kit), shape, F32, 1.0, 16.0))

    d = D_MODEL
    return {
        'x_prompt': nrm((BATCH, SEQ, d), 1.0),
        'x_sample': nrm((DEC_BATCH, DEC_SEQ, d), 1.0),
        'state_dn_conv': nrm((DEPTH, DEC_BATCH, CONV_W - 1, DN_CONV_DIM), 1.0),
        'state_dn': nrm((DEPTH, DEC_BATCH, DN_HEADS, DN_DK, DN_DV), 0.5),
        'state_ssm_conv': nrm((DEPTH, DEC_BATCH, CONV_W - 1, SSM_CONV_DIM), 1.0),
        'state_ssm': nrm((DEPTH, DEC_BATCH, SSM_HEADS, SSM_HEADDIM, SSM_STATE), 0.5),
        'cache_swa_k': nrm((DEPTH, DEC_BATCH, WINDOW, SWA_KV_HEADS, SWA_HEAD_DIM), 1.0),
        'cache_swa_v': nrm((DEPTH, DEC_BATCH, WINDOW, SWA_KV_HEADS, SWA_HEAD_DIM), 1.0),
        'c_prompt': nrm((BATCH, d), 1.0),
        'c_sample': nrm((DEC_BATCH, d), 1.0),
        'w_ada': nrm((DEPTH, d, 6 * d), 0.5 * d ** -0.5),
        'b_ada': nrm((DEPTH, 6 * d), 0.02),
        'w_in': nrm((DEPTH, d, IN_DIM), d ** -0.5),
        'dn_conv_w': nrm((DEPTH, CONV_W, DN_CONV_DIM), CONV_W ** -0.5),
        'dn_a_log': a_log_init((DEPTH, DN_HEADS)),
        'dn_dt_bias': dt_bias_init((DEPTH, DN_HEADS)),
        'dn_norm_w': 1.0 + nrm((DEPTH, DN_DV), 0.02),
        'ssm_conv_w': nrm((DEPTH, CONV_W, SSM_CONV_DIM), CONV_W ** -0.5),
        'ssm_conv_b': nrm((DEPTH, SSM_CONV_DIM), 0.02),
        'ssm_a_log': a_log_init((DEPTH, SSM_HEADS)),
        'ssm_dt_bias': dt_bias_init((DEPTH, SSM_HEADS)),
        'ssm_d': 1.0 + nrm((DEPTH, SSM_HEADS), 0.1),
        'ssm_norm_w': 1.0 + nrm((DEPTH, SSM_INNER), 0.02),
        'swa_sinks': nrm((DEPTH, SWA_HEADS), 0.5),
        'w_branch': nrm((DEPTH, N_BRANCH, BRANCH_WIDTH, d), BRANCH_WIDTH ** -0.5),
        'w_out': nrm((DEPTH, d, d), BETA_INIT * d ** -0.5),
        'ln1_g': 1.0 + nrm((DEPTH, d), 0.02),
        'ln1_b': nrm((DEPTH, d), 0.02),
        'w_router': nrm((DEPTH, d, N_EXPERTS), d ** -0.5),
        'router_bias': nrm((DEPTH, N_EXPERTS), 0.01),
        'w_exp_gate': nrm((DEPTH, N_EXPERTS, d, EXPERT_FF), d ** -0.5),
        'w_exp_up': nrm((DEPTH, N_EXPERTS, d, EXPERT_FF), d ** -0.5),
        'w_exp_down': nrm((DEPTH, N_EXPERTS, EXPERT_FF, d), BETA_INIT * EXPERT_FF ** -0.5),
        'w_sh_gate': nrm((DEPTH, d, SHARED_FF), d ** -0.5),
        'w_sh_up': nrm((DEPTH, d, SHARED_FF), d ** -0.5),
        'w_sh_down': nrm((DEPTH, SHARED_FF, d), BETA_INIT * SHARED_FF ** -0.5),
        'ln2_g': 1.0 + nrm((DEPTH, d), 0.02),
        'ln2_b': nrm((DEPTH, d), 0.02),
    }


def reference(x_prompt, x_sample, state_dn_conv, state_dn, state_ssm_conv, state_ssm, cache_swa_k, cache_swa_v,
              c_prompt, c_sample, w_ada, b_ada, w_in, dn_conv_w, dn_a_log, dn_dt_bias, dn_norm_w,
              ssm_conv_w, ssm_conv_b, ssm_a_log, ssm_dt_bias, ssm_d, ssm_norm_w, swa_sinks, w_branch, w_out,
              ln1_g, ln1_b, w_router, router_bias, w_exp_gate, w_exp_up, w_exp_down, w_sh_gate, w_sh_up,
              w_sh_down, ln2_g, ln2_b):
    bp = x_prompt.shape[0]
    pos_p = jnp.arange(x_prompt.shape[1])
    pos_s = PAST_LEN + jnp.arange(x_sample.shape[1])
    init_p = (jnp.zeros((bp, CONV_W - 1, DN_CONV_DIM), x_prompt.dtype),
              jnp.zeros((bp, DN_HEADS, DN_DK, DN_DV), F32),
              jnp.zeros((bp, CONV_W - 1, SSM_CONV_DIM), x_prompt.dtype),
              jnp.zeros((bp, SSM_HEADS, SSM_HEADDIM, SSM_STATE), F32),
              None)
    yp, ys = x_prompt, x_sample
    new_p, new_s = [], []
    for l in range(DEPTH):
        p = {'w_ada': w_ada[l], 'b_ada': b_ada[l], 'w_in': w_in[l], 'dn_conv_w': dn_conv_w[l],
             'dn_a_log': dn_a_log[l], 'dn_dt_bias': dn_dt_bias[l], 'dn_norm_w': dn_norm_w[l],
             'ssm_conv_w': ssm_conv_w[l], 'ssm_conv_b': ssm_conv_b[l], 'ssm_a_log': ssm_a_log[l],
             'ssm_dt_bias': ssm_dt_bias[l], 'ssm_d': ssm_d[l], 'ssm_norm_w': ssm_norm_w[l],
             'swa_sinks': swa_sinks[l], 'w_branch': w_branch[l], 'w_out': w_out[l],
             'ln1_g': ln1_g[l], 'ln1_b': ln1_b[l], 'w_router': w_router[l], 'router_bias': router_bias[l],
             'w_exp_gate': w_exp_gate[l], 'w_exp_up': w_exp_up[l], 'w_exp_down': w_exp_down[l],
             'w_sh_gate': w_sh_gate[l], 'w_sh_up': w_sh_up[l], 'w_sh_down': w_sh_down[l],
             'ln2_g': ln2_g[l], 'ln2_b': ln2_b[l]}
        yp, st_p = _layer(yp, c_prompt, pos_p, init_p, p)
        st_in = (state_dn_conv[l], state_dn[l], state_ssm_conv[l], state_ssm[l], (cache_swa_k[l], cache_swa_v[l]))
        ys, st_s = _layer(ys, c_sample, pos_s, st_in, p)
        new_p.append(st_p)
        new_s.append(st_s)
    dn_conv_p = jnp.stack([s[0] for s in new_p]).astype(state_dn_conv.dtype)
    dn_conv_s = jnp.stack([s[0] for s in new_s]).astype(state_dn_conv.dtype)
    dn_state_p = jnp.stack([s[1] for s in new_p]).astype(state_dn.dtype)
    dn_state_s = jnp.stack([s[1] for s in new_s]).astype(state_dn.dtype)
    ssm_conv_p = jnp.stack([s[2] for s in new_p]).astype(state_ssm_conv.dtype)
    ssm_conv_s = jnp.stack([s[2] for s in new_s]).astype(state_ssm_conv.dtype)
    ssm_state_p = jnp.stack([s[3] for s in new_p]).astype(state_ssm.dtype)
    ssm_state_s = jnp.stack([s[3] for s in new_s]).astype(state_ssm.dtype)
    swa_k_p = jnp.stack([s[4] for s in new_p]).astype(cache_swa_k.dtype)
    swa_k_s = jnp.stack([s[4] for s in new_s]).astype(cache_swa_k.dtype)
    swa_v_p = jnp.stack([s[5] for s in new_p]).astype(cache_swa_v.dtype)
    swa_v_s = jnp.stack([s[5] for s in new_s]).astype(cache_swa_v.dtype)
    return (yp, ys, dn_conv_p, dn_conv_s, dn_state_p, dn_state_s, ssm_conv_p, ssm_conv_s,
            ssm_state_p, ssm_state_s, swa_k_p, swa_k_s, swa_v_p, swa_v_s)
```

```python
import functools
import math

import jax
import jax.numpy as jnp
from jax import lax
from jax.experimental import pallas as pl
from jax.experimental.pallas import tpu as pltpu

F32 = jnp.float32
BF16 = jnp.bfloat16
I32 = jnp.int32

D_MODEL = 2048
CONV_W = 4
DN_HEADS = 8
DN_D = 128
DN_KEY = DN_HEADS * DN_D
DN_CONV_DIM = 3 * DN_KEY
SSM_HEADS = 16
SSM_P = 64
SSM_INNER = SSM_HEADS * SSM_P
SSM_GROUPS = 2
SSM_N = 128
SSM_CONV_DIM = SSM_INNER + 2 * SSM_GROUPS * SSM_N
SWA_HEADS = 16
SWA_KV = 4
SWA_D = 64
SWA_GRP = SWA_HEADS // SWA_KV
WINDOW = 128
ROPE_DIM = SWA_D // 4
ROPE_THETA = 500000.0
N_BRANCH = 3
BRANCH_W = 1024
N_EXPERTS = 64
EXPERT_FF = 512
TOP_K = 8
N_GROUPS = 8
GROUP_SIZE = N_EXPERTS // N_GROUPS
TOPK_GROUPS = 4
ROUTED_SCALE = 2.5
LN_EPS = 1e-5
NORM_EPS = 1e-6
CHUNK = 64

LANES = 128
SUBLANES = 8
VMEM_LIMIT = 56 * 1024 * 1024

MAIN_W = 14336
COL_MG, COL_QKV, COL_XBC, COL_SWK, COL_SWV, COL_DNG, COL_SSZ, COL_SWQ = (
    0, 6144, 9216, 10752, 11008, 11264, 12288, 13312)
SMALL_W = LANES
SM_A, SM_B, SM_DT = 0, 8, 16

MOE_TM = 256


def _cparams(sem, vmem=VMEM_LIMIT):
    return pltpu.CompilerParams(dimension_semantics=sem, vmem_limit_bytes=vmem)


def _dot(a, b):
    return jnp.dot(a, b, preferred_element_type=F32)


def _dot_nt(a, b):
    return lax.dot_general(a, b, (((1,), (1,)), ((), ())), preferred_element_type=F32)


def _dot_tn(a, b):
    return lax.dot_general(a, b, (((0,), (0,)), ((), ())), preferred_element_type=F32)


def _split(a):
    hi = a.astype(BF16)
    lo = (a - hi.astype(F32)).astype(BF16)
    return hi, lo


def _dot3(a, b, dot=_dot):
    ah, al = _split(a)
    bh, bl = _split(b)
    return dot(ah, bh) + dot(ah, bl) + dot(al, bh)


def _bdot(a, b, dot=_dot):
    return dot(a.astype(BF16), b.astype(BF16))


def _silu(x):
    return x * (1.0 / (1.0 + jnp.exp(-x)))


def _sigmoid(x):
    return 1.0 / (1.0 + jnp.exp(-x))


def _softplus(x):
    return jnp.maximum(x, 0.0) + jnp.log(1.0 + jnp.exp(-jnp.abs(x)))


def _ada_kernel(c_ref, w_ref, b_ref, o_ref):
    c = c_ref[...]
    o_ref[...] = _bdot(_silu(c), w_ref[...]) + b_ref[...]


def _ada(c_all, w_ada, b_ada):
    depth, d, n = w_ada.shape
    rows = c_all.shape[0]
    tn = 1024
    return pl.pallas_call(
        _ada_kernel,
        grid=(depth, n // tn),
        in_specs=[pl.BlockSpec((rows, d), lambda l, j: (0, 0)),
                  pl.BlockSpec((None, d, tn), lambda l, j: (l, 0, j)),
                  pl.BlockSpec((None, 1, tn), lambda l, j: (l, 0, j))],
        out_specs=pl.BlockSpec((None, rows, tn), lambda l, j: (l, 0, j)),
        out_shape=jax.ShapeDtypeStruct((depth, rows, n), F32),
        compiler_params=_cparams(("arbitrary", "arbitrary")),
        name="ada",
    )(c_all, w_ada, b_ada.reshape(depth, 1, n))


def _mod_spec(mod, tm, tiles_per_group):
    r = mod.shape[1]
    if r == 1:
        return pl.BlockSpec((None, 1, mod.shape[2]), lambda i, *_: (i // tiles_per_group, 0, 0))
    return pl.BlockSpec((None, r, mod.shape[2]), lambda i, *_: (i, 0, 0))


def _in_proj_kernel(x_ref, sc_ref, sh_ref, wm_ref, ws_ref, proj_ref, small_ref, hb_ref):
    @pl.when(pl.program_id(1) == 0)
    def _():
        h = x_ref[...] * (1.0 + sc_ref[...]) + sh_ref[...]
        hb_ref[...] = h.astype(BF16)
        small_ref[...] = _dot(hb_ref[...], ws_ref[...])

    proj_ref[...] = _dot(hb_ref[...], wm_ref[...])


def _in_proj(x, sc, sh, w_main, w_small, tm, tiles_per_group):
    t, d = x.shape
    tn = 1024
    return pl.pallas_call(
        _in_proj_kernel,
        grid=(t // tm, MAIN_W // tn),
        in_specs=[pl.BlockSpec((tm, d), lambda i, j: (i, 0)),
                  _mod_spec(sc, tm, tiles_per_group),
                  _mod_spec(sh, tm, tiles_per_group),
                  pl.BlockSpec((d, tn), lambda i, j: (0, j)),
                  pl.BlockSpec((d, SMALL_W), lambda i, j: (0, 0))],
        out_specs=[pl.BlockSpec((tm, tn), lambda i, j: (i, j)),
                   pl.BlockSpec((tm, SMALL_W), lambda i, j: (i, 0))],
        out_shape=[jax.ShapeDtypeStruct((t, MAIN_W), F32),
                   jax.ShapeDtypeStruct((t, SMALL_W), F32)],
        scratch_shapes=[pltpu.VMEM((tm, d), BF16)],
        compiler_params=_cparams(("arbitrary", "arbitrary")),
        name="in_proj",
    )(x, sc, sh, w_main, w_small)


def _iota2(shape, dim):
    return lax.broadcasted_iota(I32, shape, dim)


def _conv_silu(xp_ref, x_ref, w_ref, bias, c):
    xp_ref[SUBLANES:SUBLANES + c, :] = x_ref[...]
    base = SUBLANES - (CONV_W - 1)
    y = xp_ref[base:base + c, :] * w_ref[0:1, :]
    for j in range(1, CONV_W):
        y = y + xp_ref[base + j:base + j + c, :] * w_ref[j:j + 1, :]
    if bias is not None:
        y = y + bias
    return _silu(y)


def _tri_inverse(a, c):
    row = _iota2((c, c), 0)
    col = _iota2((c, c), 1)
    t = jnp.where(row == col, 1.0, 0.0).astype(F32)
    s = 1
    while s < c:
        m = ((row // s) % 2 == 1) & ((col // s) % 2 == 0) & (row // (2 * s) == col // (2 * s))
        am = jnp.where(m, a, 0.0)
        t = t - _dot3(_dot3(t, am), t)
        s *= 2
    return t


def _dn_kernel(qkv_ref, gate_ref, sm_ref, smt_ref, conv0_ref, s0_ref, cw_ref, prow_ref, pcol_ref, nw_ref,
               o_ref, conv_out_ref, s_out_ref, xp_ref, s_ref, *, c, valid):
    n = pl.program_id(1)
    base = SUBLANES - (CONV_W - 1)

    @pl.when(n == 0)
    def _():
        xp_ref[base:SUBLANES, :] = conv0_ref[...]
        s_ref[...] = s0_ref[...]

    y = _conv_silu(xp_ref, qkv_ref, cw_ref, None, c)
    tail = xp_ref[SUBLANES + valid - (CONV_W - 1):SUBLANES + valid, :]
    conv_out_ref[...] = tail
    xp_ref[base:SUBLANES, :] = tail

    row_ok = _iota2((c, 1), 0) < valid
    col_ok = _iota2((1, c), 1) < valid
    sm = sm_ref[...]
    g_all = -jnp.exp(prow_ref[0:1, :]) * _softplus(sm + prow_ref[1:2, :])
    g_all = jnp.where(row_ok, g_all, 0.0)
    beta_all = jnp.where(row_ok, _sigmoid(sm), 0.0)
    smt = smt_ref[...]
    g_t = -jnp.exp(pcol_ref[:, 0:1]) * _softplus(smt[0:DN_HEADS, :] + pcol_ref[:, 1:2])
    g_t = jnp.where(col_ok, g_t, 0.0)
    row = _iota2((c, c), 0)
    col = _iota2((c, c), 1)
    tri = jnp.where(row >= col, 1.0, 0.0).astype(F32)
    lc_all = _dot3(tri, g_all)
    lc_t = _dot3(g_t, jnp.where(row <= col, 1.0, 0.0).astype(F32))
    causal = row >= col
    strict = row > col

    for h in range(DN_HEADS):
        sl = slice(h * DN_D, (h + 1) * DN_D)
        qh = y[:, sl]
        kh = y[:, DN_KEY + h * DN_D:DN_KEY + (h + 1) * DN_D]
        vh = y[:, 2 * DN_KEY + h * DN_D:2 * DN_KEY + (h + 1) * DN_D]
        qh = qh * lax.rsqrt(jnp.sum(qh * qh, -1, keepdims=True) + NORM_EPS) * (DN_D ** -0.5)
        kh = kh * lax.rsqrt(jnp.sum(kh * kh, -1, keepdims=True) + NORM_EPS)
        kh = jnp.where(row_ok, kh, 0.0)
        vh = jnp.where(row_ok, vh, 0.0)
        lc_c = lc_all[:, SM_A + h:SM_A + h + 1]
        lc_r = lc_t[h:h + 1, :]
        beta_c = beta_all[:, SM_B + h:SM_B + h + 1]
        decay = jnp.exp(jnp.where(causal, lc_c - lc_r, -jnp.inf))
        kb = kh * beta_c
        a = jnp.where(strict, _bdot(kb, kh, _dot_nt) * decay, 0.0)
        t = _tri_inverse(a, c)
        e_lc = jnp.exp(lc_c)
        u = _dot3(t, vh * beta_c)
        w = _dot3(t, kb * e_lc)
        qk = _bdot(qh, kh, _dot_nt) * decay
        sh = s_ref[h]
        v_new = u - _bdot(w, sh)
        o = _bdot(qh * e_lc, sh) + _bdot(qk, v_new)
        last = lc_all[c - 1:c, SM_A + h:SM_A + h + 1]
        s_new = sh * jnp.exp(last) + _bdot(kh * jnp.exp(last - lc_c), v_new, _dot_tn)
        s_ref[h] = s_new
        s_out_ref[h] = s_new
        o = o * lax.rsqrt(jnp.mean(o * o, -1, keepdims=True) + NORM_EPS) * nw_ref[...]
        o_ref[:, sl] = (o * _silu(gate_ref[:, sl])).astype(o_ref.dtype)


def _dn(qkv, gate, sm, smt, conv0, s0, conv_w, prow, pcol, norm_w, *, c, valid, cb_qkv, cb_gate, out_dtype):
    b, lp = qkv.shape[0], qkv.shape[1]
    nch = lp // c
    kern = functools.partial(_dn_kernel, c=c, valid=valid)
    return pl.pallas_call(
        kern,
        grid=(b, nch),
        in_specs=[pl.BlockSpec((None, c, DN_CONV_DIM), lambda i, n: (i, n, cb_qkv)),
                  pl.BlockSpec((None, c, DN_KEY), lambda i, n: (i, n, cb_gate)),
                  pl.BlockSpec((None, c, SMALL_W), lambda i, n: (i, n, 0)),
                  pl.BlockSpec((None, None, 32, c), lambda i, n: (i, n, 0, 0)),
                  pl.BlockSpec((None, CONV_W - 1, DN_CONV_DIM), lambda i, n: (i, 0, 0)),
                  pl.BlockSpec((None, DN_HEADS, DN_D, DN_D), lambda i, n: (i, 0, 0, 0)),
                  pl.BlockSpec((CONV_W, DN_CONV_DIM), lambda i, n: (0, 0)),
                  pl.BlockSpec((2, SMALL_W), lambda i, n: (0, 0)),
                  pl.BlockSpec((DN_HEADS, 2), lambda i, n: (0, 0)),
                  pl.BlockSpec((1, DN_D), lambda i, n: (0, 0))],
        out_specs=[pl.BlockSpec((None, c, DN_KEY), lambda i, n: (i, n, 0)),
                   pl.BlockSpec((None, CONV_W - 1, DN_CONV_DIM), lambda i, n: (i, 0, 0)),
                   pl.BlockSpec((None, DN_HEADS, DN_D, DN_D), lambda i, n: (i, 0, 0, 0))],
        out_shape=[jax.ShapeDtypeStruct((b, lp, DN_KEY), out_dtype),
                   jax.ShapeDtypeStruct((b, CONV_W - 1, DN_CONV_DIM), F32),
                   jax.ShapeDtypeStruct((b, DN_HEADS, DN_D, DN_D), F32)],
        scratch_shapes=[pltpu.VMEM((c + SUBLANES, DN_CONV_DIM), F32),
                        pltpu.VMEM((DN_HEADS, DN_D, DN_D), F32)],
        compiler_params=_cparams(("arbitrary", "arbitrary")),
        name="delta_rule",
    )(qkv, gate, sm, smt, conv0, s0, conv_w, prow, pcol, norm_w)


def _ssd_kernel(z_ref, xbc_ref, sm_ref, smt_ref, conv0_ref, h0_ref, cw_ref, cb_ref, prow_ref, pcol_ref, nw_ref,
                o_ref, conv_out_ref, h_out_ref, xp_ref, h_ref, y_ref, *, c, valid):
    n = pl.program_id(1)
    base = SUBLANES - (CONV_W - 1)

    @pl.when(n == 0)
    def _():
        xp_ref[base:SUBLANES, :] = conv0_ref[...]
        h_ref[...] = h0_ref[...]

    act = _conv_silu(xp_ref, xbc_ref, cw_ref, cb_ref[...], c)
    tail = xp_ref[SUBLANES + valid - (CONV_W - 1):SUBLANES + valid, :]
    conv_out_ref[...] = tail
    xp_ref[base:SUBLANES, :] = tail

    row_ok = _iota2((c, 1), 0) < valid
    col_ok = _iota2((1, c), 1) < valid
    dt_all = _softplus(sm_ref[...] + prow_ref[0:1, :])
    la_all = jnp.where(row_ok, dt_all * -jnp.exp(prow_ref[1:2, :]), 0.0)
    dt_t = _softplus(smt_ref[SM_DT:SM_DT + SSM_HEADS, :] + pcol_ref[:, 0:1])
    la_t = jnp.where(col_ok, dt_t * -jnp.exp(pcol_ref[:, 1:2]), 0.0)
    row = _iota2((c, c), 0)
    col = _iota2((c, c), 1)
    lc_all = _dot3(jnp.where(row >= col, 1.0, 0.0).astype(F32), la_all)
    lc_t = _dot3(la_t, jnp.where(row <= col, 1.0, 0.0).astype(F32))
    causal = row >= col
    rep = SSM_HEADS // SSM_GROUPS

    for g in range(SSM_GROUPS):
        bg = act[:, SSM_INNER + g * SSM_N:SSM_INNER + (g + 1) * SSM_N]
        cg = act[:, SSM_INNER + (SSM_GROUPS + g) * SSM_N:SSM_INNER + (SSM_GROUPS + g + 1) * SSM_N]
        bg = jnp.where(row_ok, bg, 0.0)
        cbg = _bdot(cg, bg, _dot_nt)
        for hh in range(rep):
            h = g * rep + hh
            sl = slice(h * SSM_P, (h + 1) * SSM_P)
            xh = jnp.where(row_ok, act[:, sl], 0.0)
            dt_c = dt_all[:, SM_DT + h:SM_DT + h + 1]
            lc_c = lc_all[:, SM_DT + h:SM_DT + h + 1]
            lc_r = lc_t[h:h + 1, :]
            xd = xh * dt_c
            decay = jnp.exp(jnp.where(causal, lc_c - lc_r, -jnp.inf))
            st = h_ref[h]
            y = _bdot(cbg * decay, xd) + _bdot(cg * jnp.exp(lc_c), st, _dot_nt)
            last = lc_all[c - 1:c, SM_DT + h:SM_DT + h + 1]
            st_new = st * jnp.exp(last) + _bdot(xd, bg * jnp.exp(last - lc_c), _dot_tn)
            h_ref[h] = st_new
            h_out_ref[h] = st_new
            y = y + act[:, sl] * prow_ref[2:3, SM_DT + h:SM_DT + h + 1]
            y_ref[:, sl] = y * _silu(z_ref[:, sl])

    gw = SSM_INNER // SSM_GROUPS
    for g in range(SSM_GROUPS):
        yg = y_ref[:, g * gw:(g + 1) * gw]
        yg = yg * lax.rsqrt(jnp.mean(yg * yg, -1, keepdims=True) + NORM_EPS) * nw_ref[:, g * gw:(g + 1) * gw]
        o_ref[:, g * gw:(g + 1) * gw] = yg.astype(o_ref.dtype)


def _ssd(z, xbc, sm, smt, conv0, h0, conv_w, conv_b, prow, pcol, norm_w, *, c, valid, cb_z, cb_xbc, out_dtype):
    b, lp = z.shape[0], z.shape[1]
    nch = lp // c
    kern = functools.partial(_ssd_kernel, c=c, valid=valid)
    return pl.pallas_call(
        kern,
        grid=(b, nch),
        in_specs=[pl.BlockSpec((None, c, SSM_INNER), lambda i, n: (i, n, cb_z)),
                  pl.BlockSpec((None, c, SSM_CONV_DIM), lambda i, n: (i, n, cb_xbc)),
                  pl.BlockSpec((None, c, SMALL_W), lambda i, n: (i, n, 0)),
                  pl.BlockSpec((None, None, 32, c), lambda i, n: (i, n, 0, 0)),
                  pl.BlockSpec((None, CONV_W - 1, SSM_CONV_DIM), lambda i, n: (i, 0, 0)),
                  pl.BlockSpec((None, SSM_HEADS, SSM_P, SSM_N), lambda i, n: (i, 0, 0, 0)),
                  pl.BlockSpec((CONV_W, SSM_CONV_DIM), lambda i, n: (0, 0)),
                  pl.BlockSpec((1, SSM_CONV_DIM), lambda i, n: (0, 0)),
                  pl.BlockSpec((3, SMALL_W), lambda i, n: (0, 0)),
                  pl.BlockSpec((SSM_HEADS, 2), lambda i, n: (0, 0)),
                  pl.BlockSpec((1, SSM_INNER), lambda i, n: (0, 0))],
        out_specs=[pl.BlockSpec((None, c, SSM_INNER), lambda i, n: (i, n, 0)),
                   pl.BlockSpec((None, CONV_W - 1, SSM_CONV_DIM), lambda i, n: (i, 0, 0)),
                   pl.BlockSpec((None, SSM_HEADS, SSM_P, SSM_N), lambda i, n: (i, 0, 0, 0))],
        out_shape=[jax.ShapeDtypeStruct((b, lp, SSM_INNER), out_dtype),
                   jax.ShapeDtypeStruct((b, CONV_W - 1, SSM_CONV_DIM), F32),
                   jax.ShapeDtypeStruct((b, SSM_HEADS, SSM_P, SSM_N), F32)],
        scratch_shapes=[pltpu.VMEM((c + SUBLANES, SSM_CONV_DIM), F32),
                        pltpu.VMEM((SSM_HEADS, SSM_P, SSM_N), F32),
                        pltpu.VMEM((c, SSM_INNER), F32)],
        compiler_params=_cparams(("arbitrary", "arbitrary")),
        name="ssd_scan",
    )(z, xbc, sm, smt, conv0, h0, conv_w, conv_b, prow, pcol, norm_w)


def _rope_tables(pos, heads):
    half = ROPE_DIM // 2
    inv_freq = ROPE_THETA ** (-jnp.arange(half, dtype=F32) / half)
    ang = pos.astype(F32)[:, None] * inv_freq[None, :]
    cos, sin = jnp.cos(ang), jnp.sin(ang)
    n = pos.shape[0]
    pad = jnp.zeros((n, SWA_D - ROPE_DIM), F32)
    c_h = jnp.concatenate([cos, cos, pad + 1.0], -1)
    a_h = jnp.concatenate([-sin, jnp.zeros_like(sin), pad], -1)
    b_h = jnp.concatenate([jnp.zeros_like(sin), sin, pad], -1)
    return tuple(jnp.tile(t, (1, heads)) for t in (c_h, a_h, b_h))


def _rope(x, tc, ta, tb):
    half = ROPE_DIM // 2
    w = x.shape[-1]
    return x * tc + pltpu.roll(x, w - half, 1) * ta + pltpu.roll(x, half, 1) * tb


def _sink_attend(q, k, v, valid, sink):
    s = _bdot(q, k, _dot_nt) * (SWA_D ** -0.5)
    s = jnp.where(valid, s, -jnp.inf)
    m = jnp.maximum(jnp.max(s, -1, keepdims=True), sink)
    p = jnp.exp(s - m)
    den = jnp.sum(p, -1, keepdims=True) + jnp.exp(sink - m)
    return _bdot(p / den, v)


def _swa_prompt_kernel(q_ref, kp_ref, kc_ref, vp_ref, vc_ref, qt_ref, ktp_ref, ktc_ref, sink_ref,
                       o_ref, ko_ref, vo_ref):
    i = pl.program_id(1)
    w = WINDOW
    q = _rope(q_ref[...], qt_ref[0], qt_ref[1], qt_ref[2])
    kc = _rope(kc_ref[...], ktc_ref[0], ktc_ref[1], ktc_ref[2])
    kp = _rope(kp_ref[...], ktp_ref[0], ktp_ref[1], ktp_ref[2])
    vc = vc_ref[...]
    ko_ref[...] = kc
    vo_ref[...] = vc
    kk = jnp.concatenate([kp, kc], 0)
    vv = jnp.concatenate([vp_ref[...], vc], 0)
    qi = _iota2((w, 2 * w), 0) + w
    kj = _iota2((w, 2 * w), 1)
    valid = (kj <= qi) & (qi - kj < w) & ((kj >= w) | (i > 0))
    valid = jnp.concatenate([valid] * SWA_GRP, 0)
    for g in range(SWA_KV):
        kg = kk[:, g * SWA_D:(g + 1) * SWA_D]
        vg = vv[:, g * SWA_D:(g + 1) * SWA_D]
        qg = jnp.concatenate([q[:, (g * SWA_GRP + j) * SWA_D:(g * SWA_GRP + j + 1) * SWA_D]
                              for j in range(SWA_GRP)], 0)
        sk = jnp.concatenate([jnp.broadcast_to(sink_ref[0:1, g * SWA_GRP + j:g * SWA_GRP + j + 1], (w, 1))
                              for j in range(SWA_GRP)], 0)
        og = _sink_attend(qg, kg, vg, valid, sk)
        for j in range(SWA_GRP):
            hd = g * SWA_GRP + j
            o_ref[:, hd * SWA_D:(hd + 1) * SWA_D] = og[j * w:(j + 1) * w].astype(o_ref.dtype)


def _swa_prompt(proj, qt, kt, sinks):
    b, l = proj.shape[0], proj.shape[1]
    w = WINDOW
    nb = l // w
    kvw = SWA_KV * SWA_D
    qw = SWA_HEADS * SWA_D
    prev = lambda i, n: (i, jnp.maximum(n - 1, 0), COL_SWK // kvw)
    prev_v = lambda i, n: (i, jnp.maximum(n - 1, 0), COL_SWV // kvw)
    return pl.pallas_call(
        _swa_prompt_kernel,
        grid=(b, nb),
        in_specs=[pl.BlockSpec((None, w, qw), lambda i, n: (i, n, COL_SWQ // qw)),
                  pl.BlockSpec((None, w, kvw), prev),
                  pl.BlockSpec((None, w, kvw), lambda i, n: (i, n, COL_SWK // kvw)),
                  pl.BlockSpec((None, w, kvw), prev_v),
                  pl.BlockSpec((None, w, kvw), lambda i, n: (i, n, COL_SWV // kvw)),
                  pl.BlockSpec((3, w, qw), lambda i, n: (0, n, 0)),
                  pl.BlockSpec((3, w, kvw), lambda i, n: (0, jnp.maximum(n - 1, 0), 0)),
                  pl.BlockSpec((3, w, kvw), lambda i, n: (0, n, 0)),
                  pl.BlockSpec((1, SWA_HEADS), lambda i, n: (0, 0))],
        out_specs=[pl.BlockSpec((None, w, qw), lambda i, n: (i, n, 0)),
                   pl.BlockSpec((None, w, kvw), lambda i, n: (i, 0, 0)),
                   pl.BlockSpec((None, w, kvw), lambda i, n: (i, 0, 0))],
        out_shape=[jax.ShapeDtypeStruct((b, l, qw), BF16),
                   jax.ShapeDtypeStruct((b, w, kvw), F32),
                   jax.ShapeDtypeStruct((b, w, kvw), F32)],
        compiler_params=_cparams(("arbitrary", "arbitrary")),
        name="swa_prompt",
    )(proj, proj, proj, proj, proj, qt, kt, kt, sinks)


def _swa_sample_kernel(q_ref, k_ref, v_ref, kb_ref, vb_ref, qt_ref, kt_ref, sink_ref,
                       o_ref, ko_ref, vo_ref, kk_ref, vv_ref, *, l):
    w = WINDOW
    lp = SUBLANES
    q = _rope(q_ref[...], qt_ref[0], qt_ref[1], qt_ref[2])
    k = _rope(k_ref[...], kt_ref[0], kt_ref[1], kt_ref[2])
    kk_ref[0:w, :] = kb_ref[...]
    kk_ref[w:w + lp, :] = k
    vv_ref[0:w, :] = vb_ref[...]
    vv_ref[w:w + lp, :] = v_ref[...]
    ko_ref[...] = kk_ref[l:l + w, :]
    vo_ref[...] = vv_ref[l:l + w, :]
    kk = kk_ref[...]
    vv = vv_ref[...]
    qi = _iota2((lp, w + lp), 0) + w
    kj = _iota2((lp, w + lp), 1)
    valid = (kj <= qi) & (qi - kj < w) & (kj < w + l)
    valid = jnp.concatenate([valid] * SWA_GRP, 0)
    for g in range(SWA_KV):
        kg = kk[:, g * SWA_D:(g + 1) * SWA_D]
        vg = vv[:, g * SWA_D:(g + 1) * SWA_D]
        qg = jnp.concatenate([q[:, (g * SWA_GRP + j) * SWA_D:(g * SWA_GRP + j + 1) * SWA_D]
                              for j in range(SWA_GRP)], 0)
        sk = jnp.concatenate([jnp.broadcast_to(sink_ref[0:1, g * SWA_GRP + j:g * SWA_GRP + j + 1], (lp, 1))
                              for j in range(SWA_GRP)], 0)
        og = _sink_attend(qg, kg, vg, valid, sk)
        for j in range(SWA_GRP):
            hd = g * SWA_GRP + j
            o_ref[:, hd * SWA_D:(hd + 1) * SWA_D] = og[j * lp:(j + 1) * lp]


def _swa_sample(q, k, v, kbuf, vbuf, qt, kt, sinks, l):
    b = q.shape[0]
    w = WINDOW
    lp = SUBLANES
    kvw = SWA_KV * SWA_D
    qw = SWA_HEADS * SWA_D
    kern = functools.partial(_swa_sample_kernel, l=l)
    return pl.pallas_call(
        kern,
        grid=(b,),
        in_specs=[pl.BlockSpec((None, lp, qw), lambda i: (i, 0, 0)),
                  pl.BlockSpec((None, lp, kvw), lambda i: (i, 0, 0)),
                  pl.BlockSpec((None, lp, kvw), lambda i: (i, 0, 0)),
                  pl.BlockSpec((None, w, kvw), lambda i: (i, 0, 0)),
                  pl.BlockSpec((None, w, kvw), lambda i: (i, 0, 0)),
                  pl.BlockSpec((3, lp, qw), lambda i: (0, 0, 0)),
                  pl.BlockSpec((3, lp, kvw), lambda i: (0, 0, 0)),
                  pl.BlockSpec((1, SWA_HEADS), lambda i: (0, 0))],
        out_specs=[pl.BlockSpec((None, lp, qw), lambda i: (i, 0, 0)),
                   pl.BlockSpec((None, w, kvw), lambda i: (i, 0, 0)),
                   pl.BlockSpec((None, w, kvw), lambda i: (i, 0, 0))],
        out_shape=[jax.ShapeDtypeStruct((b, lp, qw), F32),
                   jax.ShapeDtypeStruct((b, w, kvw), F32),
                   jax.ShapeDtypeStruct((b, w, kvw), F32)],
        scratch_shapes=[pltpu.VMEM((w + lp, kvw), F32), pltpu.VMEM((w + lp, kvw), F32)],
        compiler_params=_cparams(("arbitrary",)),
        name="swa_sample",
    )(q, k, v, kbuf, vbuf, qt, kt, sinks)


def _merge_kernel(oa_ref, ob_ref, oc_ref, ga_ref, gb_ref, gc_ref, wa_ref, wb_ref, wc_ref, o_ref):
    acc = _sigmoid(ga_ref[...]) * _bdot(oa_ref[...], wa_ref[...])
    acc = acc + _sigmoid(gb_ref[...]) * _bdot(ob_ref[...], wb_ref[...])
    acc = acc + _sigmoid(gc_ref[...]) * _bdot(oc_ref[...], wc_ref[...])
    o_ref[...] = acc.astype(o_ref.dtype)


def _merge(oa, ob, oc, proj, wbr, tm):
    t = oa.shape[0]
    tn = 512
    nj = D_MODEL // tn
    gate = lambda k: pl.BlockSpec((tm, tn), lambda i, j: (i, COL_MG // tn + k * nj + j))
    wsp = lambda k: pl.BlockSpec((None, BRANCH_W, tn), lambda i, j: (k, 0, j))
    osp = pl.BlockSpec((tm, BRANCH_W), lambda i, j: (i, 0))
    return pl.pallas_call(
        _merge_kernel,
        grid=(t // tm, nj),
        in_specs=[osp, osp, osp, gate(0), gate(1), gate(2), wsp(0), wsp(1), wsp(2)],
        out_specs=pl.BlockSpec((tm, tn), lambda i, j: (i, j)),
        out_shape=jax.ShapeDtypeStruct((t, D_MODEL), BF16),
        compiler_params=_cparams(("arbitrary", "arbitrary")),
        name="merge",
    )(oa, ob, oc, proj, proj, proj, wbr, wbr, wbr)


def _layer_norm(r, g, b):
    mu = jnp.mean(r, -1, keepdims=True)
    rc = r - mu
    var = jnp.mean(rc * rc, -1, keepdims=True)
    return rc * lax.rsqrt(var + LN_EPS) * g + b


def _out_ln_kernel(m_ref, w_ref, x_ref, g1_ref, sc_ref, sh_ref, lg_ref, lb_ref, x1_ref, h2_ref, *, alpha):
    mix = _dot(m_ref[...], w_ref[...])
    x1 = _layer_norm(alpha * x_ref[...] + g1_ref[...] * mix, lg_ref[...], lb_ref[...])
    x1_ref[...] = x1
    h2_ref[...] = x1 * (1.0 + sc_ref[...]) + sh_ref[...]


def _out_ln(merged, w_out, x, g1, sc2, sh2, ln_g, ln_b, tm, tiles_per_group, alpha):
    t, d = x.shape
    row = pl.BlockSpec((tm, d), lambda i: (i, 0))
    vec = pl.BlockSpec((1, d), lambda i: (0, 0))
    kern = functools.partial(_out_ln_kernel, alpha=alpha)
    return pl.pallas_call(
        kern,
        grid=(t // tm,),
        in_specs=[row, pl.BlockSpec((d, d), lambda i: (0, 0)), row,
                  _mod_spec(g1, tm, tiles_per_group), _mod_spec(sc2, tm, tiles_per_group),
                  _mod_spec(sh2, tm, tiles_per_group), vec, vec],
        out_specs=[row, row],
        out_shape=[jax.ShapeDtypeStruct((t, d), F32), jax.ShapeDtypeStruct((t, d), F32)],
        compiler_params=_cparams(("arbitrary",)),
        name="out_ln",
    )(merged, w_out, x, g1, sc2, sh2, ln_g, ln_b)


ROUTER_TR = 640


def _router_kernel(h_ref, w_ref, b_ref, eidx_ref, rank_ref, wgt_ref, cnt_ref, carry_ref):
    i = pl.program_id(0)
    tr = h_ref.shape[0]
    ng, gs = N_GROUPS, GROUP_SIZE

    @pl.when(i == 0)
    def _():
        carry_ref[...] = jnp.zeros_like(carry_ref)

    logits = _dot3(w_ref[...], h_ref[...], _dot_nt)
    scores = _sigmoid(logits)
    sc3 = scores.reshape(ng, gs, tr)
    ch3 = (scores + b_ref[...]).reshape(ng, gs, tr)
    io_e = _iota2((ng, gs, tr), 1)
    io_g = _iota2((ng, 1, tr), 0)
    io_x = _iota2((ng, gs, tr), 0) * gs + io_e
    ninf = -jnp.inf

    m1 = jnp.max(ch3, 1, keepdims=True)
    i1 = jnp.min(jnp.where(ch3 == m1, io_e, gs), 1, keepdims=True)
    m2 = jnp.max(jnp.where(io_e == i1, ninf, ch3), 1, keepdims=True)
    grp = m1 + m2
    keep = jnp.zeros((ng, 1, tr), jnp.bool_)
    for _ in range(TOPK_GROUPS):
        m = jnp.max(grp, 0, keepdims=True)
        first = jnp.min(jnp.where(grp == m, io_g, ng), 0, keepdims=True)
        hit = io_g == first
        keep = keep | hit
        grp = jnp.where(hit, ninf, grp)

    cm = jnp.where(keep, ch3, ninf)
    hits = []
    firsts = []
    for _ in range(TOP_K):
        m = jnp.max(jnp.max(cm, 1, keepdims=True), 0, keepdims=True)
        cand = jnp.where(cm == m, io_x, N_EXPERTS)
        first = jnp.min(jnp.min(cand, 1, keepdims=True), 0, keepdims=True)
        hit = io_x == first
        hits.append(hit)
        firsts.append(first)
        cm = jnp.where(hit, ninf, cm)
    sel = hits[0]
    for hit in hits[1:]:
        sel = sel | hit
    self32 = jnp.where(sel, 1.0, 0.0).astype(F32)
    wsel = sc3 * self32
    den = jnp.sum(jnp.sum(wsel, 1, keepdims=True), 0, keepdims=True)
    comb = wsel / den * ROUTED_SCALE

    sel2 = self32.reshape(N_EXPERTS, tr)
    upper = jnp.where(_iota2((tr, tr), 0) <= _iota2((tr, tr), 1), 1.0, 0.0).astype(BF16)
    incl = _dot(sel2.astype(BF16), upper)
    carry = carry_ref[:, 0:1]
    rank3 = (carry + incl - sel2).reshape(ng, gs, tr)
    for r in range(TOP_K):
        hf = jnp.where(hits[r], 1.0, 0.0).astype(F32)
        rk = jnp.sum(jnp.sum(hf * rank3, 1, keepdims=True), 0, keepdims=True)
        wg = jnp.sum(jnp.sum(hf * comb, 1, keepdims=True), 0, keepdims=True)
        eidx_ref[r:r + 1, :] = firsts[r].reshape(1, tr)
        rank_ref[r:r + 1, :] = rk.reshape(1, tr).astype(I32)
        wgt_ref[r:r + 1, :] = wg.reshape(1, tr)
    new_carry = carry + incl[:, tr - 1:tr]
    carry_ref[...] = jnp.broadcast_to(new_carry, carry_ref.shape)
    cnt_ref[...] = jnp.broadcast_to(new_carry, cnt_ref.shape).astype(I32)


def _router(h2, wr_t, bias_col):
    t, d = h2.shape
    tr = ROUTER_TR
    out = pl.BlockSpec((TOP_K, tr), lambda i: (0, i))
    return pl.pallas_call(
        _router_kernel,
        grid=(t // tr,),
        in_specs=[pl.BlockSpec((tr, d), lambda i: (i, 0)),
                  pl.BlockSpec((N_EXPERTS, d), lambda i: (0, 0)),
                  pl.BlockSpec((N_EXPERTS, 1), lambda i: (0, 0))],
        out_specs=[out, out, out, pl.BlockSpec((N_EXPERTS, LANES), lambda i: (0, 0))],
        out_shape=[jax.ShapeDtypeStruct((TOP_K, t), I32), jax.ShapeDtypeStruct((TOP_K, t), I32),
                   jax.ShapeDtypeStruct((TOP_K, t), F32), jax.ShapeDtypeStruct((N_EXPERTS, LANES), I32)],
        scratch_shapes=[pltpu.VMEM((N_EXPERTS, LANES), F32)],
        compiler_params=_cparams(("arbitrary",)),
        name="router",
    )(h2, wr_t, bias_col)


DISPATCH_TT = 128


def _dispatch_kernel(zt_ref, slot_ref, h_ref, zeros_ref, xs_ref, sem_z, sem, *, tm):
    i = pl.program_id(0)
    tt = DISPATCH_TT

    @pl.when(i == 0)
    def _():
        def fill(e):
            return pltpu.make_async_copy(zeros_ref, xs_ref.at[pl.ds(zt_ref[e] * tm, tm)], sem_z)
        for e in range(N_EXPERTS):
            @pl.when(zt_ref[e] >= 0)
            def _():
                fill(e).start()
        for e in range(N_EXPERTS):
            @pl.when(zt_ref[e] >= 0)
            def _():
                fill(e).wait()

        def clear(tile):
            return pltpu.make_async_copy(zeros_ref, xs_ref.at[pl.ds(tile * tm, tm)], sem_z)
        n_tiles = xs_ref.shape[0] // tm

        def cbody(tile, carry):
            clear(tile).start()
            return carry
        lax.fori_loop(zt_ref[N_EXPERTS], n_tiles, cbody, 0)

        def cwait(tile, carry):
            clear(tile).wait()
            return carry
        lax.fori_loop(zt_ref[N_EXPERTS], n_tiles, cwait, 0)

    def row_copy(tok, s):
        return pltpu.make_async_copy(h_ref.at[pl.ds(tok, 1)], xs_ref.at[pl.ds(s, 1)], sem)

    def body(j, carry):
        tok = i * tt + j
        f = tok * TOP_K
        for k in range(TOP_K):
            s = slot_ref[(f + k) // LANES, (f + k) % LANES]
            row_copy(tok, s).start()
        return carry
    lax.fori_loop(0, tt, body, 0)

    def wbody(j, carry):
        for k in range(TOP_K):
            row_copy(0, 0).wait()
        return carry
    lax.fori_loop(0, tt, wbody, 0)


def _dispatch(zt, slots2d, h2, n_rows, tm):
    t, d = h2.shape
    kern = functools.partial(_dispatch_kernel, tm=tm)
    return pl.pallas_call(
        kern,
        grid_spec=pltpu.PrefetchScalarGridSpec(
            num_scalar_prefetch=2,
            grid=(t // DISPATCH_TT,),
            in_specs=[pl.BlockSpec(memory_space=pl.ANY),
                      pl.BlockSpec((tm, d), lambda i, *_: (0, 0))],
            out_specs=pl.BlockSpec(memory_space=pl.ANY),
            scratch_shapes=[pltpu.SemaphoreType.DMA(()), pltpu.SemaphoreType.DMA(())]),
        out_shape=jax.ShapeDtypeStruct((n_rows, d), F32),
        compiler_params=_cparams(("arbitrary",)),
        name="dispatch",
    )(zt, slots2d, h2, jnp.zeros((tm, d), F32))


def _ffn_kernel(te_ref, na_ref, x_ref, wg_ref, wu_ref, wd_ref, y_ref, wgb_ref, wub_ref, wdb_ref):
    i = pl.program_id(0)
    prev = te_ref[jnp.maximum(i - 1, 0)]

    @pl.when((i == 0) | (te_ref[i] != prev))
    def _():
        wgb_ref[...] = wg_ref[...].astype(BF16)
        wub_ref[...] = wu_ref[...].astype(BF16)
        wdb_ref[...] = wd_ref[...].astype(BF16)

    @pl.when(i < na_ref[0])
    def _():
        xb = x_ref[...].astype(BF16)
        a = _dot(xb, wgb_ref[...])
        u = _dot(xb, wub_ref[...])
        y_ref[...] = _dot((_silu(a) * u).astype(BF16), wdb_ref[...])

    @pl.when(i >= na_ref[0])
    def _():
        y_ref[...] = jnp.zeros_like(y_ref)


def _ffn(te, na, xs, wg, wu, wd, tm):
    p, d = xs.shape
    ff = wg.shape[-1]
    xmap = lambda i, te, na: (jnp.minimum(i, na[0] - 1), 0)
    return pl.pallas_call(
        _ffn_kernel,
        grid_spec=pltpu.PrefetchScalarGridSpec(
            num_scalar_prefetch=2,
            grid=(p // tm,),
            in_specs=[pl.BlockSpec((tm, d), xmap),
                      pl.BlockSpec((None, d, ff), lambda i, te, na: (te[i], 0, 0)),
                      pl.BlockSpec((None, d, ff), lambda i, te, na: (te[i], 0, 0)),
                      pl.BlockSpec((None, ff, d), lambda i, te, na: (te[i], 0, 0))],
            out_specs=pl.BlockSpec((tm, d), lambda i, te, na: (i, 0)),
            scratch_shapes=[pltpu.VMEM((d, ff), BF16), pltpu.VMEM((d, ff), BF16), pltpu.VMEM((ff, d), BF16)]),
        out_shape=jax.ShapeDtypeStruct((p, d), F32),
        compiler_params=_cparams(("arbitrary",)),
        name="expert_ffn",
    )(te, na, xs, wg, wu, wd)


COMBINE_TC = 32


def _combine_kernel(slot_ref, ys_ref, w_ref, sh_ref, x_ref, g2_ref, lg_ref, lb_ref, o_ref, buf_ref, sem, *, alpha):
    i = pl.program_id(0)
    n = pl.num_programs(0)
    tc = COMBINE_TC

    def row_copy(s, b, r):
        return pltpu.make_async_copy(ys_ref.at[pl.ds(s, 1)], buf_ref.at[b, pl.ds(r, 1)], sem.at[b])

    def issue(tile, b):
        def body(j, carry):
            f = (tile * tc + j) * TOP_K
            for k in range(TOP_K):
                s = slot_ref[(f + k) // LANES, (f + k) % LANES]
                row_copy(s, b, k * tc + j).start()
            return carry
        lax.fori_loop(0, tc, body, 0)

    @pl.when(i == 0)
    def _():
        issue(0, 0)

    @pl.when(i + 1 < n)
    def _():
        issue(i + 1, (i + 1) % 2)

    b = i % 2

    def wbody(j, carry):
        for k in range(TOP_K):
            row_copy(0, b, 0).wait()
        return carry
    lax.fori_loop(0, tc, wbody, 0)

    w = w_ref[...]
    acc = sh_ref[...]
    for k in range(TOP_K):
        acc = acc + w[:, k:k + 1] * buf_ref[b, k * tc:(k + 1) * tc, :]
    o_ref[...] = _layer_norm(alpha * x_ref[...] + g2_ref[...] * acc, lg_ref[...], lb_ref[...])


def _combine(slots2d, ys, wgt, shared, x1, g2rows, ln_g, ln_b, alpha):
    t, d = x1.shape
    tc = COMBINE_TC
    row = pl.BlockSpec((tc, d), lambda i, *_: (i, 0))
    vec = pl.BlockSpec((1, d), lambda i, *_: (0, 0))
    kern = functools.partial(_combine_kernel, alpha=alpha)
    return pl.pallas_call(
        kern,
        grid_spec=pltpu.PrefetchScalarGridSpec(
            num_scalar_prefetch=1,
            grid=(t // tc,),
            in_specs=[pl.BlockSpec(memory_space=pl.ANY),
                      pl.BlockSpec((tc, TOP_K), lambda i, *_: (i, 0)),
                      row, row, row, vec, vec],
            out_specs=row,
            scratch_shapes=[pltpu.VMEM((2, TOP_K * tc, d), F32), pltpu.SemaphoreType.DMA((2,))]),
        out_shape=jax.ShapeDtypeStruct((t, d), F32),
        compiler_params=_cparams(("arbitrary",)),
        name="combine_ln",
    )(slots2d, ys, wgt, shared, x1, g2rows, ln_g, ln_b)


def _moe(h2, x1, g2rows, p, alpha):
    t, d = h2.shape
    tm = MOE_TM
    eidx, rank, wgt, cnt = _router(h2, p['w_router'].T, p['router_bias'].reshape(N_EXPERTS, 1))
    counts = cnt[:, 0]
    tiles = (counts + tm - 1) // tm
    tile_end = jnp.cumsum(tiles)
    tile_start = tile_end - tiles
    n_active = tile_end[-1]
    n_tiles = (t * TOP_K + N_EXPERTS * (tm - 1)) // tm
    ids = jnp.arange(n_tiles, dtype=I32)
    te = jnp.minimum(jnp.searchsorted(tile_end, ids, side='right').astype(I32), N_EXPERTS - 1)
    te = jnp.where(ids < n_active, te, te[jnp.maximum(n_active - 1, 0)])
    zt = jnp.where(counts % tm != 0, tile_end - 1, -1).astype(I32)
    zt = jnp.concatenate([zt, n_active.reshape(1).astype(I32)])
    slots = (tile_start * tm)[eidx] + rank
    slots2d = slots.T.reshape(t * TOP_K // LANES, LANES).astype(I32)
    xs = _dispatch(zt, slots2d, h2, n_tiles * tm, tm)
    ys = _ffn(te, n_active.reshape(1).astype(I32), xs, p['w_exp_gate'], p['w_exp_up'], p['w_exp_down'], tm)
    tsh = 640
    nsh = t // tsh
    shared = _ffn(jnp.zeros((nsh,), I32), jnp.full((1,), nsh, I32), h2,
                  p['w_sh_gate'][None], p['w_sh_up'][None], p['w_sh_down'][None], tsh)
    return _combine(slots2d, ys, wgt.T, shared, x1, g2rows, p['ln2_g'].reshape(1, d), p['ln2_b'].reshape(1, d), alpha)


def _chunk_t(sm, c):
    b, l, _ = sm.shape
    return jnp.swapaxes(sm.reshape(b, l // c, c, SMALL_W)[..., :32], -1, -2)


def _pad_rows(a, rows):
    return jnp.pad(a, ((0, 0), (0, rows - a.shape[1]), (0, 0)))


def _mixer_group(x, mod, states, p, tabs, *, prompt):
    b, l, d = x.shape
    t = b * l
    dn_conv0, dn_s0, ssm_conv0, ssm_h0, kv_buf = states
    sh1, sc1, g1, sh2, sc2, g2 = jnp.split(mod, 6, axis=-1)
    if prompt:
        tm = 1024
        tpg = l // tm
        shape = lambda m: m.reshape(b, 1, d)
    else:
        tm = t
        tpg = 1
        shape = lambda m: jnp.repeat(m, l, axis=0).reshape(1, t, d)
    proj, small = _in_proj(x.reshape(t, d), shape(sc1), shape(sh1), p['w_main'], p['w_small'], tm, tpg)
    proj3 = proj.reshape(b, l, MAIN_W)
    small3 = small.reshape(b, l, SMALL_W)
    if prompt:
        c = CHUNK
        dn_in = (proj3, proj3)
        dn_cb = dict(cb_qkv=COL_QKV // DN_CONV_DIM, cb_gate=COL_DNG // DN_KEY)
        ssd_in = (proj3, proj3)
        ssd_cb = dict(cb_z=COL_SSZ // SSM_INNER, cb_xbc=COL_XBC // SSM_CONV_DIM)
        sm_in = small3
        odt = BF16
    else:
        c = SUBLANES
        cut = lambda c0, w: _pad_rows(proj3[:, :, c0:c0 + w], c)
        dn_in = (cut(COL_QKV, DN_CONV_DIM), cut(COL_DNG, DN_KEY))
        dn_cb = dict(cb_qkv=0, cb_gate=0)
        ssd_in = (cut(COL_SSZ, SSM_INNER), cut(COL_XBC, SSM_CONV_DIM))
        ssd_cb = dict(cb_z=0, cb_xbc=0)
        sm_in = _pad_rows(small3, c)
        odt = F32
    valid = c if prompt else l
    smt = _chunk_t(sm_in, c)
    o_a, dn_conv, dn_s = _dn(dn_in[0], dn_in[1], sm_in, smt, dn_conv0, dn_s0, p['dn_conv_w'], p['dn_prow'],
                             p['dn_pcol'], p['dn_norm_w'].reshape(1, DN_D), c=c, valid=valid, out_dtype=odt, **dn_cb)
    o_b, ssm_conv, ssm_h = _ssd(ssd_in[0], ssd_in[1], sm_in, smt, ssm_conv0, ssm_h0, p['ssm_conv_w'],
                                p['ssm_conv_b'].reshape(1, SSM_CONV_DIM), p['ssm_prow'], p['ssm_pcol'],
                                p['ssm_norm_w'].reshape(1, SSM_INNER), c=c, valid=valid, out_dtype=odt, **ssd_cb)
    sinks = p['swa_sinks'].reshape(1, SWA_HEADS)
    if prompt:
        o_c, k_new, v_new = _swa_prompt(proj3, tabs[0], tabs[1], sinks)
    else:
        o_c, k_new, v_new = _swa_sample(cut(COL_SWQ, SWA_HEADS * SWA_D), cut(COL_SWK, SWA_KV * SWA_D),
                                        cut(COL_SWV, SWA_KV * SWA_D),
                                        kv_buf[0].reshape(b, WINDOW, SWA_KV * SWA_D),
                                        kv_buf[1].reshape(b, WINDOW, SWA_KV * SWA_D), tabs[0], tabs[1], sinks, l)
    flat = lambda o: o[:, :l].reshape(t, BRANCH_W).astype(BF16)
    merged = _merge(flat(o_a), flat(o_b), flat(o_c), proj, p['w_branch'], tm=min(tm, 512))
    tm2 = min(tm, 512)
    tpg2 = l // tm2 if prompt else 1
    shape2 = (lambda m: m.reshape(b, 1, d)) if prompt else shape
    x1, h2 = _out_ln(merged, p['w_out'], x.reshape(t, d), shape2(g1), shape2(sc2), shape2(sh2),
                     p['ln1_g'].reshape(1, d), p['ln1_b'].reshape(1, d), tm2, tpg2, p['alpha'])
    g2rows = jnp.repeat(g2, l, axis=0)
    k_new = k_new.reshape(b, WINDOW, SWA_KV, SWA_D)
    v_new = v_new.reshape(b, WINDOW, SWA_KV, SWA_D)
    return x1, h2, g2rows, (dn_conv, dn_s, ssm_conv, ssm_h, k_new, v_new)


def _forward(x_prompt, x_sample, state_dn_conv, state_dn, state_ssm_conv, state_ssm, cache_swa_k, cache_swa_v,
             c_prompt, c_sample, w_ada, b_ada, w_in, dn_conv_w, dn_a_log, dn_dt_bias, dn_norm_w,
             ssm_conv_w, ssm_conv_b, ssm_a_log, ssm_dt_bias, ssm_d, ssm_norm_w, swa_sinks, w_branch, w_out,
             ln1_g, ln1_b, w_router, router_bias, w_exp_gate, w_exp_up, w_exp_down, w_sh_gate, w_sh_up,
             w_sh_down, ln2_g, ln2_b):
    depth = w_in.shape[0]
    bp, lp, d = x_prompt.shape
    bs, ls, _ = x_sample.shape
    tp, ts = bp * lp, bs * ls
    alpha = (2 * depth) ** 0.25
    past_len = 16384

    c_all = jnp.concatenate([c_prompt, c_sample, jnp.zeros((4, d), F32)], 0)
    mod_all = _ada(c_all, w_ada, b_ada)
    tabs_p = tuple(jnp.stack(_rope_tables(jnp.arange(lp), h)) for h in (SWA_HEADS, SWA_KV))
    tabs_s = tuple(jnp.stack(_rope_tables(past_len + jnp.arange(SUBLANES), h)) for h in (SWA_HEADS, SWA_KV))
    init_p = (jnp.zeros((bp, CONV_W - 1, DN_CONV_DIM), F32), jnp.zeros((bp, DN_HEADS, DN_D, DN_D), F32),
              jnp.zeros((bp, CONV_W - 1, SSM_CONV_DIM), F32), jnp.zeros((bp, SSM_HEADS, SSM_P, SSM_N), F32), None)

    def pad_lanes(v, at):
        return jnp.zeros((SMALL_W,), F32).at[at:at + v.shape[0]].set(v)

    yp, ys = x_prompt, x_sample
    new_p, new_s = [], []
    for l in range(depth):
        w = w_in[l]
        seg = lambda a, n: w[:, a:a + n]
        w_main = jnp.concatenate([seg(8224, 6144), seg(0, 3072), seg(5136, 1536), seg(7712, 256), seg(7968, 256),
                                  seg(3072, 1024), seg(4112, 1024), seg(6688, 1024)], 1).astype(BF16)
        w_small = jnp.concatenate([seg(4096, 16), seg(6672, 16), jnp.zeros((d, SMALL_W - 32), F32)], 1).astype(BF16)
        p = {'w_main': w_main, 'w_small': w_small, 'alpha': alpha,
             'dn_conv_w': dn_conv_w[l], 'dn_norm_w': dn_norm_w[l],
             'dn_prow': jnp.stack([pad_lanes(dn_a_log[l], SM_A), pad_lanes(dn_dt_bias[l], SM_A)]),
             'dn_pcol': jnp.stack([dn_a_log[l], dn_dt_bias[l]], 1),
             'ssm_conv_w': ssm_conv_w[l], 'ssm_conv_b': ssm_conv_b[l], 'ssm_norm_w': ssm_norm_w[l],
             'ssm_prow': jnp.stack([pad_lanes(ssm_dt_bias[l], SM_DT), pad_lanes(ssm_a_log[l], SM_DT),
                                    pad_lanes(ssm_d[l], SM_DT)]),
             'ssm_pcol': jnp.stack([ssm_dt_bias[l], ssm_a_log[l]], 1),
             'swa_sinks': swa_sinks[l], 'w_branch': w_branch[l].astype(BF16), 'w_out': w_out[l].astype(BF16),
             'ln1_g': ln1_g[l], 'ln1_b': ln1_b[l], 'w_router': w_router[l], 'router_bias': router_bias[l],
             'w_exp_gate': w_exp_gate[l], 'w_exp_up': w_exp_up[l], 'w_exp_down': w_exp_down[l],
             'w_sh_gate': w_sh_gate[l], 'w_sh_up': w_sh_up[l], 'w_sh_down': w_sh_down[l],
             'ln2_g': ln2_g[l], 'ln2_b': ln2_b[l]}
        mod = mod_all[l]
        x1p, h2p, g2p, st_p = _mixer_group(yp, mod[:bp], init_p, p, tabs_p, prompt=True)
        st_in = (state_dn_conv[l], state_dn[l], state_ssm_conv[l], state_ssm[l], (cache_swa_k[l], cache_swa_v[l]))
        x1s, h2s, g2s, st_s = _mixer_group(ys, mod[bp:bp + bs], st_in, p, tabs_s, prompt=False)
        cat = lambda a, b_: jnp.concatenate([a, b_], 0)
        x2 = _moe(cat(h2p, h2s), cat(x1p, x1s), cat(g2p, g2s), p, alpha)
        yp = x2[:tp].reshape(bp, lp, d)
        ys = x2[tp:].reshape(bs, ls, d)
        new_p.append(st_p)
        new_s.append(st_s)
    outs = [yp, ys]
    for k in range(6):
        outs.append(jnp.stack([s[k] for s in new_p]))
        outs.append(jnp.stack([s[k] for s in new_s]))
    return tuple(outs)


def kernel(x_prompt, x_sample, state_dn_conv, state_dn, state_ssm_conv, state_ssm, cache_swa_k, cache_swa_v, c_prompt, c_sample, w_ada, b_ada, w_in, dn_conv_w, dn_a_log, dn_dt_bias, dn_norm_w, ssm_conv_w, ssm_conv_b, ssm_a_log, ssm_dt_bias, ssm_d, ssm_norm_w, swa_sinks, w_branch, w_out, ln1_g, ln1_b, w_router, router_bias, w_exp_gate, w_exp_up, w_exp_down, w_sh_gate, w_sh_up, w_sh_down, ln2_g, ln2_b):
    return _forward(x_prompt, x_sample, state_dn_conv, state_dn, state_ssm_conv, state_ssm, cache_swa_k, cache_swa_v, c_prompt, c_sample, w_ada, b_ada, w_in, dn_conv_w, dn_a_log, dn_dt_bias, dn_norm_w, ssm_conv_w, ssm_conv_b, ssm_a_log, ssm_dt_bias, ssm_d, ssm_norm_w, swa_sinks, w_branch, w_out, ln1_g, ln1_b, w_router, router_bias, w_exp_gate, w_exp_up, w_exp_down, w_sh_gate, w_sh_up, w_sh_down, ln2_g, ln2_b)
```

```python
import functools
import math

import jax
import jax.numpy as jnp
from jax import lax
from jax.experimental import pallas as pl
from jax.experimental.pallas import tpu as pltpu

F32 = jnp.float32
BF16 = jnp.bfloat16
I32 = jnp.int32

D_MODEL = 2048
CONV_W = 4
DN_HEADS = 8
DN_D = 128
DN_KEY = DN_HEADS * DN_D
DN_CONV_DIM = 3 * DN_KEY
SSM_HEADS = 16
SSM_P = 64
SSM_INNER = SSM_HEADS * SSM_P
SSM_GROUPS = 2
SSM_N = 128
SSM_CONV_DIM = SSM_INNER + 2 * SSM_GROUPS * SSM_N
SWA_HEADS = 16
SWA_KV = 4
SWA_D = 64
SWA_GRP = SWA_HEADS // SWA_KV
WINDOW = 128
ROPE_DIM = SWA_D // 4
ROPE_THETA = 500000.0
N_BRANCH = 3
BRANCH_W = 1024
N_EXPERTS = 64
EXPERT_FF = 512
TOP_K = 8
N_GROUPS = 8
GROUP_SIZE = N_EXPERTS // N_GROUPS
TOPK_GROUPS = 4
ROUTED_SCALE = 2.5
LN_EPS = 1e-5
NORM_EPS = 1e-6
CHUNK = 64

LANES = 128
SUBLANES = 8
VMEM_LIMIT = 56 * 1024 * 1024

MAIN_W = 14336
COL_MG, COL_QKV, COL_XBC, COL_SWK, COL_SWV, COL_DNG, COL_SSZ, COL_SWQ = (
    0, 6144, 9216, 10752, 11008, 11264, 12288, 13312)
SMALL_W = LANES
SM_A, SM_B, SM_DT = 0, 8, 16

MOE_TM = 256


def _cparams(sem, vmem=VMEM_LIMIT):
    return pltpu.CompilerParams(dimension_semantics=sem, vmem_limit_bytes=vmem)


def _dot(a, b):
    return jnp.dot(a, b, preferred_element_type=F32)


def _dot_nt(a, b):
    return lax.dot_general(a, b, (((1,), (1,)), ((), ())), preferred_element_type=F32)


def _dot_tn(a, b):
    return lax.dot_general(a, b, (((0,), (0,)), ((), ())), preferred_element_type=F32)


def _split(a):
    hi = a.astype(BF16)
    lo = (a - hi.astype(F32)).astype(BF16)
    return hi, lo


def _dot3(a, b, dot=_dot):
    ah, al = _split(a)
    bh, bl = _split(b)
    return dot(ah, bh) + dot(ah, bl) + dot(al, bh)


def _bdot(a, b, dot=_dot):
    return dot(a.astype(BF16), b.astype(BF16))


def _silu(x):
    return x * (1.0 / (1.0 + jnp.exp(-x)))


def _sigmoid(x):
    return 1.0 / (1.0 + jnp.exp(-x))


def _softplus(x):
    return jnp.maximum(x, 0.0) + jnp.log(1.0 + jnp.exp(-jnp.abs(x)))


def _ada_kernel(c_ref, w_ref, b_ref, o_ref):
    c = c_ref[...]
    o_ref[...] = _bdot(_silu(c), w_ref[...]) + b_ref[...]


def _ada(c_all, w_ada, b_ada):
    depth, d, n = w_ada.shape
    rows = c_all.shape[0]
    tn = 1024
    return pl.pallas_call(
        _ada_kernel,
        grid=(depth, n // tn),
        in_specs=[pl.BlockSpec((rows, d), lambda l, j: (0, 0)),
                  pl.BlockSpec((None, d, tn), lambda l, j: (l, 0, j)),
                  pl.BlockSpec((None, 1, tn), lambda l, j: (l, 0, j))],
        out_specs=pl.BlockSpec((None, rows, tn), lambda l, j: (l, 0, j)),
        out_shape=jax.ShapeDtypeStruct((depth, rows, n), F32),
        compiler_params=_cparams(("arbitrary", "arbitrary")),
        name="ada",
    )(c_all, w_ada, b_ada.reshape(depth, 1, n))


def _mod_spec(mod, tm, tiles_per_group):
    r = mod.shape[1]
    if r == 1:
        return pl.BlockSpec((None, 1, mod.shape[2]), lambda i, *_: (i // tiles_per_group, 0, 0))
    return pl.BlockSpec((None, r, mod.shape[2]), lambda i, *_: (i, 0, 0))


def _in_proj_kernel(x_ref, sc_ref, sh_ref, wm_ref, ws_ref, proj_ref, small_ref, hb_ref):
    @pl.when(pl.program_id(1) == 0)
    def _():
        h = x_ref[...] * (1.0 + sc_ref[...]) + sh_ref[...]
        hb_ref[...] = h.astype(BF16)
        small_ref[...] = _dot(hb_ref[...], ws_ref[...])

    proj_ref[...] = _dot(hb_ref[...], wm_ref[...])


def _in_proj(x, sc, sh, w_main, w_small, tm, tiles_per_group):
    t, d = x.shape
    tn = 1024
    return pl.pallas_call(
        _in_proj_kernel,
        grid=(t // tm, MAIN_W // tn),
        in_specs=[pl.BlockSpec((tm, d), lambda i, j: (i, 0)),
                  _mod_spec(sc, tm, tiles_per_group),
                  _mod_spec(sh, tm, tiles_per_group),
                  pl.BlockSpec((d, tn), lambda i, j: (0, j)),
                  pl.BlockSpec((d, SMALL_W), lambda i, j: (0, 0))],
        out_specs=[pl.BlockSpec((tm, tn), lambda i, j: (i, j)),
                   pl.BlockSpec((tm, SMALL_W), lambda i, j: (i, 0))],
        out_shape=[jax.ShapeDtypeStruct((t, MAIN_W), F32),
                   jax.ShapeDtypeStruct((t, SMALL_W), F32)],
        scratch_shapes=[pltpu.VMEM((tm, d), BF16)],
        compiler_params=_cparams(("arbitrary", "arbitrary")),
        name="in_proj",
    )(x, sc, sh, w_main, w_small)


def _iota2(shape, dim):
    return lax.broadcasted_iota(I32, shape, dim)


def _conv_silu(xp_ref, x_ref, w_ref, bias, c):
    xp_ref[SUBLANES:SUBLANES + c, :] = x_ref[...]
    base = SUBLANES - (CONV_W - 1)
    y = xp_ref[base:base + c, :] * w_ref[0:1, :]
    for j in range(1, CONV_W):
        y = y + xp_ref[base + j:base + j + c, :] * w_ref[j:j + 1, :]
    if bias is not None:
        y = y + bias
    return _silu(y)


def _dn_kernel(qkv_ref, gate_ref, sm_ref, smt_ref, conv0_ref, s0_ref, cw_ref, prow_ref, pcol_ref, nw_ref,
               o_ref, conv_out_ref, s_out_ref, xp_ref, s_ref, y_ref, *, c, valid):
    n = pl.program_id(1)
    base = SUBLANES - (CONV_W - 1)

    @pl.when(n == 0)
    def _():
        xp_ref[base:SUBLANES, :] = conv0_ref[...]
        s_ref[...] = s0_ref[...]

    y = _conv_silu(xp_ref, qkv_ref, cw_ref, None, c)
    tail = xp_ref[SUBLANES + valid - (CONV_W - 1):SUBLANES + valid, :]
    conv_out_ref[...] = tail
    xp_ref[base:SUBLANES, :] = tail

    row_ok = _iota2((c, 1), 0) < valid
    col_ok = _iota2((1, c), 1) < valid
    sm = sm_ref[...]
    g_all = -jnp.exp(prow_ref[0:1, :]) * _softplus(sm + prow_ref[1:2, :])
    g_all = jnp.where(row_ok, g_all, 0.0)
    beta_all = jnp.where(row_ok, _sigmoid(sm), 0.0)
    smt = smt_ref[...]
    g_t = -jnp.exp(pcol_ref[:, 0:1]) * _softplus(smt[0:DN_HEADS, :] + pcol_ref[:, 1:2])
    g_t = jnp.where(col_ok, g_t, 0.0)
    row = _iota2((c, c), 0)
    col = _iota2((c, c), 1)
    tri = jnp.where(row >= col, 1.0, 0.0).astype(F32)
    lc_all = _dot3(tri, g_all)
    lc_t = _dot3(g_t, jnp.where(row <= col, 1.0, 0.0).astype(F32))
    causal = row >= col
    strict = row > col

    heads = range(DN_HEADS)
    y_ref[...] = y
    qs, ks, vs, lcs, betas, decays, mats, qks = [], [], [], [], [], [], [], []
    for h in heads:
        qh = y_ref[:, h * DN_D:(h + 1) * DN_D]
        kh = y_ref[:, DN_KEY + h * DN_D:DN_KEY + (h + 1) * DN_D]
        vh = y_ref[:, 2 * DN_KEY + h * DN_D:2 * DN_KEY + (h + 1) * DN_D]
        qh = qh * lax.rsqrt(jnp.sum(qh * qh, -1, keepdims=True) + NORM_EPS) * (DN_D ** -0.5)
        kh = kh * lax.rsqrt(jnp.sum(kh * kh, -1, keepdims=True) + NORM_EPS)
        qs.append(qh)
        ks.append(jnp.where(row_ok, kh, 0.0))
        vs.append(jnp.where(row_ok, vh, 0.0))
        lcs.append(lc_all[:, SM_A + h:SM_A + h + 1])
        betas.append(beta_all[:, SM_B + h:SM_B + h + 1])
        decays.append(jnp.exp(jnp.where(causal, lcs[h] - lc_t[h:h + 1, :], -jnp.inf)))
    kbs = [ks[h] * betas[h] for h in heads]
    for h in heads:
        mats.append(jnp.where(strict, _bdot(kbs[h], ks[h], _dot_nt) * decays[h], 0.0))
        qks.append(_bdot(qs[h], ks[h], _dot_nt) * decays[h])

    ts = [jnp.where(row == col, 1.0, 0.0).astype(F32)] * DN_HEADS
    s = 1
    while s < c:
        m = ((row // s) % 2 == 1) & ((col // s) % 2 == 0) & (row // (2 * s) == col // (2 * s))
        tsp = [_split(ts[h]) for h in heads]
        ams = [_split(jnp.where(m, mats[h], 0.0)) for h in heads]
        ps = [_dot(tsp[h][0], ams[h][0]) + _dot(tsp[h][0], ams[h][1]) + _dot(tsp[h][1], ams[h][0]) for h in heads]
        psp = [_split(ps[h]) for h in heads]
        ts = [ts[h] - (_dot(psp[h][0], tsp[h][0]) + _dot(psp[h][0], tsp[h][1]) + _dot(psp[h][1], tsp[h][0]))
              for h in heads]
        s *= 2

    e_lcs = [jnp.exp(lcs[h]) for h in heads]
    us = [_dot3(ts[h], vs[h] * betas[h]) for h in heads]
    ws = [_dot3(ts[h], kbs[h] * e_lcs[h]) for h in heads]
    shs = [s_ref[h] for h in heads]
    v_news = [us[h] - _bdot(ws[h], shs[h]) for h in heads]
    os_ = [_bdot(qs[h] * e_lcs[h], shs[h]) + _bdot(qks[h], v_news[h]) for h in heads]
    lasts = [lc_all[c - 1:c, SM_A + h:SM_A + h + 1] for h in heads]
    s_news = [shs[h] * jnp.exp(lasts[h]) + _bdot(ks[h] * jnp.exp(lasts[h] - lcs[h]), v_news[h], _dot_tn)
              for h in heads]
    for h in heads:
        s_ref[h] = s_news[h]
        s_out_ref[h] = s_news[h]
        o = os_[h]
        o = o * lax.rsqrt(jnp.mean(o * o, -1, keepdims=True) + NORM_EPS) * nw_ref[...]
        sl = slice(h * DN_D, (h + 1) * DN_D)
        o_ref[:, sl] = (o * _silu(gate_ref[:, sl])).astype(o_ref.dtype)


def _dn(qkv, gate, sm, smt, conv0, s0, conv_w, prow, pcol, norm_w, *, c, valid, cb_qkv, cb_gate, out_dtype):
    b, lp = qkv.shape[0], qkv.shape[1]
    nch = lp // c
    kern = functools.partial(_dn_kernel, c=c, valid=valid)
    return pl.pallas_call(
        kern,
        grid=(b, nch),
        in_specs=[pl.BlockSpec((None, c, DN_CONV_DIM), lambda i, n: (i, n, cb_qkv)),
                  pl.BlockSpec((None, c, DN_KEY), lambda i, n: (i, n, cb_gate)),
                  pl.BlockSpec((None, c, SMALL_W), lambda i, n: (i, n, 0)),
                  pl.BlockSpec((None, None, 32, c), lambda i, n: (i, n, 0, 0)),
                  pl.BlockSpec((None, CONV_W - 1, DN_CONV_DIM), lambda i, n: (i, 0, 0)),
                  pl.BlockSpec((None, DN_HEADS, DN_D, DN_D), lambda i, n: (i, 0, 0, 0)),
                  pl.BlockSpec((CONV_W, DN_CONV_DIM), lambda i, n: (0, 0)),
                  pl.BlockSpec((2, SMALL_W), lambda i, n: (0, 0)),
                  pl.BlockSpec((DN_HEADS, 2), lambda i, n: (0, 0)),
                  pl.BlockSpec((1, DN_D), lambda i, n: (0, 0))],
        out_specs=[pl.BlockSpec((None, c, DN_KEY), lambda i, n: (i, n, 0)),
                   pl.BlockSpec((None, CONV_W - 1, DN_CONV_DIM), lambda i, n: (i, 0, 0)),
                   pl.BlockSpec((None, DN_HEADS, DN_D, DN_D), lambda i, n: (i, 0, 0, 0))],
        out_shape=[jax.ShapeDtypeStruct((b, lp, DN_KEY), out_dtype),
                   jax.ShapeDtypeStruct((b, CONV_W - 1, DN_CONV_DIM), F32),
                   jax.ShapeDtypeStruct((b, DN_HEADS, DN_D, DN_D), F32)],
        scratch_shapes=[pltpu.VMEM((c + SUBLANES, DN_CONV_DIM), F32),
                        pltpu.VMEM((DN_HEADS, DN_D, DN_D), F32),
                        pltpu.VMEM((c, DN_CONV_DIM), F32)],
        compiler_params=_cparams(("arbitrary", "arbitrary")),
        name="delta_rule",
    )(qkv, gate, sm, smt, conv0, s0, conv_w, prow, pcol, norm_w)


def _ssd_kernel(z_ref, xbc_ref, sm_ref, smt_ref, conv0_ref, h0_ref, cw_ref, cb_ref, prow_ref, pcol_ref, nw_ref,
                o_ref, conv_out_ref, h_out_ref, xp_ref, h_ref, y_ref, *, c, valid):
    n = pl.program_id(1)
    base = SUBLANES - (CONV_W - 1)

    @pl.when(n == 0)
    def _():
        xp_ref[base:SUBLANES, :] = conv0_ref[...]
        h_ref[...] = h0_ref[...]

    act = _conv_silu(xp_ref, xbc_ref, cw_ref, cb_ref[...], c)
    tail = xp_ref[SUBLANES + valid - (CONV_W - 1):SUBLANES + valid, :]
    conv_out_ref[...] = tail
    xp_ref[base:SUBLANES, :] = tail

    row_ok = _iota2((c, 1), 0) < valid
    col_ok = _iota2((1, c), 1) < valid
    dt_all = _softplus(sm_ref[...] + prow_ref[0:1, :])
    la_all = jnp.where(row_ok, dt_all * -jnp.exp(prow_ref[1:2, :]), 0.0)
    dt_t = _softplus(smt_ref[SM_DT:SM_DT + SSM_HEADS, :] + pcol_ref[:, 0:1])
    la_t = jnp.where(col_ok, dt_t * -jnp.exp(pcol_ref[:, 1:2]), 0.0)
    row = _iota2((c, c), 0)
    col = _iota2((c, c), 1)
    lc_all = _dot3(jnp.where(row >= col, 1.0, 0.0).astype(F32), la_all)
    lc_t = _dot3(la_t, jnp.where(row <= col, 1.0, 0.0).astype(F32))
    causal = row >= col
    rep = SSM_HEADS // SSM_GROUPS

    for g in range(SSM_GROUPS):
        bg = act[:, SSM_INNER + g * SSM_N:SSM_INNER + (g + 1) * SSM_N]
        cg = act[:, SSM_INNER + (SSM_GROUPS + g) * SSM_N:SSM_INNER + (SSM_GROUPS + g + 1) * SSM_N]
        bg = jnp.where(row_ok, bg, 0.0)
        cbg = _bdot(cg, bg, _dot_nt)
        for hh in range(rep):
            h = g * rep + hh
            sl = slice(h * SSM_P, (h + 1) * SSM_P)
            xh = jnp.where(row_ok, act[:, sl], 0.0)
            dt_c = dt_all[:, SM_DT + h:SM_DT + h + 1]
            lc_c = lc_all[:, SM_DT + h:SM_DT + h + 1]
            lc_r = lc_t[h:h + 1, :]
            xd = xh * dt_c
            decay = jnp.exp(jnp.where(causal, lc_c - lc_r, -jnp.inf))
            st = h_ref[h]
            y = _bdot(cbg * decay, xd) + _bdot(cg * jnp.exp(lc_c), st, _dot_nt)
            last = lc_all[c - 1:c, SM_DT + h:SM_DT + h + 1]
            st_new = st * jnp.exp(last) + _bdot(xd, bg * jnp.exp(last - lc_c), _dot_tn)
            h_ref[h] = st_new
            h_out_ref[h] = st_new
            y = y + act[:, sl] * prow_ref[2:3, SM_DT + h:SM_DT + h + 1]
            y_ref[:, sl] = y * _silu(z_ref[:, sl])

    gw = SSM_INNER // SSM_GROUPS
    for g in range(SSM_GROUPS):
        yg = y_ref[:, g * gw:(g + 1) * gw]
        yg = yg * lax.rsqrt(jnp.mean(yg * yg, -1, keepdims=True) + NORM_EPS) * nw_ref[:, g * gw:(g + 1) * gw]
        o_ref[:, g * gw:(g + 1) * gw] = yg.astype(o_ref.dtype)


def _ssd(z, xbc, sm, smt, conv0, h0, conv_w, conv_b, prow, pcol, norm_w, *, c, valid, cb_z, cb_xbc, out_dtype):
    b, lp = z.shape[0], z.shape[1]
    nch = lp // c
    kern = functools.partial(_ssd_kernel, c=c, valid=valid)
    return pl.pallas_call(
        kern,
        grid=(b, nch),
        in_specs=[pl.BlockSpec((None, c, SSM_INNER), lambda i, n: (i, n, cb_z)),
                  pl.BlockSpec((None, c, SSM_CONV_DIM), lambda i, n: (i, n, cb_xbc)),
                  pl.BlockSpec((None, c, SMALL_W), lambda i, n: (i, n, 0)),
                  pl.BlockSpec((None, None, 32, c), lambda i, n: (i, n, 0, 0)),
                  pl.BlockSpec((None, CONV_W - 1, SSM_CONV_DIM), lambda i, n: (i, 0, 0)),
                  pl.BlockSpec((None, SSM_HEADS, SSM_P, SSM_N), lambda i, n: (i, 0, 0, 0)),
                  pl.BlockSpec((CONV_W, SSM_CONV_DIM), lambda i, n: (0, 0)),
                  pl.BlockSpec((1, SSM_CONV_DIM), lambda i, n: (0, 0)),
                  pl.BlockSpec((3, SMALL_W), lambda i, n: (0, 0)),
                  pl.BlockSpec((SSM_HEADS, 2), lambda i, n: (0, 0)),
                  pl.BlockSpec((1, SSM_INNER), lambda i, n: (0, 0))],
        out_specs=[pl.BlockSpec((None, c, SSM_INNER), lambda i, n: (i, n, 0)),
                   pl.BlockSpec((None, CONV_W - 1, SSM_CONV_DIM), lambda i, n: (i, 0, 0)),
                   pl.BlockSpec((None, SSM_HEADS, SSM_P, SSM_N), lambda i, n: (i, 0, 0, 0))],
        out_shape=[jax.ShapeDtypeStruct((b, lp, SSM_INNER), out_dtype),
                   jax.ShapeDtypeStruct((b, CONV_W - 1, SSM_CONV_DIM), F32),
                   jax.ShapeDtypeStruct((b, SSM_HEADS, SSM_P, SSM_N), F32)],
        scratch_shapes=[pltpu.VMEM((c + SUBLANES, SSM_CONV_DIM), F32),
                        pltpu.VMEM((SSM_HEADS, SSM_P, SSM_N), F32),
                        pltpu.VMEM((c, SSM_INNER), F32)],
        compiler_params=_cparams(("arbitrary", "arbitrary")),
        name="ssd_scan",
    )(z, xbc, sm, smt, conv0, h0, conv_w, conv_b, prow, pcol, norm_w)


def _rope_tables(pos, heads):
    half = ROPE_DIM // 2
    inv_freq = ROPE_THETA ** (-jnp.arange(half, dtype=F32) / half)
    ang = pos.astype(F32)[:, None] * inv_freq[None, :]
    cos, sin = jnp.cos(ang), jnp.sin(ang)
    n = pos.shape[0]
    pad = jnp.zeros((n, SWA_D - ROPE_DIM), F32)
    c_h = jnp.concatenate([cos, cos, pad + 1.0], -1)
    a_h = jnp.concatenate([-sin, jnp.zeros_like(sin), pad], -1)
    b_h = jnp.concatenate([jnp.zeros_like(sin), sin, pad], -1)
    return tuple(jnp.tile(t, (1, heads)) for t in (c_h, a_h, b_h))


def _rope(x, tc, ta, tb):
    half = ROPE_DIM // 2
    w = x.shape[-1]
    return x * tc + pltpu.roll(x, w - half, 1) * ta + pltpu.roll(x, half, 1) * tb


def _sink_attend(q, k, v, valid, sink):
    s = _bdot(q, k, _dot_nt) * (SWA_D ** -0.5)
    s = jnp.where(valid, s, -jnp.inf)
    m = jnp.maximum(jnp.max(s, -1, keepdims=True), sink)
    p = jnp.exp(s - m)
    den = jnp.sum(p, -1, keepdims=True) + jnp.exp(sink - m)
    return _bdot(p / den, v)


def _swa_prompt_kernel(q_ref, kp_ref, kc_ref, vp_ref, vc_ref, qt_ref, ktp_ref, ktc_ref, sink_ref,
                       o_ref, ko_ref, vo_ref):
    i = pl.program_id(1)
    w = WINDOW
    q = _rope(q_ref[...], qt_ref[0], qt_ref[1], qt_ref[2])
    kc = _rope(kc_ref[...], ktc_ref[0], ktc_ref[1], ktc_ref[2])
    kp = _rope(kp_ref[...], ktp_ref[0], ktp_ref[1], ktp_ref[2])
    vc = vc_ref[...]
    ko_ref[...] = kc
    vo_ref[...] = vc
    kk = jnp.concatenate([kp, kc], 0)
    vv = jnp.concatenate([vp_ref[...], vc], 0)
    qi = _iota2((w, 2 * w), 0) + w
    kj = _iota2((w, 2 * w), 1)
    valid = (kj <= qi) & (qi - kj < w) & ((kj >= w) | (i > 0))
    valid = jnp.concatenate([valid] * SWA_GRP, 0)
    for g in range(SWA_KV):
        kg = kk[:, g * SWA_D:(g + 1) * SWA_D]
        vg = vv[:, g * SWA_D:(g + 1) * SWA_D]
        qg = jnp.concatenate([q[:, (g * SWA_GRP + j) * SWA_D:(g * SWA_GRP + j + 1) * SWA_D]
                              for j in range(SWA_GRP)], 0)
        sk = jnp.concatenate([jnp.broadcast_to(sink_ref[0:1, g * SWA_GRP + j:g * SWA_GRP + j + 1], (w, 1))
                              for j in range(SWA_GRP)], 0)
        og = _sink_attend(qg, kg, vg, valid, sk)
        for j in range(SWA_GRP):
            hd = g * SWA_GRP + j
            o_ref[:, hd * SWA_D:(hd + 1) * SWA_D] = og[j * w:(j + 1) * w].astype(o_ref.dtype)


def _swa_prompt(proj, qt, kt, sinks):
    b, l = proj.shape[0], proj.shape[1]
    w = WINDOW
    nb = l // w
    kvw = SWA_KV * SWA_D
    qw = SWA_HEADS * SWA_D
    prev = lambda i, n: (i, jnp.maximum(n - 1, 0), COL_SWK // kvw)
    prev_v = lambda i, n: (i, jnp.maximum(n - 1, 0), COL_SWV // kvw)
    return pl.pallas_call(
        _swa_prompt_kernel,
        grid=(b, nb),
        in_specs=[pl.BlockSpec((None, w, qw), lambda i, n: (i, n, COL_SWQ // qw)),
                  pl.BlockSpec((None, w, kvw), prev),
                  pl.BlockSpec((None, w, kvw), lambda i, n: (i, n, COL_SWK // kvw)),
                  pl.BlockSpec((None, w, kvw), prev_v),
                  pl.BlockSpec((None, w, kvw), lambda i, n: (i, n, COL_SWV // kvw)),
                  pl.BlockSpec((3, w, qw), lambda i, n: (0, n, 0)),
                  pl.BlockSpec((3, w, kvw), lambda i, n: (0, jnp.maximum(n - 1, 0), 0)),
                  pl.BlockSpec((3, w, kvw), lambda i, n: (0, n, 0)),
                  pl.BlockSpec((1, SWA_HEADS), lambda i, n: (0, 0))],
        out_specs=[pl.BlockSpec((None, w, qw), lambda i, n: (i, n, 0)),
                   pl.BlockSpec((None, w, kvw), lambda i, n: (i, 0, 0)),
                   pl.BlockSpec((None, w, kvw), lambda i, n: (i, 0, 0))],
        out_shape=[jax.ShapeDtypeStruct((b, l, qw), BF16),
                   jax.ShapeDtypeStruct((b, w, kvw), F32),
                   jax.ShapeDtypeStruct((b, w, kvw), F32)],
        compiler_params=_cparams(("arbitrary", "arbitrary")),
        name="swa_prompt",
    )(proj, proj, proj, proj, proj, qt, kt, kt, sinks)


def _swa_sample_kernel(q_ref, k_ref, v_ref, kb_ref, vb_ref, qt_ref, kt_ref, sink_ref,
                       o_ref, ko_ref, vo_ref, kk_ref, vv_ref, *, l):
    w = WINDOW
    lp = SUBLANES
    q = _rope(q_ref[...], qt_ref[0], qt_ref[1], qt_ref[2])
    k = _rope(k_ref[...], kt_ref[0], kt_ref[1], kt_ref[2])
    kk_ref[0:w, :] = kb_ref[...]
    kk_ref[w:w + lp, :] = k
    vv_ref[0:w, :] = vb_ref[...]
    vv_ref[w:w + lp, :] = v_ref[...]
    ko_ref[...] = kk_ref[l:l + w, :]
    vo_ref[...] = vv_ref[l:l + w, :]
    kk = kk_ref[...]
    vv = vv_ref[...]
    qi = _iota2((lp, w + lp), 0) + w
    kj = _iota2((lp, w + lp), 1)
    valid = (kj <= qi) & (qi - kj < w) & (kj < w + l)
    valid = jnp.concatenate([valid] * SWA_GRP, 0)
    for g in range(SWA_KV):
        kg = kk[:, g * SWA_D:(g + 1) * SWA_D]
        vg = vv[:, g * SWA_D:(g + 1) * SWA_D]
        qg = jnp.concatenate([q[:, (g * SWA_GRP + j) * SWA_D:(g * SWA_GRP + j + 1) * SWA_D]
                              for j in range(SWA_GRP)], 0)
        sk = jnp.concatenate([jnp.broadcast_to(sink_ref[0:1, g * SWA_GRP + j:g * SWA_GRP + j + 1], (lp, 1))
                              for j in range(SWA_GRP)], 0)
        og = _sink_attend(qg, kg, vg, valid, sk)
        for j in range(SWA_GRP):
            hd = g * SWA_GRP + j
            o_ref[:, hd * SWA_D:(hd + 1) * SWA_D] = og[j * lp:(j + 1) * lp]


def _swa_sample(q, k, v, kbuf, vbuf, qt, kt, sinks, l):
    b = q.shape[0]
    w = WINDOW
    lp = SUBLANES
    kvw = SWA_KV * SWA_D
    qw = SWA_HEADS * SWA_D
    kern = functools.partial(_swa_sample_kernel, l=l)
    return pl.pallas_call(
        kern,
        grid=(b,),
        in_specs=[pl.BlockSpec((None, lp, qw), lambda i: (i, 0, 0)),
                  pl.BlockSpec((None, lp, kvw), lambda i: (i, 0, 0)),
                  pl.BlockSpec((None, lp, kvw), lambda i: (i, 0, 0)),
                  pl.BlockSpec((None, w, kvw), lambda i: (i, 0, 0)),
                  pl.BlockSpec((None, w, kvw), lambda i: (i, 0, 0)),
                  pl.BlockSpec((3, lp, qw), lambda i: (0, 0, 0)),
                  pl.BlockSpec((3, lp, kvw), lambda i: (0, 0, 0)),
                  pl.BlockSpec((1, SWA_HEADS), lambda i: (0, 0))],
        out_specs=[pl.BlockSpec((None, lp, qw), lambda i: (i, 0, 0)),
                   pl.BlockSpec((None, w, kvw), lambda i: (i, 0, 0)),
                   pl.BlockSpec((None, w, kvw), lambda i: (i, 0, 0))],
        out_shape=[jax.ShapeDtypeStruct((b, lp, qw), F32),
                   jax.ShapeDtypeStruct((b, w, kvw), F32),
                   jax.ShapeDtypeStruct((b, w, kvw), F32)],
        scratch_shapes=[pltpu.VMEM((w + lp, kvw), F32), pltpu.VMEM((w + lp, kvw), F32)],
        compiler_params=_cparams(("arbitrary",)),
        name="swa_sample",
    )(q, k, v, kbuf, vbuf, qt, kt, sinks)


def _merge_kernel(oa_ref, ob_ref, oc_ref, ga_ref, gb_ref, gc_ref, wa_ref, wb_ref, wc_ref, o_ref):
    acc = _sigmoid(ga_ref[...]) * _bdot(oa_ref[...], wa_ref[...])
    acc = acc + _sigmoid(gb_ref[...]) * _bdot(ob_ref[...], wb_ref[...])
    acc = acc + _sigmoid(gc_ref[...]) * _bdot(oc_ref[...], wc_ref[...])
    o_ref[...] = acc.astype(o_ref.dtype)


def _merge(oa, ob, oc, proj, wbr, tm):
    t = oa.shape[0]
    tn = 512
    nj = D_MODEL // tn
    gate = lambda k: pl.BlockSpec((tm, tn), lambda i, j: (i, COL_MG // tn + k * nj + j))
    wsp = lambda k: pl.BlockSpec((None, BRANCH_W, tn), lambda i, j: (k, 0, j))
    osp = pl.BlockSpec((tm, BRANCH_W), lambda i, j: (i, 0))
    return pl.pallas_call(
        _merge_kernel,
        grid=(t // tm, nj),
        in_specs=[osp, osp, osp, gate(0), gate(1), gate(2), wsp(0), wsp(1), wsp(2)],
        out_specs=pl.BlockSpec((tm, tn), lambda i, j: (i, j)),
        out_shape=jax.ShapeDtypeStruct((t, D_MODEL), BF16),
        compiler_params=_cparams(("arbitrary", "arbitrary")),
        name="merge",
    )(oa, ob, oc, proj, proj, proj, wbr, wbr, wbr)


def _layer_norm(r, g, b):
    mu = jnp.mean(r, -1, keepdims=True)
    rc = r - mu
    var = jnp.mean(rc * rc, -1, keepdims=True)
    return rc * lax.rsqrt(var + LN_EPS) * g + b


def _out_ln_kernel(m_ref, w_ref, x_ref, g1_ref, sc_ref, sh_ref, lg_ref, lb_ref, x1_ref, h2_ref, *, alpha):
    mix = _dot(m_ref[...], w_ref[...])
    x1 = _layer_norm(alpha * x_ref[...] + g1_ref[...] * mix, lg_ref[...], lb_ref[...])
    x1_ref[...] = x1
    h2_ref[...] = x1 * (1.0 + sc_ref[...]) + sh_ref[...]


def _out_ln(merged, w_out, x, g1, sc2, sh2, ln_g, ln_b, tm, tiles_per_group, alpha):
    t, d = x.shape
    row = pl.BlockSpec((tm, d), lambda i: (i, 0))
    vec = pl.BlockSpec((1, d), lambda i: (0, 0))
    kern = functools.partial(_out_ln_kernel, alpha=alpha)
    return pl.pallas_call(
        kern,
        grid=(t // tm,),
        in_specs=[row, pl.BlockSpec((d, d), lambda i: (0, 0)), row,
                  _mod_spec(g1, tm, tiles_per_group), _mod_spec(sc2, tm, tiles_per_group),
                  _mod_spec(sh2, tm, tiles_per_group), vec, vec],
        out_specs=[row, row],
        out_shape=[jax.ShapeDtypeStruct((t, d), F32), jax.ShapeDtypeStruct((t, d), F32)],
        compiler_params=_cparams(("arbitrary",)),
        name="out_ln",
    )(merged, w_out, x, g1, sc2, sh2, ln_g, ln_b)


ROUTER_TR = 640


def _router_kernel(h_ref, w_ref, b_ref, eidx_ref, rank_ref, wgt_ref, cnt_ref, carry_ref):
    i = pl.program_id(0)
    tr = h_ref.shape[0]
    ng, gs = N_GROUPS, GROUP_SIZE

    @pl.when(i == 0)
    def _():
        carry_ref[...] = jnp.zeros_like(carry_ref)

    logits = _dot3(w_ref[...], h_ref[...], _dot_nt)
    scores = _sigmoid(logits)
    sc3 = scores.reshape(ng, gs, tr)
    ch3 = (scores + b_ref[...]).reshape(ng, gs, tr)
    io_e = _iota2((ng, gs, tr), 1)
    io_g = _iota2((ng, 1, tr), 0)
    io_x = _iota2((ng, gs, tr), 0) * gs + io_e
    ninf = -jnp.inf

    m1 = jnp.max(ch3, 1, keepdims=True)
    i1 = jnp.min(jnp.where(ch3 == m1, io_e, gs), 1, keepdims=True)
    m2 = jnp.max(jnp.where(io_e == i1, ninf, ch3), 1, keepdims=True)
    grp = m1 + m2
    keep = jnp.zeros((ng, 1, tr), jnp.bool_)
    for _ in range(TOPK_GROUPS):
        m = jnp.max(grp, 0, keepdims=True)
        first = jnp.min(jnp.where(grp == m, io_g, ng), 0, keepdims=True)
        hit = io_g == first
        keep = keep | hit
        grp = jnp.where(hit, ninf, grp)

    cm = jnp.where(keep, ch3, ninf)
    hits = []
    firsts = []
    for _ in range(TOP_K):
        m = jnp.max(jnp.max(cm, 1, keepdims=True), 0, keepdims=True)
        cand = jnp.where(cm == m, io_x, N_EXPERTS)
        first = jnp.min(jnp.min(cand, 1, keepdims=True), 0, keepdims=True)
        hit = io_x == first
        hits.append(hit)
        firsts.append(first)
        cm = jnp.where(hit, ninf, cm)
    sel = hits[0]
    for hit in hits[1:]:
        sel = sel | hit
    self32 = jnp.where(sel, 1.0, 0.0).astype(F32)
    wsel = sc3 * self32
    den = jnp.sum(jnp.sum(wsel, 1, keepdims=True), 0, keepdims=True)
    comb = wsel / den * ROUTED_SCALE

    sel2 = self32.reshape(N_EXPERTS, tr)
    upper = jnp.where(_iota2((tr, tr), 0) <= _iota2((tr, tr), 1), 1.0, 0.0).astype(BF16)
    incl = _dot(sel2.astype(BF16), upper)
    carry = carry_ref[:, 0:1]
    rank3 = (carry + incl - sel2).reshape(ng, gs, tr)
    for r in range(TOP_K):
        hf = jnp.where(hits[r], 1.0, 0.0).astype(F32)
        rk = jnp.sum(jnp.sum(hf * rank3, 1, keepdims=True), 0, keepdims=True)
        wg = jnp.sum(jnp.sum(hf * comb, 1, keepdims=True), 0, keepdims=True)
        eidx_ref[r:r + 1, :] = firsts[r].reshape(1, tr)
        rank_ref[r:r + 1, :] = rk.reshape(1, tr).astype(I32)
        wgt_ref[r:r + 1, :] = wg.reshape(1, tr)
    new_carry = carry + incl[:, tr - 1:tr]
    carry_ref[...] = jnp.broadcast_to(new_carry, carry_ref.shape)
    cnt_ref[...] = jnp.broadcast_to(new_carry, cnt_ref.shape).astype(I32)


def _router(h2, wr_t, bias_col):
    t, d = h2.shape
    tr = ROUTER_TR
    out = pl.BlockSpec((TOP_K, tr), lambda i: (0, i))
    return pl.pallas_call(
        _router_kernel,
        grid=(t // tr,),
        in_specs=[pl.BlockSpec((tr, d), lambda i: (i, 0)),
                  pl.BlockSpec((N_EXPERTS, d), lambda i: (0, 0)),
                  pl.BlockSpec((N_EXPERTS, 1), lambda i: (0, 0))],
        out_specs=[out, out, out, pl.BlockSpec((N_EXPERTS, LANES), lambda i: (0, 0))],
        out_shape=[jax.ShapeDtypeStruct((TOP_K, t), I32), jax.ShapeDtypeStruct((TOP_K, t), I32),
                   jax.ShapeDtypeStruct((TOP_K, t), F32), jax.ShapeDtypeStruct((N_EXPERTS, LANES), I32)],
        scratch_shapes=[pltpu.VMEM((N_EXPERTS, LANES), F32)],
        compiler_params=_cparams(("arbitrary",)),
        name="router",
    )(h2, wr_t, bias_col)


def _invert_kernel(slot_ref, tok_ref):
    def init(r, carry):
        for u in range(LANES):
            tok_ref[r, u] = 0
        return carry
    lax.fori_loop(0, tok_ref.shape[0], init, 0)

    def body(t, carry):
        for k in range(TOP_K):
            s = slot_ref[k, t]
            tok_ref[s // LANES, s % LANES] = t
        return carry
    lax.fori_loop(0, slot_ref.shape[1], body, 0)


def _invert(slots, n_rows):
    return pl.pallas_call(
        _invert_kernel,
        grid_spec=pltpu.PrefetchScalarGridSpec(
            num_scalar_prefetch=1,
            grid=(1,),
            in_specs=[],
            out_specs=pl.BlockSpec(memory_space=pltpu.SMEM)),
        out_shape=jax.ShapeDtypeStruct((n_rows // LANES, LANES), I32),
        compiler_params=_cparams(("arbitrary",)),
        name="invert_slots",
    )(slots)


def _cast_weights(te_ref, wg_ref, wu_ref, wd_ref, wgb_ref, wub_ref, wdb_ref):
    i = pl.program_id(0)
    prev = te_ref[jnp.maximum(i - 1, 0)]

    @pl.when((i == 0) | (te_ref[i] != prev))
    def _():
        wgb_ref[...] = wg_ref[...].astype(BF16)
        wub_ref[...] = wu_ref[...].astype(BF16)
        wdb_ref[...] = wd_ref[...].astype(BF16)


def _swiglu(xb, wgb_ref, wub_ref, wdb_ref):
    a = _dot(xb, wgb_ref[...])
    u = _dot(xb, wub_ref[...])
    return _dot((_silu(a) * u).astype(BF16), wdb_ref[...])


def _ffn_kernel(te_ref, na_ref, x_ref, wg_ref, wu_ref, wd_ref, y_ref, wgb_ref, wub_ref, wdb_ref):
    _cast_weights(te_ref, wg_ref, wu_ref, wd_ref, wgb_ref, wub_ref, wdb_ref)
    y_ref[...] = _swiglu(x_ref[...].astype(BF16), wgb_ref, wub_ref, wdb_ref)


def _gffn_kernel(te_ref, na_ref, tok_ref, h_ref, wg_ref, wu_ref, wd_ref, y_ref,
                 wgb_ref, wub_ref, wdb_ref, xbuf_ref, sem, *, tm):
    i = pl.program_id(0)
    na = na_ref[0]
    unroll = SUBLANES

    def row_copy(t, b, r):
        return pltpu.make_async_copy(h_ref.at[pl.ds(t, 1)], xbuf_ref.at[b, pl.ds(r, 1)], sem.at[b])

    def issue(tile, b):
        def body(j, carry):
            for u in range(unroll):
                r = j * unroll + u
                f = tile * tm + r
                row_copy(tok_ref[f // LANES, f % LANES], b, r).start()
            return carry
        lax.fori_loop(0, tm // unroll, body, 0)

    @pl.when(i == 0)
    def _():
        issue(0, 0)

    @pl.when(i + 1 < na)
    def _():
        issue(i + 1, (i + 1) % 2)

    _cast_weights(te_ref, wg_ref, wu_ref, wd_ref, wgb_ref, wub_ref, wdb_ref)

    @pl.when(i < na)
    def _():
        b = i % 2

        def wbody(j, carry):
            for u in range(unroll):
                row_copy(0, b, 0).wait()
            return carry
        lax.fori_loop(0, tm // unroll, wbody, 0)
        y_ref[...] = _swiglu(xbuf_ref[b].astype(BF16), wgb_ref, wub_ref, wdb_ref)

    @pl.when(i >= na)
    def _():
        y_ref[...] = jnp.zeros_like(y_ref)


def _wspecs(layer, d, ff):
    wmap = lambda i, te, *_: (layer, te[i], 0, 0)
    return [pl.BlockSpec((None, None, d, ff), wmap), pl.BlockSpec((None, None, d, ff), wmap),
            pl.BlockSpec((None, None, ff, d), wmap)]


def _wscratch(d, ff):
    return [pltpu.VMEM((d, ff), BF16), pltpu.VMEM((d, ff), BF16), pltpu.VMEM((ff, d), BF16)]


def _ffn(te, na, x, wg, wu, wd, layer, tm):
    t, d = x.shape
    ff = wg.shape[-1]
    return pl.pallas_call(
        _ffn_kernel,
        grid_spec=pltpu.PrefetchScalarGridSpec(
            num_scalar_prefetch=2,
            grid=(t // tm,),
            in_specs=[pl.BlockSpec((tm, d), lambda i, *_: (i, 0))] + _wspecs(layer, d, ff),
            out_specs=pl.BlockSpec((tm, d), lambda i, *_: (i, 0)),
            scratch_shapes=_wscratch(d, ff)),
        out_shape=jax.ShapeDtypeStruct((t, d), F32),
        compiler_params=_cparams(("arbitrary",)),
        name="shared_ffn",
    )(te, na, x, wg, wu, wd)


def _gffn(te, na, tok, h2, wg, wu, wd, layer, tm):
    d = h2.shape[1]
    ff = wg.shape[-1]
    n_rows = tok.shape[0] * tok.shape[1]
    kern = functools.partial(_gffn_kernel, tm=tm)
    return pl.pallas_call(
        kern,
        grid_spec=pltpu.PrefetchScalarGridSpec(
            num_scalar_prefetch=3,
            grid=(n_rows // tm,),
            in_specs=[pl.BlockSpec(memory_space=pl.ANY)] + _wspecs(layer, d, ff),
            out_specs=pl.BlockSpec((tm, d), lambda i, *_: (i, 0)),
            scratch_shapes=_wscratch(d, ff) + [pltpu.VMEM((2, tm, d), F32), pltpu.SemaphoreType.DMA((2,))]),
        out_shape=jax.ShapeDtypeStruct((n_rows, d), F32),
        compiler_params=_cparams(("arbitrary",)),
        name="expert_ffn",
    )(te, na, tok, h2, wg, wu, wd)


COMBINE_TC = 32


def _combine_kernel(slot_ref, ys_ref, w_ref, sh_ref, x_ref, g2_ref, lg_ref, lb_ref, o_ref, buf_ref, sem, *, alpha):
    i = pl.program_id(0)
    n = pl.num_programs(0)
    tc = COMBINE_TC

    def row_copy(s, b, r):
        return pltpu.make_async_copy(ys_ref.at[pl.ds(s, 1)], buf_ref.at[b, pl.ds(r, 1)], sem.at[b])

    def issue(tile, b):
        def body(j, carry):
            for k in range(TOP_K):
                row_copy(slot_ref[k, tile * tc + j], b, k * tc + j).start()
            return carry
        lax.fori_loop(0, tc, body, 0)

    @pl.when(i == 0)
    def _():
        issue(0, 0)

    @pl.when(i + 1 < n)
    def _():
        issue(i + 1, (i + 1) % 2)

    b = i % 2

    def wbody(j, carry):
        for k in range(TOP_K):
            row_copy(0, b, 0).wait()
        return carry
    lax.fori_loop(0, tc, wbody, 0)

    w = w_ref[...]
    acc = sh_ref[...]
    for k in range(TOP_K):
        acc = acc + w[:, k:k + 1] * buf_ref[b, k * tc:(k + 1) * tc, :]
    o_ref[...] = _layer_norm(alpha * x_ref[...] + g2_ref[...] * acc, lg_ref[...], lb_ref[...])


def _combine(slots, ys, wgt, shared, x1, g2rows, ln_g, ln_b, alpha):
    t, d = x1.shape
    tc = COMBINE_TC
    row = pl.BlockSpec((tc, d), lambda i, *_: (i, 0))
    vec = pl.BlockSpec((1, d), lambda i, *_: (0, 0))
    kern = functools.partial(_combine_kernel, alpha=alpha)
    return pl.pallas_call(
        kern,
        grid_spec=pltpu.PrefetchScalarGridSpec(
            num_scalar_prefetch=1,
            grid=(t // tc,),
            in_specs=[pl.BlockSpec(memory_space=pl.ANY),
                      pl.BlockSpec((tc, TOP_K), lambda i, *_: (i, 0)),
                      row, row, row, vec, vec],
            out_specs=row,
            scratch_shapes=[pltpu.VMEM((2, TOP_K * tc, d), F32), pltpu.SemaphoreType.DMA((2,))]),
        out_shape=jax.ShapeDtypeStruct((t, d), F32),
        compiler_params=_cparams(("arbitrary",)),
        name="combine_ln",
    )(slots, ys, wgt, shared, x1, g2rows, ln_g, ln_b)


def _moe(h2, x1, g2rows, p, layer, alpha):
    t, d = h2.shape
    tm = MOE_TM
    eidx, rank, wgt, cnt = _router(h2, p['w_router'].T, p['router_bias'].reshape(N_EXPERTS, 1))
    counts = cnt[:, 0]
    tiles = (counts + tm - 1) // tm
    tile_end = jnp.cumsum(tiles)
    offs = ((tile_end - tiles) * tm).astype(I32)
    n_active = tile_end[-1]
    n_tiles = (t * TOP_K + N_EXPERTS * (tm - 1)) // tm
    ids = jnp.minimum(jnp.arange(n_tiles, dtype=I32), n_active - 1)
    te = jnp.sum((tile_end[None, :] <= ids[:, None]).astype(I32), axis=1)
    onehot = eidx[..., None] == jnp.arange(N_EXPERTS, dtype=I32)
    slots = rank + jnp.sum(jnp.where(onehot, offs, 0), -1)
    tok = _invert(slots, n_tiles * tm)
    ys = _gffn(te, n_active.reshape(1).astype(I32), tok, h2, p['w_exp_gate'], p['w_exp_up'], p['w_exp_down'],
               layer, tm)
    tsh = 640
    nsh = t // tsh
    sh4 = lambda w: w.reshape((w.shape[0], 1) + w.shape[1:])
    shared = _ffn(jnp.zeros((nsh,), I32), jnp.full((1,), nsh, I32), h2,
                  sh4(p['w_sh_gate']), sh4(p['w_sh_up']), sh4(p['w_sh_down']), layer, tsh)
    return _combine(slots, ys, wgt.T, shared, x1, g2rows, p['ln2_g'].reshape(1, d), p['ln2_b'].reshape(1, d), alpha)


def _chunk_t(sm, c):
    b, l, _ = sm.shape
    return jnp.swapaxes(sm.reshape(b, l // c, c, SMALL_W)[..., :32], -1, -2)


def _pad_rows(a, rows):
    return jnp.pad(a, ((0, 0), (0, rows - a.shape[1]), (0, 0)))


def _mixer_group(x, mod, states, p, tabs, *, prompt):
    b, l, d = x.shape
    t = b * l
    dn_conv0, dn_s0, ssm_conv0, ssm_h0, kv_buf = states
    sh1, sc1, g1, sh2, sc2, g2 = jnp.split(mod, 6, axis=-1)
    if prompt:
        tm = 1024
        tpg = l // tm
        shape = lambda m: m.reshape(b, 1, d)
    else:
        tm = t
        tpg = 1
        shape = lambda m: jnp.repeat(m, l, axis=0).reshape(1, t, d)
    proj, small = _in_proj(x.reshape(t, d), shape(sc1), shape(sh1), p['w_main'], p['w_small'], tm, tpg)
    proj3 = proj.reshape(b, l, MAIN_W)
    small3 = small.reshape(b, l, SMALL_W)
    if prompt:
        c = CHUNK
        dn_in = (proj3, proj3)
        dn_cb = dict(cb_qkv=COL_QKV // DN_CONV_DIM, cb_gate=COL_DNG // DN_KEY)
        ssd_in = (proj3, proj3)
        ssd_cb = dict(cb_z=COL_SSZ // SSM_INNER, cb_xbc=COL_XBC // SSM_CONV_DIM)
        sm_in = small3
        odt = BF16
    else:
        c = SUBLANES
        cut = lambda c0, w: _pad_rows(proj3[:, :, c0:c0 + w], c)
        dn_in = (cut(COL_QKV, DN_CONV_DIM), cut(COL_DNG, DN_KEY))
        dn_cb = dict(cb_qkv=0, cb_gate=0)
        ssd_in = (cut(COL_SSZ, SSM_INNER), cut(COL_XBC, SSM_CONV_DIM))
        ssd_cb = dict(cb_z=0, cb_xbc=0)
        sm_in = _pad_rows(small3, c)
        odt = F32
    valid = c if prompt else l
    smt = _chunk_t(sm_in, c)
    o_a, dn_conv, dn_s = _dn(dn_in[0], dn_in[1], sm_in, smt, dn_conv0, dn_s0, p['dn_conv_w'], p['dn_prow'],
                             p['dn_pcol'], p['dn_norm_w'].reshape(1, DN_D), c=c, valid=valid, out_dtype=odt, **dn_cb)
    o_b, ssm_conv, ssm_h = _ssd(ssd_in[0], ssd_in[1], sm_in, smt, ssm_conv0, ssm_h0, p['ssm_conv_w'],
                                p['ssm_conv_b'].reshape(1, SSM_CONV_DIM), p['ssm_prow'], p['ssm_pcol'],
                                p['ssm_norm_w'].reshape(1, SSM_INNER), c=c, valid=valid, out_dtype=odt, **ssd_cb)
    sinks = p['swa_sinks'].reshape(1, SWA_HEADS)
    if prompt:
        o_c, k_new, v_new = _swa_prompt(proj3, tabs[0], tabs[1], sinks)
    else:
        o_c, k_new, v_new = _swa_sample(cut(COL_SWQ, SWA_HEADS * SWA_D), cut(COL_SWK, SWA_KV * SWA_D),
                                        cut(COL_SWV, SWA_KV * SWA_D),
                                        kv_buf[0].reshape(b, WINDOW, SWA_KV * SWA_D),
                                        kv_buf[1].reshape(b, WINDOW, SWA_KV * SWA_D), tabs[0], tabs[1], sinks, l)
    flat = lambda o: o[:, :l].reshape(t, BRANCH_W).astype(BF16)
    merged = _merge(flat(o_a), flat(o_b), flat(o_c), proj, p['w_branch'], tm=min(tm, 512))
    tm2 = min(tm, 512)
    tpg2 = l // tm2 if prompt else 1
    shape2 = (lambda m: m.reshape(b, 1, d)) if prompt else shape
    x1, h2 = _out_ln(merged, p['w_out'], x.reshape(t, d), shape2(g1), shape2(sc2), shape2(sh2),
                     p['ln1_g'].reshape(1, d), p['ln1_b'].reshape(1, d), tm2, tpg2, p['alpha'])
    g2rows = jnp.repeat(g2, l, axis=0)
    k_new = k_new.reshape(b, WINDOW, SWA_KV, SWA_D)
    v_new = v_new.reshape(b, WINDOW, SWA_KV, SWA_D)
    return x1, h2, g2rows, (dn_conv, dn_s, ssm_conv, ssm_h, k_new, v_new)


def _forward(x_prompt, x_sample, state_dn_conv, state_dn, state_ssm_conv, state_ssm, cache_swa_k, cache_swa_v,
             c_prompt, c_sample, w_ada, b_ada, w_in, dn_conv_w, dn_a_log, dn_dt_bias, dn_norm_w,
             ssm_conv_w, ssm_conv_b, ssm_a_log, ssm_dt_bias, ssm_d, ssm_norm_w, swa_sinks, w_branch, w_out,
             ln1_g, ln1_b, w_router, router_bias, w_exp_gate, w_exp_up, w_exp_down, w_sh_gate, w_sh_up,
             w_sh_down, ln2_g, ln2_b):
    depth = w_in.shape[0]
    bp, lp, d = x_prompt.shape
    bs, ls, _ = x_sample.shape
    tp, ts = bp * lp, bs * ls
    alpha = (2 * depth) ** 0.25
    past_len = 16384

    c_all = jnp.concatenate([c_prompt, c_sample, jnp.zeros((4, d), F32)], 0)
    mod_all = _ada(c_all, w_ada, b_ada)
    tabs_p = tuple(jnp.stack(_rope_tables(jnp.arange(lp), h)) for h in (SWA_HEADS, SWA_KV))
    tabs_s = tuple(jnp.stack(_rope_tables(past_len + jnp.arange(SUBLANES), h)) for h in (SWA_HEADS, SWA_KV))
    init_p = (jnp.zeros((bp, CONV_W - 1, DN_CONV_DIM), F32), jnp.zeros((bp, DN_HEADS, DN_D, DN_D), F32),
              jnp.zeros((bp, CONV_W - 1, SSM_CONV_DIM), F32), jnp.zeros((bp, SSM_HEADS, SSM_P, SSM_N), F32), None)

    def pad_lanes(v, at):
        return jnp.zeros((SMALL_W,), F32).at[at:at + v.shape[0]].set(v)

    yp, ys = x_prompt, x_sample
    new_p, new_s = [], []
    for l in range(depth):
        w = w_in[l]
        seg = lambda a, n: w[:, a:a + n]
        w_main = jnp.concatenate([seg(8224, 6144), seg(0, 3072), seg(5136, 1536), seg(7712, 256), seg(7968, 256),
                                  seg(3072, 1024), seg(4112, 1024), seg(6688, 1024)], 1).astype(BF16)
        w_small = jnp.concatenate([seg(4096, 16), seg(6672, 16), jnp.zeros((d, SMALL_W - 32), F32)], 1).astype(BF16)
        p = {'w_main': w_main, 'w_small': w_small, 'alpha': alpha,
             'dn_conv_w': dn_conv_w[l], 'dn_norm_w': dn_norm_w[l],
             'dn_prow': jnp.stack([pad_lanes(dn_a_log[l], SM_A), pad_lanes(dn_dt_bias[l], SM_A)]),
             'dn_pcol': jnp.stack([dn_a_log[l], dn_dt_bias[l]], 1),
             'ssm_conv_w': ssm_conv_w[l], 'ssm_conv_b': ssm_conv_b[l], 'ssm_norm_w': ssm_norm_w[l],
             'ssm_prow': jnp.stack([pad_lanes(ssm_dt_bias[l], SM_DT), pad_lanes(ssm_a_log[l], SM_DT),
                                    pad_lanes(ssm_d[l], SM_DT)]),
             'ssm_pcol': jnp.stack([ssm_dt_bias[l], ssm_a_log[l]], 1),
             'swa_sinks': swa_sinks[l], 'w_branch': w_branch[l].astype(BF16), 'w_out': w_out[l].astype(BF16),
             'ln1_g': ln1_g[l], 'ln1_b': ln1_b[l], 'w_router': w_router[l], 'router_bias': router_bias[l],
             'w_exp_gate': w_exp_gate, 'w_exp_up': w_exp_up, 'w_exp_down': w_exp_down,
             'w_sh_gate': w_sh_gate, 'w_sh_up': w_sh_up, 'w_sh_down': w_sh_down,
             'ln2_g': ln2_g[l], 'ln2_b': ln2_b[l]}
        mod = mod_all[l]
        x1p, h2p, g2p, st_p = _mixer_group(yp, mod[:bp], init_p, p, tabs_p, prompt=True)
        st_in = (state_dn_conv[l], state_dn[l], state_ssm_conv[l], state_ssm[l], (cache_swa_k[l], cache_swa_v[l]))
        x1s, h2s, g2s, st_s = _mixer_group(ys, mod[bp:bp + bs], st_in, p, tabs_s, prompt=False)
        cat = lambda a, b_: jnp.concatenate([a, b_], 0)
        x2 = _moe(cat(h2p, h2s), cat(x1p, x1s), cat(g2p, g2s), p, l, alpha)
        yp = x2[:tp].reshape(bp, lp, d)
        ys = x2[tp:].reshape(bs, ls, d)
        new_p.append(st_p)
        new_s.append(st_s)
    outs = [yp, ys]
    for k in range(6):
        outs.append(jnp.stack([s[k] for s in new_p]))
        outs.append(jnp.stack([s[k] for s in new_s]))
    return tuple(outs)


def kernel(x_prompt, x_sample, state_dn_conv, state_dn, state_ssm_conv, state_ssm, cache_swa_k, cache_swa_v, c_prompt, c_sample, w_ada, b_ada, w_in, dn_conv_w, dn_a_log, dn_dt_bias, dn_norm_w, ssm_conv_w, ssm_conv_b, ssm_a_log, ssm_dt_bias, ssm_d, ssm_norm_w, swa_sinks, w_branch, w_out, ln1_g, ln1_b, w_router, router_bias, w_exp_gate, w_exp_up, w_exp_down, w_sh_gate, w_sh_up, w_sh_down, ln2_g, ln2_b):
    return _forward(x_prompt, x_sample, state_dn_conv, state_dn, state_ssm_conv, state_ssm, cache_swa_k, cache_swa_v, c_prompt, c_sample, w_ada, b_ada, w_in, dn_conv_w, dn_a_log, dn_dt_bias, dn_norm_w, ssm_conv_w, ssm_conv_b, ssm_a_log, ssm_dt_bias, ssm_d, ssm_norm_w, swa_sinks, w_branch, w_out, ln1_g, ln1_b, w_router, router_bias, w_exp_gate, w_exp_up, w_exp_down, w_sh_gate, w_sh_up, w_sh_down, ln2_g, ln2_b)
```

```python
import functools
import math

import jax
import jax.numpy as jnp
from jax import lax
from jax.experimental import pallas as pl
from jax.experimental.pallas import tpu as pltpu

F32 = jnp.float32
BF16 = jnp.bfloat16
I32 = jnp.int32

D_MODEL = 2048
CONV_W = 4
DN_HEADS = 8
DN_D = 128
DN_KEY = DN_HEADS * DN_D
DN_CONV_DIM = 3 * DN_KEY
SSM_HEADS = 16
SSM_P = 64
SSM_INNER = SSM_HEADS * SSM_P
SSM_GROUPS = 2
SSM_N = 128
SSM_CONV_DIM = SSM_INNER + 2 * SSM_GROUPS * SSM_N
SWA_HEADS = 16
SWA_KV = 4
SWA_D = 64
SWA_GRP = SWA_HEADS // SWA_KV
WINDOW = 128
ROPE_DIM = SWA_D // 4
ROPE_THETA = 500000.0
N_BRANCH = 3
BRANCH_W = 1024
N_EXPERTS = 64
EXPERT_FF = 512
TOP_K = 8
N_GROUPS = 8
GROUP_SIZE = N_EXPERTS // N_GROUPS
TOPK_GROUPS = 4
ROUTED_SCALE = 2.5
LN_EPS = 1e-5
NORM_EPS = 1e-6
CHUNK = 64

LANES = 128
LANE_BITS = 7
SUBLANES = 8
VMEM_LIMIT = 56 * 1024 * 1024

MAIN_W = 14336
COL_MG, COL_QKV, COL_XBC, COL_SWK, COL_SWV, COL_DNG, COL_SSZ, COL_SWQ = (
    0, 6144, 9216, 10752, 11008, 11264, 12288, 13312)
SMALL_W = LANES
SM_A, SM_B, SM_DT = 0, 8, 16

MOE_TM = 256
SLAB = D_MODEL // LANES
MOD_GROUP = 4


def _cparams(sem, vmem=VMEM_LIMIT):
    return pltpu.CompilerParams(dimension_semantics=sem, vmem_limit_bytes=vmem)


def _dot(a, b):
    return jnp.dot(a, b, preferred_element_type=F32)


def _dot_nt(a, b):
    return lax.dot_general(a, b, (((1,), (1,)), ((), ())), preferred_element_type=F32)


def _dot_tn(a, b):
    return lax.dot_general(a, b, (((0,), (0,)), ((), ())), preferred_element_type=F32)


def _split(a):
    hi = a.astype(BF16)
    lo = (a - hi.astype(F32)).astype(BF16)
    return hi, lo


def _dot3(a, b, dot=_dot):
    ah, al = _split(a)
    bh, bl = _split(b)
    return dot(ah, bh) + dot(ah, bl) + dot(al, bh)


def _bdot(a, b, dot=_dot):
    return dot(a.astype(BF16), b.astype(BF16))


def _silu(x):
    return x * (1.0 / (1.0 + jnp.exp(-x)))


def _sigmoid(x):
    return 1.0 / (1.0 + jnp.exp(-x))


def _softplus(x):
    return jnp.maximum(x, 0.0) + jnp.log(1.0 + jnp.exp(-jnp.abs(x)))


def _ada_kernel(c_ref, w_ref, b_ref, o_ref):
    c = c_ref[...]
    o_ref[...] = _bdot(_silu(c), w_ref[...]) + b_ref[...]


def _ada(c_all, w_ada, b_ada):
    depth, d, n = w_ada.shape
    rows = c_all.shape[0]
    tn = 1024
    return pl.pallas_call(
        _ada_kernel,
        grid=(depth, n // tn),
        in_specs=[pl.BlockSpec((rows, d), lambda l, j: (0, 0)),
                  pl.BlockSpec((None, d, tn), lambda l, j: (l, 0, j)),
                  pl.BlockSpec((None, 1, tn), lambda l, j: (l, 0, j))],
        out_specs=pl.BlockSpec((None, rows, tn), lambda l, j: (l, 0, j)),
        out_shape=jax.ShapeDtypeStruct((depth, rows, n), F32),
        compiler_params=_cparams(("arbitrary", "arbitrary")),
        name="ada",
    )(c_all, w_ada, b_ada.reshape(depth, 1, n))


def _mod_spec(mod, tm, tiles_per_group):
    r = mod.shape[1]
    if r == 1:
        return pl.BlockSpec((None, 1, mod.shape[2]), lambda i, *_: (i // tiles_per_group, 0, 0))
    return pl.BlockSpec((None, r, mod.shape[2]), lambda i, *_: (i, 0, 0))


def _in_proj_kernel(x_ref, sc_ref, sh_ref, wm_ref, ws_ref, proj_ref, small_ref, hb_ref):
    @pl.when(pl.program_id(1) == 0)
    def _():
        h = x_ref[...] * (1.0 + sc_ref[...]) + sh_ref[...]
        hb_ref[...] = h.astype(BF16)
        small_ref[...] = _dot(hb_ref[...], ws_ref[...])

    proj_ref[...] = _dot(hb_ref[...], wm_ref[...])


def _in_proj(x, sc, sh, w_main, w_small, tm, tiles_per_group):
    t, d = x.shape
    tn = 1024
    return pl.pallas_call(
        _in_proj_kernel,
        grid=(t // tm, MAIN_W // tn),
        in_specs=[pl.BlockSpec((tm, d), lambda i, j: (i, 0)),
                  _mod_spec(sc, tm, tiles_per_group),
                  _mod_spec(sh, tm, tiles_per_group),
                  pl.BlockSpec((d, tn), lambda i, j: (0, j)),
                  pl.BlockSpec((d, SMALL_W), lambda i, j: (0, 0))],
        out_specs=[pl.BlockSpec((tm, tn), lambda i, j: (i, j)),
                   pl.BlockSpec((tm, SMALL_W), lambda i, j: (i, 0))],
        out_shape=[jax.ShapeDtypeStruct((t, MAIN_W), F32),
                   jax.ShapeDtypeStruct((t, SMALL_W), F32)],
        scratch_shapes=[pltpu.VMEM((tm, d), BF16)],
        compiler_params=_cparams(("arbitrary", "arbitrary")),
        name="in_proj",
    )(x, sc, sh, w_main, w_small)


def _iota2(shape, dim):
    return lax.broadcasted_iota(I32, shape, dim)


def _conv_silu(xp_ref, x_ref, w_ref, bias, c):
    xp_ref[SUBLANES:SUBLANES + c, :] = x_ref[...]
    base = SUBLANES - (CONV_W - 1)
    y = xp_ref[base:base + c, :] * w_ref[0:1, :]
    for j in range(1, CONV_W):
        y = y + xp_ref[base + j:base + j + c, :] * w_ref[j:j + 1, :]
    if bias is not None:
        y = y + bias
    return _silu(y)


def _dn_kernel(qkv_ref, gate_ref, sm_ref, smt_ref, conv0_ref, s0_ref, cw_ref, prow_ref, pcol_ref, nw_ref,
               o_ref, conv_out_ref, s_out_ref, xp_ref, s_ref, y_ref, *, c, valid):
    n = pl.program_id(1)
    base = SUBLANES - (CONV_W - 1)

    @pl.when(n == 0)
    def _():
        xp_ref[base:SUBLANES, :] = conv0_ref[...]
        s_ref[...] = s0_ref[...]

    y = _conv_silu(xp_ref, qkv_ref, cw_ref, None, c)
    tail = xp_ref[SUBLANES + valid - (CONV_W - 1):SUBLANES + valid, :]
    conv_out_ref[...] = tail
    xp_ref[base:SUBLANES, :] = tail

    row_ok = _iota2((c, 1), 0) < valid
    col_ok = _iota2((1, c), 1) < valid
    sm = sm_ref[...]
    g_all = -jnp.exp(prow_ref[0:1, :]) * _softplus(sm + prow_ref[1:2, :])
    g_all = jnp.where(row_ok, g_all, 0.0)
    beta_all = jnp.where(row_ok, _sigmoid(sm), 0.0)
    smt = smt_ref[...]
    g_t = -jnp.exp(pcol_ref[:, 0:1]) * _softplus(smt[0:DN_HEADS, :] + pcol_ref[:, 1:2])
    g_t = jnp.where(col_ok, g_t, 0.0)
    row = _iota2((c, c), 0)
    col = _iota2((c, c), 1)
    tri = jnp.where(row >= col, 1.0, 0.0).astype(F32)
    lc_all = _dot3(tri, g_all)
    lc_t = _dot3(g_t, jnp.where(row <= col, 1.0, 0.0).astype(F32))
    causal = row >= col
    strict = row > col

    heads = range(DN_HEADS)
    y_ref[...] = y
    qs, ks, vs, lcs, betas, decays, mats, qks = [], [], [], [], [], [], [], []
    for h in heads:
        qh = y_ref[:, h * DN_D:(h + 1) * DN_D]
        kh = y_ref[:, DN_KEY + h * DN_D:DN_KEY + (h + 1) * DN_D]
        vh = y_ref[:, 2 * DN_KEY + h * DN_D:2 * DN_KEY + (h + 1) * DN_D]
        qh = qh * lax.rsqrt(jnp.sum(qh * qh, -1, keepdims=True) + NORM_EPS) * (DN_D ** -0.5)
        kh = kh * lax.rsqrt(jnp.sum(kh * kh, -1, keepdims=True) + NORM_EPS)
        qs.append(qh)
        ks.append(jnp.where(row_ok, kh, 0.0))
        vs.append(jnp.where(row_ok, vh, 0.0))
        lcs.append(lc_all[:, SM_A + h:SM_A + h + 1])
        betas.append(beta_all[:, SM_B + h:SM_B + h + 1])
        decays.append(jnp.exp(jnp.where(causal, lcs[h] - lc_t[h:h + 1, :], -jnp.inf)))
    kbs = [ks[h] * betas[h] for h in heads]
    for h in heads:
        mats.append(jnp.where(strict, _bdot(kbs[h], ks[h], _dot_nt) * decays[h], 0.0))
        qks.append(_bdot(qs[h], ks[h], _dot_nt) * decays[h])

    ts = [jnp.where(row == col, 1.0, 0.0).astype(F32)] * DN_HEADS
    s = 1
    while s < c:
        m = ((row // s) % 2 == 1) & ((col // s) % 2 == 0) & (row // (2 * s) == col // (2 * s))
        tsp = [_split(ts[h]) for h in heads]
        ams = [_split(jnp.where(m, mats[h], 0.0)) for h in heads]
        ps = [_dot(tsp[h][0], ams[h][0]) + _dot(tsp[h][0], ams[h][1]) + _dot(tsp[h][1], ams[h][0]) for h in heads]
        psp = [_split(ps[h]) for h in heads]
        ts = [ts[h] - (_dot(psp[h][0], tsp[h][0]) + _dot(psp[h][0], tsp[h][1]) + _dot(psp[h][1], tsp[h][0]))
              for h in heads]
        s *= 2

    e_lcs = [jnp.exp(lcs[h]) for h in heads]
    us = [_dot3(ts[h], vs[h] * betas[h]) for h in heads]
    ws = [_dot3(ts[h], kbs[h] * e_lcs[h]) for h in heads]
    shs = [s_ref[h] for h in heads]
    v_news = [us[h] - _bdot(ws[h], shs[h]) for h in heads]
    os_ = [_bdot(qs[h] * e_lcs[h], shs[h]) + _bdot(qks[h], v_news[h]) for h in heads]
    lasts = [lc_all[c - 1:c, SM_A + h:SM_A + h + 1] for h in heads]
    s_news = [shs[h] * jnp.exp(lasts[h]) + _bdot(ks[h] * jnp.exp(lasts[h] - lcs[h]), v_news[h], _dot_tn)
              for h in heads]
    for h in heads:
        s_ref[h] = s_news[h]
        s_out_ref[h] = s_news[h]
        o = os_[h]
        o = o * lax.rsqrt(jnp.mean(o * o, -1, keepdims=True) + NORM_EPS) * nw_ref[...]
        sl = slice(h * DN_D, (h + 1) * DN_D)
        o_ref[:, sl] = (o * _silu(gate_ref[:, sl])).astype(o_ref.dtype)


def _dn(qkv, gate, sm, smt, conv0, s0, conv_w, prow, pcol, norm_w, *, c, valid, cb_qkv, cb_gate, out_dtype):
    b, lp = qkv.shape[0], qkv.shape[1]
    nch = lp // c
    kern = functools.partial(_dn_kernel, c=c, valid=valid)
    return pl.pallas_call(
        kern,
        grid=(b, nch),
        in_specs=[pl.BlockSpec((None, c, DN_CONV_DIM), lambda i, n: (i, n, cb_qkv)),
                  pl.BlockSpec((None, c, DN_KEY), lambda i, n: (i, n, cb_gate)),
                  pl.BlockSpec((None, c, SMALL_W), lambda i, n: (i, n, 0)),
                  pl.BlockSpec((None, None, 32, c), lambda i, n: (i, n, 0, 0)),
                  pl.BlockSpec((None, CONV_W - 1, DN_CONV_DIM), lambda i, n: (i, 0, 0)),
                  pl.BlockSpec((None, DN_HEADS, DN_D, DN_D), lambda i, n: (i, 0, 0, 0)),
                  pl.BlockSpec((CONV_W, DN_CONV_DIM), lambda i, n: (0, 0)),
                  pl.BlockSpec((2, SMALL_W), lambda i, n: (0, 0)),
                  pl.BlockSpec((DN_HEADS, 2), lambda i, n: (0, 0)),
                  pl.BlockSpec((1, DN_D), lambda i, n: (0, 0))],
        out_specs=[pl.BlockSpec((None, c, DN_KEY), lambda i, n: (i, n, 0)),
                   pl.BlockSpec((None, CONV_W - 1, DN_CONV_DIM), lambda i, n: (i, 0, 0)),
                   pl.BlockSpec((None, DN_HEADS, DN_D, DN_D), lambda i, n: (i, 0, 0, 0))],
        out_shape=[jax.ShapeDtypeStruct((b, lp, DN_KEY), out_dtype),
                   jax.ShapeDtypeStruct((b, CONV_W - 1, DN_CONV_DIM), F32),
                   jax.ShapeDtypeStruct((b, DN_HEADS, DN_D, DN_D), F32)],
        scratch_shapes=[pltpu.VMEM((c + SUBLANES, DN_CONV_DIM), F32),
                        pltpu.VMEM((DN_HEADS, DN_D, DN_D), F32),
                        pltpu.VMEM((c, DN_CONV_DIM), F32)],
        compiler_params=_cparams(("arbitrary", "arbitrary")),
        name="delta_rule",
    )(qkv, gate, sm, smt, conv0, s0, conv_w, prow, pcol, norm_w)


def _ssd_kernel(z_ref, xbc_ref, sm_ref, smt_ref, conv0_ref, h0_ref, cw_ref, cb_ref, prow_ref, pcol_ref, nw_ref,
                o_ref, conv_out_ref, h_out_ref, xp_ref, h_ref, y_ref, *, c, valid):
    n = pl.program_id(1)
    base = SUBLANES - (CONV_W - 1)

    @pl.when(n == 0)
    def _():
        xp_ref[base:SUBLANES, :] = conv0_ref[...]
        h_ref[...] = h0_ref[...]

    act = _conv_silu(xp_ref, xbc_ref, cw_ref, cb_ref[...], c)
    tail = xp_ref[SUBLANES + valid - (CONV_W - 1):SUBLANES + valid, :]
    conv_out_ref[...] = tail
    xp_ref[base:SUBLANES, :] = tail

    row_ok = _iota2((c, 1), 0) < valid
    col_ok = _iota2((1, c), 1) < valid
    dt_all = _softplus(sm_ref[...] + prow_ref[0:1, :])
    la_all = jnp.where(row_ok, dt_all * -jnp.exp(prow_ref[1:2, :]), 0.0)
    dt_t = _softplus(smt_ref[SM_DT:SM_DT + SSM_HEADS, :] + pcol_ref[:, 0:1])
    la_t = jnp.where(col_ok, dt_t * -jnp.exp(pcol_ref[:, 1:2]), 0.0)
    row = _iota2((c, c), 0)
    col = _iota2((c, c), 1)
    lc_all = _dot3(jnp.where(row >= col, 1.0, 0.0).astype(F32), la_all)
    lc_t = _dot3(la_t, jnp.where(row <= col, 1.0, 0.0).astype(F32))
    causal = row >= col
    rep = SSM_HEADS // SSM_GROUPS

    for g in range(SSM_GROUPS):
        bg = act[:, SSM_INNER + g * SSM_N:SSM_INNER + (g + 1) * SSM_N]
        cg = act[:, SSM_INNER + (SSM_GROUPS + g) * SSM_N:SSM_INNER + (SSM_GROUPS + g + 1) * SSM_N]
        bg = jnp.where(row_ok, bg, 0.0)
        cbg = _bdot(cg, bg, _dot_nt)
        for hh in range(rep):
            h = g * rep + hh
            sl = slice(h * SSM_P, (h + 1) * SSM_P)
            xh = jnp.where(row_ok, act[:, sl], 0.0)
            dt_c = dt_all[:, SM_DT + h:SM_DT + h + 1]
            lc_c = lc_all[:, SM_DT + h:SM_DT + h + 1]
            lc_r = lc_t[h:h + 1, :]
            xd = xh * dt_c
            decay = jnp.exp(jnp.where(causal, lc_c - lc_r, -jnp.inf))
            st = h_ref[h]
            y = _bdot(cbg * decay, xd) + _bdot(cg * jnp.exp(lc_c), st, _dot_nt)
            last = lc_all[c - 1:c, SM_DT + h:SM_DT + h + 1]
            st_new = st * jnp.exp(last) + _bdot(xd, bg * jnp.exp(last - lc_c), _dot_tn)
            h_ref[h] = st_new
            h_out_ref[h] = st_new
            y = y + act[:, sl] * prow_ref[2:3, SM_DT + h:SM_DT + h + 1]
            y_ref[:, sl] = y * _silu(z_ref[:, sl])

    gw = SSM_INNER // SSM_GROUPS
    for g in range(SSM_GROUPS):
        yg = y_ref[:, g * gw:(g + 1) * gw]
        yg = yg * lax.rsqrt(jnp.mean(yg * yg, -1, keepdims=True) + NORM_EPS) * nw_ref[:, g * gw:(g + 1) * gw]
        o_ref[:, g * gw:(g + 1) * gw] = yg.astype(o_ref.dtype)


def _ssd(z, xbc, sm, smt, conv0, h0, conv_w, conv_b, prow, pcol, norm_w, *, c, valid, cb_z, cb_xbc, out_dtype):
    b, lp = z.shape[0], z.shape[1]
    nch = lp // c
    kern = functools.partial(_ssd_kernel, c=c, valid=valid)
    return pl.pallas_call(
        kern,
        grid=(b, nch),
        in_specs=[pl.BlockSpec((None, c, SSM_INNER), lambda i, n: (i, n, cb_z)),
                  pl.BlockSpec((None, c, SSM_CONV_DIM), lambda i, n: (i, n, cb_xbc)),
                  pl.BlockSpec((None, c, SMALL_W), lambda i, n: (i, n, 0)),
                  pl.BlockSpec((None, None, 32, c), lambda i, n: (i, n, 0, 0)),
                  pl.BlockSpec((None, CONV_W - 1, SSM_CONV_DIM), lambda i, n: (i, 0, 0)),
                  pl.BlockSpec((None, SSM_HEADS, SSM_P, SSM_N), lambda i, n: (i, 0, 0, 0)),
                  pl.BlockSpec((CONV_W, SSM_CONV_DIM), lambda i, n: (0, 0)),
                  pl.BlockSpec((1, SSM_CONV_DIM), lambda i, n: (0, 0)),
                  pl.BlockSpec((3, SMALL_W), lambda i, n: (0, 0)),
                  pl.BlockSpec((SSM_HEADS, 2), lambda i, n: (0, 0)),
                  pl.BlockSpec((1, SSM_INNER), lambda i, n: (0, 0))],
        out_specs=[pl.BlockSpec((None, c, SSM_INNER), lambda i, n: (i, n, 0)),
                   pl.BlockSpec((None, CONV_W - 1, SSM_CONV_DIM), lambda i, n: (i, 0, 0)),
                   pl.BlockSpec((None, SSM_HEADS, SSM_P, SSM_N), lambda i, n: (i, 0, 0, 0))],
        out_shape=[jax.ShapeDtypeStruct((b, lp, SSM_INNER), out_dtype),
                   jax.ShapeDtypeStruct((b, CONV_W - 1, SSM_CONV_DIM), F32),
                   jax.ShapeDtypeStruct((b, SSM_HEADS, SSM_P, SSM_N), F32)],
        scratch_shapes=[pltpu.VMEM((c + SUBLANES, SSM_CONV_DIM), F32),
                        pltpu.VMEM((SSM_HEADS, SSM_P, SSM_N), F32),
                        pltpu.VMEM((c, SSM_INNER), F32)],
        compiler_params=_cparams(("arbitrary", "arbitrary")),
        name="ssd_scan",
    )(z, xbc, sm, smt, conv0, h0, conv_w, conv_b, prow, pcol, norm_w)


def _rope_tables(pos, heads):
    half = ROPE_DIM // 2
    inv_freq = ROPE_THETA ** (-jnp.arange(half, dtype=F32) / half)
    ang = pos.astype(F32)[:, None] * inv_freq[None, :]
    cos, sin = jnp.cos(ang), jnp.sin(ang)
    n = pos.shape[0]
    pad = jnp.zeros((n, SWA_D - ROPE_DIM), F32)
    c_h = jnp.concatenate([cos, cos, pad + 1.0], -1)
    a_h = jnp.concatenate([-sin, jnp.zeros_like(sin), pad], -1)
    b_h = jnp.concatenate([jnp.zeros_like(sin), sin, pad], -1)
    return tuple(jnp.tile(t, (1, heads)) for t in (c_h, a_h, b_h))


def _rope(x, tc, ta, tb):
    half = ROPE_DIM // 2
    w = x.shape[-1]
    return x * tc + pltpu.roll(x, w - half, 1) * ta + pltpu.roll(x, half, 1) * tb


def _sink_attend(q, k, v, valid, sink):
    s = _bdot(q, k, _dot_nt) * (SWA_D ** -0.5)
    s = jnp.where(valid, s, -jnp.inf)
    m = jnp.maximum(jnp.max(s, -1, keepdims=True), sink)
    p = jnp.exp(s - m)
    den = jnp.sum(p, -1, keepdims=True) + jnp.exp(sink - m)
    return _bdot(p / den, v)


def _swa_prompt_kernel(q_ref, kp_ref, kc_ref, vp_ref, vc_ref, qt_ref, ktp_ref, ktc_ref, sink_ref,
                       o_ref, ko_ref, vo_ref):
    i = pl.program_id(1)
    w = WINDOW
    q = _rope(q_ref[...], qt_ref[0], qt_ref[1], qt_ref[2])
    kc = _rope(kc_ref[...], ktc_ref[0], ktc_ref[1], ktc_ref[2])
    kp = _rope(kp_ref[...], ktp_ref[0], ktp_ref[1], ktp_ref[2])
    vc = vc_ref[...]
    ko_ref[...] = kc
    vo_ref[...] = vc
    kk = jnp.concatenate([kp, kc], 0)
    vv = jnp.concatenate([vp_ref[...], vc], 0)
    qi = _iota2((w, 2 * w), 0) + w
    kj = _iota2((w, 2 * w), 1)
    valid = (kj <= qi) & (qi - kj < w) & ((kj >= w) | (i > 0))
    valid = jnp.concatenate([valid] * SWA_GRP, 0)
    for g in range(SWA_KV):
        kg = kk[:, g * SWA_D:(g + 1) * SWA_D]
        vg = vv[:, g * SWA_D:(g + 1) * SWA_D]
        qg = jnp.concatenate([q[:, (g * SWA_GRP + j) * SWA_D:(g * SWA_GRP + j + 1) * SWA_D]
                              for j in range(SWA_GRP)], 0)
        sk = jnp.concatenate([jnp.broadcast_to(sink_ref[0:1, g * SWA_GRP + j:g * SWA_GRP + j + 1], (w, 1))
                              for j in range(SWA_GRP)], 0)
        og = _sink_attend(qg, kg, vg, valid, sk)
        for j in range(SWA_GRP):
            hd = g * SWA_GRP + j
            o_ref[:, hd * SWA_D:(hd + 1) * SWA_D] = og[j * w:(j + 1) * w].astype(o_ref.dtype)


def _swa_prompt(proj, qt, kt, sinks):
    b, l = proj.shape[0], proj.shape[1]
    w = WINDOW
    nb = l // w
    kvw = SWA_KV * SWA_D
    qw = SWA_HEADS * SWA_D
    prev = lambda i, n: (i, jnp.maximum(n - 1, 0), COL_SWK // kvw)
    prev_v = lambda i, n: (i, jnp.maximum(n - 1, 0), COL_SWV // kvw)
    return pl.pallas_call(
        _swa_prompt_kernel,
        grid=(b, nb),
        in_specs=[pl.BlockSpec((None, w, qw), lambda i, n: (i, n, COL_SWQ // qw)),
                  pl.BlockSpec((None, w, kvw), prev),
                  pl.BlockSpec((None, w, kvw), lambda i, n: (i, n, COL_SWK // kvw)),
                  pl.BlockSpec((None, w, kvw), prev_v),
                  pl.BlockSpec((None, w, kvw), lambda i, n: (i, n, COL_SWV // kvw)),
                  pl.BlockSpec((3, w, qw), lambda i, n: (0, n, 0)),
                  pl.BlockSpec((3, w, kvw), lambda i, n: (0, jnp.maximum(n - 1, 0), 0)),
                  pl.BlockSpec((3, w, kvw), lambda i, n: (0, n, 0)),
                  pl.BlockSpec((1, SWA_HEADS), lambda i, n: (0, 0))],
        out_specs=[pl.BlockSpec((None, w, qw), lambda i, n: (i, n, 0)),
                   pl.BlockSpec((None, w, kvw), lambda i, n: (i, 0, 0)),
                   pl.BlockSpec((None, w, kvw), lambda i, n: (i, 0, 0))],
        out_shape=[jax.ShapeDtypeStruct((b, l, qw), BF16),
                   jax.ShapeDtypeStruct((b, w, kvw), F32),
                   jax.ShapeDtypeStruct((b, w, kvw), F32)],
        compiler_params=_cparams(("arbitrary", "arbitrary")),
        name="swa_prompt",
    )(proj, proj, proj, proj, proj, qt, kt, kt, sinks)


def _swa_sample_kernel(q_ref, k_ref, v_ref, kb_ref, vb_ref, qt_ref, kt_ref, sink_ref,
                       o_ref, ko_ref, vo_ref, kk_ref, vv_ref, *, l):
    w = WINDOW
    lp = SUBLANES
    q = _rope(q_ref[...], qt_ref[0], qt_ref[1], qt_ref[2])
    k = _rope(k_ref[...], kt_ref[0], kt_ref[1], kt_ref[2])
    kk_ref[0:w, :] = kb_ref[...]
    kk_ref[w:w + lp, :] = k
    vv_ref[0:w, :] = vb_ref[...]
    vv_ref[w:w + lp, :] = v_ref[...]
    ko_ref[...] = kk_ref[l:l + w, :]
    vo_ref[...] = vv_ref[l:l + w, :]
    kk = kk_ref[...]
    vv = vv_ref[...]
    qi = _iota2((lp, w + lp), 0) + w
    kj = _iota2((lp, w + lp), 1)
    valid = (kj <= qi) & (qi - kj < w) & (kj < w + l)
    valid = jnp.concatenate([valid] * SWA_GRP, 0)
    for g in range(SWA_KV):
        kg = kk[:, g * SWA_D:(g + 1) * SWA_D]
        vg = vv[:, g * SWA_D:(g + 1) * SWA_D]
        qg = jnp.concatenate([q[:, (g * SWA_GRP + j) * SWA_D:(g * SWA_GRP + j + 1) * SWA_D]
                              for j in range(SWA_GRP)], 0)
        sk = jnp.concatenate([jnp.broadcast_to(sink_ref[0:1, g * SWA_GRP + j:g * SWA_GRP + j + 1], (lp, 1))
                              for j in range(SWA_GRP)], 0)
        og = _sink_attend(qg, kg, vg, valid, sk)
        for j in range(SWA_GRP):
            hd = g * SWA_GRP + j
            o_ref[:, hd * SWA_D:(hd + 1) * SWA_D] = og[j * lp:(j + 1) * lp]


def _swa_sample(q, k, v, kbuf, vbuf, qt, kt, sinks, l):
    b = q.shape[0]
    w = WINDOW
    lp = SUBLANES
    kvw = SWA_KV * SWA_D
    qw = SWA_HEADS * SWA_D
    kern = functools.partial(_swa_sample_kernel, l=l)
    return pl.pallas_call(
        kern,
        grid=(b,),
        in_specs=[pl.BlockSpec((None, lp, qw), lambda i: (i, 0, 0)),
                  pl.BlockSpec((None, lp, kvw), lambda i: (i, 0, 0)),
                  pl.BlockSpec((None, lp, kvw), lambda i: (i, 0, 0)),
                  pl.BlockSpec((None, w, kvw), lambda i: (i, 0, 0)),
                  pl.BlockSpec((None, w, kvw), lambda i: (i, 0, 0)),
                  pl.BlockSpec((3, lp, qw), lambda i: (0, 0, 0)),
                  pl.BlockSpec((3, lp, kvw), lambda i: (0, 0, 0)),
                  pl.BlockSpec((1, SWA_HEADS), lambda i: (0, 0))],
        out_specs=[pl.BlockSpec((None, lp, qw), lambda i: (i, 0, 0)),
                   pl.BlockSpec((None, w, kvw), lambda i: (i, 0, 0)),
                   pl.BlockSpec((None, w, kvw), lambda i: (i, 0, 0))],
        out_shape=[jax.ShapeDtypeStruct((b, lp, qw), F32),
                   jax.ShapeDtypeStruct((b, w, kvw), F32),
                   jax.ShapeDtypeStruct((b, w, kvw), F32)],
        scratch_shapes=[pltpu.VMEM((w + lp, kvw), F32), pltpu.VMEM((w + lp, kvw), F32)],
        compiler_params=_cparams(("arbitrary",)),
        name="swa_sample",
    )(q, k, v, kbuf, vbuf, qt, kt, sinks)


def _merge_kernel(oa_ref, ob_ref, oc_ref, ga_ref, gb_ref, gc_ref, wa_ref, wb_ref, wc_ref, o_ref):
    acc = _sigmoid(ga_ref[...]) * _bdot(oa_ref[...], wa_ref[...])
    acc = acc + _sigmoid(gb_ref[...]) * _bdot(ob_ref[...], wb_ref[...])
    acc = acc + _sigmoid(gc_ref[...]) * _bdot(oc_ref[...], wc_ref[...])
    o_ref[...] = acc.astype(o_ref.dtype)


def _merge(oa, ob, oc, proj, wbr, tm):
    t = oa.shape[0]
    tn = 512
    nj = D_MODEL // tn
    gate = lambda k: pl.BlockSpec((tm, tn), lambda i, j: (i, COL_MG // tn + k * nj + j))
    wsp = lambda k: pl.BlockSpec((None, BRANCH_W, tn), lambda i, j: (k, 0, j))
    osp = pl.BlockSpec((tm, BRANCH_W), lambda i, j: (i, 0))
    return pl.pallas_call(
        _merge_kernel,
        grid=(t // tm, nj),
        in_specs=[osp, osp, osp, gate(0), gate(1), gate(2), wsp(0), wsp(1), wsp(2)],
        out_specs=pl.BlockSpec((tm, tn), lambda i, j: (i, j)),
        out_shape=jax.ShapeDtypeStruct((t, D_MODEL), BF16),
        compiler_params=_cparams(("arbitrary", "arbitrary")),
        name="merge",
    )(oa, ob, oc, proj, proj, proj, wbr, wbr, wbr)


def _layer_norm(r, g, b):
    mu = jnp.mean(r, -1, keepdims=True)
    rc = r - mu
    var = jnp.mean(rc * rc, -1, keepdims=True)
    return rc * lax.rsqrt(var + LN_EPS) * g + b


def _to_slab(ref, val):
    rows = val.shape[0]
    for s in range(SLAB):
        ref[pl.ds(s, rows, stride=SLAB), :] = val[:, s * LANES:(s + 1) * LANES]


def _from_slab(ref, lead, start, rows, dtype):
    parts = []
    for s in range(SLAB):
        idx = (pl.ds(start * SLAB + s, rows, stride=SLAB), slice(None))
        parts.append(ref[(lead,) + idx if lead is not None else idx].astype(dtype))
    return jnp.concatenate(parts, axis=1)


def _out_ln_kernel(m_ref, w_ref, x_ref, g1_ref, sc_ref, sh_ref, lg_ref, lb_ref, x1_ref, h2_ref, hs_ref, *, alpha):
    mix = _dot(m_ref[...], w_ref[...])
    x1 = _layer_norm(alpha * x_ref[...] + g1_ref[...] * mix, lg_ref[...], lb_ref[...])
    x1_ref[...] = x1
    h2 = x1 * (1.0 + sc_ref[...]) + sh_ref[...]
    h2_ref[...] = h2
    _to_slab(hs_ref, h2)


def _out_ln(merged, w_out, x, g1, sc2, sh2, ln_g, ln_b, tm, tiles_per_group, alpha):
    t, d = x.shape
    row = pl.BlockSpec((tm, d), lambda i: (i, 0))
    vec = pl.BlockSpec((1, d), lambda i: (0, 0))
    kern = functools.partial(_out_ln_kernel, alpha=alpha)
    return pl.pallas_call(
        kern,
        grid=(t // tm,),
        in_specs=[row, pl.BlockSpec((d, d), lambda i: (0, 0)), row,
                  _mod_spec(g1, tm, tiles_per_group), _mod_spec(sc2, tm, tiles_per_group),
                  _mod_spec(sh2, tm, tiles_per_group), vec, vec],
        out_specs=[row, row, pl.BlockSpec((tm * SLAB, LANES), lambda i: (i, 0))],
        out_shape=[jax.ShapeDtypeStruct((t, d), F32), jax.ShapeDtypeStruct((t, d), F32),
                   jax.ShapeDtypeStruct((t * SLAB, LANES), F32)],
        compiler_params=_cparams(("arbitrary",)),
        name="out_ln",
    )(merged, w_out, x, g1, sc2, sh2, ln_g, ln_b)


ROUTER_TR = 640


def _router_kernel(h_ref, w_ref, b_ref, eidx_ref, rank_ref, wgt_ref, cnt_ref, carry_ref):
    i = pl.program_id(0)
    tr = h_ref.shape[0]
    ng, gs = N_GROUPS, GROUP_SIZE

    @pl.when(i == 0)
    def _():
        carry_ref[...] = jnp.zeros_like(carry_ref)

    logits = _dot3(w_ref[...], h_ref[...], _dot_nt)
    scores = _sigmoid(logits)
    sc3 = scores.reshape(ng, gs, tr)
    ch3 = (scores + b_ref[...]).reshape(ng, gs, tr)
    io_e = _iota2((ng, gs, tr), 1)
    io_g = _iota2((ng, 1, tr), 0)
    io_x = _iota2((ng, gs, tr), 0) * gs + io_e
    ninf = -jnp.inf

    m1 = jnp.max(ch3, 1, keepdims=True)
    i1 = jnp.min(jnp.where(ch3 == m1, io_e, gs), 1, keepdims=True)
    m2 = jnp.max(jnp.where(io_e == i1, ninf, ch3), 1, keepdims=True)
    grp = m1 + m2
    keep = jnp.zeros((ng, 1, tr), jnp.bool_)
    for _ in range(TOPK_GROUPS):
        m = jnp.max(grp, 0, keepdims=True)
        first = jnp.min(jnp.where(grp == m, io_g, ng), 0, keepdims=True)
        hit = io_g == first
        keep = keep | hit
        grp = jnp.where(hit, ninf, grp)

    cm = jnp.where(keep, ch3, ninf)
    hits = []
    firsts = []
    for _ in range(TOP_K):
        m = jnp.max(jnp.max(cm, 1, keepdims=True), 0, keepdims=True)
        cand = jnp.where(cm == m, io_x, N_EXPERTS)
        first = jnp.min(jnp.min(cand, 1, keepdims=True), 0, keepdims=True)
        hit = io_x == first
        hits.append(hit)
        firsts.append(first)
        cm = jnp.where(hit, ninf, cm)
    sel = hits[0]
    for hit in hits[1:]:
        sel = sel | hit
    self32 = jnp.where(sel, 1.0, 0.0).astype(F32)
    wsel = sc3 * self32
    den = jnp.sum(jnp.sum(wsel, 1, keepdims=True), 0, keepdims=True)
    comb = wsel / den * ROUTED_SCALE

    sel2 = self32.reshape(N_EXPERTS, tr)
    upper = jnp.where(_iota2((tr, tr), 0) <= _iota2((tr, tr), 1), 1.0, 0.0).astype(BF16)
    incl = _dot(sel2.astype(BF16), upper)
    carry = carry_ref[:, 0:1]
    rank3 = (carry + incl - sel2).reshape(ng, gs, tr)
    for r in range(TOP_K):
        hf = jnp.where(hits[r], 1.0, 0.0).astype(F32)
        rk = jnp.sum(jnp.sum(hf * rank3, 1, keepdims=True), 0, keepdims=True)
        wg = jnp.sum(jnp.sum(hf * comb, 1, keepdims=True), 0, keepdims=True)
        eidx_ref[r:r + 1, :] = firsts[r].reshape(1, tr)
        rank_ref[r:r + 1, :] = rk.reshape(1, tr).astype(I32)
        wgt_ref[r:r + 1, :] = wg.reshape(1, tr)
    new_carry = carry + incl[:, tr - 1:tr]
    carry_ref[...] = jnp.broadcast_to(new_carry, carry_ref.shape)
    cnt_ref[...] = jnp.broadcast_to(new_carry, cnt_ref.shape).astype(I32)


def _router(h2, wr_t, bias_col):
    t, d = h2.shape
    tr = ROUTER_TR
    out = pl.BlockSpec((TOP_K, tr), lambda i: (0, i))
    return pl.pallas_call(
        _router_kernel,
        grid=(t // tr,),
        in_specs=[pl.BlockSpec((tr, d), lambda i: (i, 0)),
                  pl.BlockSpec((N_EXPERTS, d), lambda i: (0, 0)),
                  pl.BlockSpec((N_EXPERTS, 1), lambda i: (0, 0))],
        out_specs=[out, out, out, pl.BlockSpec((N_EXPERTS, LANES), lambda i: (0, 0))],
        out_shape=[jax.ShapeDtypeStruct((TOP_K, t), I32), jax.ShapeDtypeStruct((TOP_K, t), I32),
                   jax.ShapeDtypeStruct((TOP_K, t), F32), jax.ShapeDtypeStruct((N_EXPERTS, LANES), I32)],
        scratch_shapes=[pltpu.VMEM((N_EXPERTS, LANES), F32)],
        compiler_params=_cparams(("arbitrary",)),
        name="router",
    )(h2, wr_t, bias_col)


def _invert_kernel(slot_ref, tok_ref):
    def init(r, carry):
        for u in range(LANES):
            tok_ref[r, u] = 0
        return carry
    lax.fori_loop(0, tok_ref.shape[0], init, 0)

    def body(t, carry):
        for k in range(TOP_K):
            s = slot_ref[k, t]
            tok_ref[lax.shift_right_logical(s, LANE_BITS), s & (LANES - 1)] = t
        return carry
    lax.fori_loop(0, slot_ref.shape[1], body, 0)


def _invert(slots, n_rows):
    return pl.pallas_call(
        _invert_kernel,
        grid_spec=pltpu.PrefetchScalarGridSpec(
            num_scalar_prefetch=1,
            grid=(1,),
            in_specs=[],
            out_specs=pl.BlockSpec(memory_space=pltpu.SMEM)),
        out_shape=jax.ShapeDtypeStruct((n_rows // LANES, LANES), I32),
        compiler_params=_cparams(("arbitrary",)),
        name="invert_slots",
    )(slots)


def _cast_weights(te_ref, wg_ref, wu_ref, wd_ref, wgb_ref, wub_ref, wdb_ref):
    i = pl.program_id(0)
    prev = te_ref[jnp.maximum(i - 1, 0)]

    @pl.when((i == 0) | (te_ref[i] != prev))
    def _():
        wgb_ref[...] = wg_ref[...].astype(BF16)
        wub_ref[...] = wu_ref[...].astype(BF16)
        wdb_ref[...] = wd_ref[...].astype(BF16)


def _swiglu(xb, wgb_ref, wub_ref, wdb_ref):
    a = _dot(xb, wgb_ref[...])
    u = _dot(xb, wub_ref[...])
    return _dot((_silu(a) * u).astype(BF16), wdb_ref[...])


def _ffn_kernel(te_ref, na_ref, x_ref, wg_ref, wu_ref, wd_ref, y_ref, wgb_ref, wub_ref, wdb_ref):
    _cast_weights(te_ref, wg_ref, wu_ref, wd_ref, wgb_ref, wub_ref, wdb_ref)
    y_ref[...] = _swiglu(x_ref[...].astype(BF16), wgb_ref, wub_ref, wdb_ref)


def _slab_copy(src_ref, row, dst_ref, buf, slot, sem):
    src = src_ref.at[pl.ds(pl.multiple_of(row * SLAB, SLAB), SLAB)]
    dst = dst_ref.at[buf, pl.ds(pl.multiple_of(slot * SLAB, SLAB), SLAB)]
    return pltpu.make_async_copy(src, dst, sem.at[buf])


def _gffn_kernel(te_ref, na_ref, tok_ref, h_ref, wg_ref, wu_ref, wd_ref, y_ref,
                 wgb_ref, wub_ref, wdb_ref, xbuf_ref, sem, *, tm):
    i = pl.program_id(0)
    na = na_ref[0]
    unroll = SUBLANES
    tok_rows = tm // LANES

    def issue(tile, b):
        for q in range(tok_rows):
            def body(g, carry):
                for u in range(unroll):
                    c = g * unroll + u
                    _slab_copy(h_ref, tok_ref[tile * tok_rows + q, c], xbuf_ref, b, q * LANES + c, sem).start()
                return carry
            lax.fori_loop(0, LANES // unroll, body, 0)

    @pl.when(i == 0)
    def _():
        issue(0, 0)

    @pl.when(i + 1 < na)
    def _():
        issue(i + 1, (i + 1) % 2)

    _cast_weights(te_ref, wg_ref, wu_ref, wd_ref, wgb_ref, wub_ref, wdb_ref)

    @pl.when(i < na)
    def _():
        b = i % 2

        def wbody(j, carry):
            for u in range(unroll):
                _slab_copy(h_ref, 0, xbuf_ref, b, 0, sem).wait()
            return carry
        lax.fori_loop(0, tm // unroll, wbody, 0)
        xb = _from_slab(xbuf_ref, b, 0, tm, BF16)
        _to_slab(y_ref, _swiglu(xb, wgb_ref, wub_ref, wdb_ref))

    @pl.when(i >= na)
    def _():
        y_ref[...] = jnp.zeros_like(y_ref)


def _wspecs(layer, d, ff):
    wmap = lambda i, te, *_: (layer, te[i], 0, 0)
    return [pl.BlockSpec((None, None, d, ff), wmap), pl.BlockSpec((None, None, d, ff), wmap),
            pl.BlockSpec((None, None, ff, d), wmap)]


def _wscratch(d, ff):
    return [pltpu.VMEM((d, ff), BF16), pltpu.VMEM((d, ff), BF16), pltpu.VMEM((ff, d), BF16)]


def _ffn(te, na, x, wg, wu, wd, layer, tm):
    t, d = x.shape
    ff = wg.shape[-1]
    return pl.pallas_call(
        _ffn_kernel,
        grid_spec=pltpu.PrefetchScalarGridSpec(
            num_scalar_prefetch=2,
            grid=(t // tm,),
            in_specs=[pl.BlockSpec((tm, d), lambda i, *_: (i, 0))] + _wspecs(layer, d, ff),
            out_specs=pl.BlockSpec((tm, d), lambda i, *_: (i, 0)),
            scratch_shapes=_wscratch(d, ff)),
        out_shape=jax.ShapeDtypeStruct((t, d), F32),
        compiler_params=_cparams(("arbitrary",)),
        name="shared_ffn",
    )(te, na, x, wg, wu, wd)


def _gffn(te, na, tok, h2s, wg, wu, wd, layer, tm):
    d = D_MODEL
    ff = wg.shape[-1]
    n_rows = tok.shape[0] * tok.shape[1]
    kern = functools.partial(_gffn_kernel, tm=tm)
    return pl.pallas_call(
        kern,
        grid_spec=pltpu.PrefetchScalarGridSpec(
            num_scalar_prefetch=3,
            grid=(n_rows // tm,),
            in_specs=[pl.BlockSpec(memory_space=pl.ANY)] + _wspecs(layer, d, ff),
            out_specs=pl.BlockSpec((tm * SLAB, LANES), lambda i, *_: (i, 0)),
            scratch_shapes=_wscratch(d, ff) + [pltpu.VMEM((2, tm * SLAB, LANES), F32),
                                               pltpu.SemaphoreType.DMA((2,))]),
        out_shape=jax.ShapeDtypeStruct((n_rows * SLAB, LANES), F32),
        compiler_params=_cparams(("arbitrary",)),
        name="expert_ffn",
    )(te, na, tok, h2s, wg, wu, wd)


COMBINE_TC = 32


def _combine_kernel(slot_ref, ys_ref, w_ref, sh_ref, x_ref, g2_ref, lg_ref, lb_ref, o_ref, buf_ref, g2x_ref, sem,
                    *, alpha):
    i = pl.program_id(0)
    n = pl.num_programs(0)
    tc = COMBINE_TC

    def issue(tile, b):
        def body(j, carry):
            for k in range(TOP_K):
                _slab_copy(ys_ref, slot_ref[k, tile * tc + j], buf_ref, b, k * tc + j, sem).start()
            return carry
        lax.fori_loop(0, tc, body, 0)

    @pl.when(i == 0)
    def _():
        issue(0, 0)

    @pl.when(i + 1 < n)
    def _():
        issue(i + 1, (i + 1) % 2)

    b = i % 2

    def wbody(j, carry):
        for k in range(TOP_K):
            _slab_copy(ys_ref, 0, buf_ref, b, 0, sem).wait()
        return carry
    lax.fori_loop(0, tc, wbody, 0)

    for s in range(SLAB):
        for u in range(MOD_GROUP):
            g2x_ref[s, pl.ds(u, tc // MOD_GROUP, stride=MOD_GROUP), :] = g2_ref[:, s * LANES:(s + 1) * LANES]
    g2 = jnp.concatenate([g2x_ref[s] for s in range(SLAB)], axis=1)

    w = w_ref[...]
    acc = sh_ref[...]
    for k in range(TOP_K):
        acc = acc + w[:, k:k + 1] * _from_slab(buf_ref, b, k * tc, tc, F32)
    o_ref[...] = _layer_norm(alpha * x_ref[...] + g2 * acc, lg_ref[...], lb_ref[...])


def _combine(slots, ys, wgt, shared, x1, g2grp, ln_g, ln_b, alpha):
    t, d = x1.shape
    tc = COMBINE_TC
    row = pl.BlockSpec((tc, d), lambda i, *_: (i, 0))
    vec = pl.BlockSpec((1, d), lambda i, *_: (0, 0))
    kern = functools.partial(_combine_kernel, alpha=alpha)
    return pl.pallas_call(
        kern,
        grid_spec=pltpu.PrefetchScalarGridSpec(
            num_scalar_prefetch=1,
            grid=(t // tc,),
            in_specs=[pl.BlockSpec(memory_space=pl.ANY),
                      pl.BlockSpec((tc, TOP_K), lambda i, *_: (i, 0)),
                      row, row, pl.BlockSpec((tc // MOD_GROUP, d), lambda i, *_: (i, 0)), vec, vec],
            out_specs=row,
            scratch_shapes=[pltpu.VMEM((2, TOP_K * tc * SLAB, LANES), F32), pltpu.VMEM((SLAB, tc, LANES), F32),
                            pltpu.SemaphoreType.DMA((2,))]),
        out_shape=jax.ShapeDtypeStruct((t, d), F32),
        compiler_params=_cparams(("arbitrary",)),
        name="combine_ln",
    )(slots, ys, wgt, shared, x1, g2grp, ln_g, ln_b)


def _moe(h2, h2s, x1, g2grp, p, layer, alpha):
    t, d = h2.shape
    tm = MOE_TM
    eidx, rank, wgt, cnt = _router(h2, p['w_router'].T, p['router_bias'].reshape(N_EXPERTS, 1))
    counts = cnt[:, 0]
    tiles = (counts + tm - 1) // tm
    tile_end = jnp.cumsum(tiles)
    offs = ((tile_end - tiles) * tm).astype(I32)
    n_active = tile_end[-1]
    n_tiles = (t * TOP_K + N_EXPERTS * (tm - 1)) // tm
    ids = jnp.minimum(jnp.arange(n_tiles, dtype=I32), n_active - 1)
    te = jnp.sum((tile_end[None, :] <= ids[:, None]).astype(I32), axis=1)
    onehot = eidx[..., None] == jnp.arange(N_EXPERTS, dtype=I32)
    slots = rank + jnp.sum(jnp.where(onehot, offs, 0), -1)
    tok = _invert(slots, n_tiles * tm)
    ys = _gffn(te, n_active.reshape(1).astype(I32), tok, h2s, p['w_exp_gate'], p['w_exp_up'], p['w_exp_down'],
               layer, tm)
    tsh = 640
    nsh = t // tsh
    sh4 = lambda w: w.reshape((w.shape[0], 1) + w.shape[1:])
    shared = _ffn(jnp.zeros((nsh,), I32), jnp.full((1,), nsh, I32), h2,
                  sh4(p['w_sh_gate']), sh4(p['w_sh_up']), sh4(p['w_sh_down']), layer, tsh)
    return _combine(slots, ys, wgt.T, shared, x1, g2grp, p['ln2_g'].reshape(1, d), p['ln2_b'].reshape(1, d), alpha)


def _chunk_t(sm, c):
    b, l, _ = sm.shape
    return jnp.swapaxes(sm.reshape(b, l // c, c, SMALL_W)[..., :32], -1, -2)


def _pad_rows(a, rows):
    return jnp.pad(a, ((0, 0), (0, rows - a.shape[1]), (0, 0)))


def _mixer_group(x, mod, states, p, tabs, *, prompt):
    b, l, d = x.shape
    t = b * l
    dn_conv0, dn_s0, ssm_conv0, ssm_h0, kv_buf = states
    sh1, sc1, g1, sh2, sc2, g2 = jnp.split(mod, 6, axis=-1)
    if prompt:
        tm = 1024
        tpg = l // tm
        shape = lambda m: m.reshape(b, 1, d)
    else:
        tm = t
        tpg = 1
        shape = lambda m: jnp.repeat(m, l, axis=0).reshape(1, t, d)
    proj, small = _in_proj(x.reshape(t, d), shape(sc1), shape(sh1), p['w_main'], p['w_small'], tm, tpg)
    proj3 = proj.reshape(b, l, MAIN_W)
    small3 = small.reshape(b, l, SMALL_W)
    if prompt:
        c = CHUNK
        dn_in = (proj3, proj3)
        dn_cb = dict(cb_qkv=COL_QKV // DN_CONV_DIM, cb_gate=COL_DNG // DN_KEY)
        ssd_in = (proj3, proj3)
        ssd_cb = dict(cb_z=COL_SSZ // SSM_INNER, cb_xbc=COL_XBC // SSM_CONV_DIM)
        sm_in = small3
        odt = BF16
    else:
        c = SUBLANES
        cut = lambda c0, w: _pad_rows(proj3[:, :, c0:c0 + w], c)
        dn_in = (cut(COL_QKV, DN_CONV_DIM), cut(COL_DNG, DN_KEY))
        dn_cb = dict(cb_qkv=0, cb_gate=0)
        ssd_in = (cut(COL_SSZ, SSM_INNER), cut(COL_XBC, SSM_CONV_DIM))
        ssd_cb = dict(cb_z=0, cb_xbc=0)
        sm_in = _pad_rows(small3, c)
        odt = F32
    valid = c if prompt else l
    smt = _chunk_t(sm_in, c)
    o_a, dn_conv, dn_s = _dn(dn_in[0], dn_in[1], sm_in, smt, dn_conv0, dn_s0, p['dn_conv_w'], p['dn_prow'],
                             p['dn_pcol'], p['dn_norm_w'].reshape(1, DN_D), c=c, valid=valid, out_dtype=odt, **dn_cb)
    o_b, ssm_conv, ssm_h = _ssd(ssd_in[0], ssd_in[1], sm_in, smt, ssm_conv0, ssm_h0, p['ssm_conv_w'],
                                p['ssm_conv_b'].reshape(1, SSM_CONV_DIM), p['ssm_prow'], p['ssm_pcol'],
                                p['ssm_norm_w'].reshape(1, SSM_INNER), c=c, valid=valid, out_dtype=odt, **ssd_cb)
    sinks = p['swa_sinks'].reshape(1, SWA_HEADS)
    if prompt:
        o_c, k_new, v_new = _swa_prompt(proj3, tabs[0], tabs[1], sinks)
    else:
        o_c, k_new, v_new = _swa_sample(cut(COL_SWQ, SWA_HEADS * SWA_D), cut(COL_SWK, SWA_KV * SWA_D),
                                        cut(COL_SWV, SWA_KV * SWA_D),
                                        kv_buf[0].reshape(b, WINDOW, SWA_KV * SWA_D),
                                        kv_buf[1].reshape(b, WINDOW, SWA_KV * SWA_D), tabs[0], tabs[1], sinks, l)
    flat = lambda o: o[:, :l].reshape(t, BRANCH_W).astype(BF16)
    merged = _merge(flat(o_a), flat(o_b), flat(o_c), proj, p['w_branch'], tm=min(tm, 512))
    tm2 = min(tm, 512)
    tpg2 = l // tm2 if prompt else 1
    shape2 = (lambda m: m.reshape(b, 1, d)) if prompt else shape
    x1, h2, h2s = _out_ln(merged, p['w_out'], x.reshape(t, d), shape2(g1), shape2(sc2), shape2(sh2),
                          p['ln1_g'].reshape(1, d), p['ln1_b'].reshape(1, d), tm2, tpg2, p['alpha'])
    g2grp = jnp.repeat(g2, l // MOD_GROUP, axis=0)
    k_new = k_new.reshape(b, WINDOW, SWA_KV, SWA_D)
    v_new = v_new.reshape(b, WINDOW, SWA_KV, SWA_D)
    return x1, h2, h2s, g2grp, (dn_conv, dn_s, ssm_conv, ssm_h, k_new, v_new)


def _forward(x_prompt, x_sample, state_dn_conv, state_dn, state_ssm_conv, state_ssm, cache_swa_k, cache_swa_v,
             c_prompt, c_sample, w_ada, b_ada, w_in, dn_conv_w, dn_a_log, dn_dt_bias, dn_norm_w,
             ssm_conv_w, ssm_conv_b, ssm_a_log, ssm_dt_bias, ssm_d, ssm_norm_w, swa_sinks, w_branch, w_out,
             ln1_g, ln1_b, w_router, router_bias, w_exp_gate, w_exp_up, w_exp_down, w_sh_gate, w_sh_up,
             w_sh_down, ln2_g, ln2_b):
    depth = w_in.shape[0]
    bp, lp, d = x_prompt.shape
    bs, ls, _ = x_sample.shape
    tp, ts = bp * lp, bs * ls
    alpha = (2 * depth) ** 0.25
    past_len = 16384

    c_all = jnp.concatenate([c_prompt, c_sample, jnp.zeros((4, d), F32)], 0)
    mod_all = _ada(c_all, w_ada, b_ada)
    tabs_p = tuple(jnp.stack(_rope_tables(jnp.arange(lp), h)) for h in (SWA_HEADS, SWA_KV))
    tabs_s = tuple(jnp.stack(_rope_tables(past_len + jnp.arange(SUBLANES), h)) for h in (SWA_HEADS, SWA_KV))
    init_p = (jnp.zeros((bp, CONV_W - 1, DN_CONV_DIM), F32), jnp.zeros((bp, DN_HEADS, DN_D, DN_D), F32),
              jnp.zeros((bp, CONV_W - 1, SSM_CONV_DIM), F32), jnp.zeros((bp, SSM_HEADS, SSM_P, SSM_N), F32), None)

    def pad_lanes(v, at):
        return jnp.zeros((SMALL_W,), F32).at[at:at + v.shape[0]].set(v)

    yp, ys = x_prompt, x_sample
    new_p, new_s = [], []
    for l in range(depth):
        w = w_in[l]
        seg = lambda a, n: w[:, a:a + n]
        w_main = jnp.concatenate([seg(8224, 6144), seg(0, 3072), seg(5136, 1536), seg(7712, 256), seg(7968, 256),
                                  seg(3072, 1024), seg(4112, 1024), seg(6688, 1024)], 1).astype(BF16)
        w_small = jnp.concatenate([seg(4096, 16), seg(6672, 16), jnp.zeros((d, SMALL_W - 32), F32)], 1).astype(BF16)
        p = {'w_main': w_main, 'w_small': w_small, 'alpha': alpha,
             'dn_conv_w': dn_conv_w[l], 'dn_norm_w': dn_norm_w[l],
             'dn_prow': jnp.stack([pad_lanes(dn_a_log[l], SM_A), pad_lanes(dn_dt_bias[l], SM_A)]),
             'dn_pcol': jnp.stack([dn_a_log[l], dn_dt_bias[l]], 1),
             'ssm_conv_w': ssm_conv_w[l], 'ssm_conv_b': ssm_conv_b[l], 'ssm_norm_w': ssm_norm_w[l],
             'ssm_prow': jnp.stack([pad_lanes(ssm_dt_bias[l], SM_DT), pad_lanes(ssm_a_log[l], SM_DT),
                                    pad_lanes(ssm_d[l], SM_DT)]),
             'ssm_pcol': jnp.stack([ssm_dt_bias[l], ssm_a_log[l]], 1),
             'swa_sinks': swa_sinks[l], 'w_branch': w_branch[l].astype(BF16), 'w_out': w_out[l].astype(BF16),
             'ln1_g': ln1_g[l], 'ln1_b': ln1_b[l], 'w_router': w_router[l], 'router_bias': router_bias[l],
             'w_exp_gate': w_exp_gate, 'w_exp_up': w_exp_up, 'w_exp_down': w_exp_down,
             'w_sh_gate': w_sh_gate, 'w_sh_up': w_sh_up, 'w_sh_down': w_sh_down,
             'ln2_g': ln2_g[l], 'ln2_b': ln2_b[l]}
        mod = mod_all[l]
        x1p, h2p, hsp, g2p, st_p = _mixer_group(yp, mod[:bp], init_p, p, tabs_p, prompt=True)
        st_in = (state_dn_conv[l], state_dn[l], state_ssm_conv[l], state_ssm[l], (cache_swa_k[l], cache_swa_v[l]))
        x1s, h2s, hss, g2s, st_s = _mixer_group(ys, mod[bp:bp + bs], st_in, p, tabs_s, prompt=False)
        cat = lambda a, b_: jnp.concatenate([a, b_], 0)
        x2 = _moe(cat(h2p, h2s), cat(hsp, hss), cat(x1p, x1s), cat(g2p, g2s), p, l, alpha)
        yp = x2[:tp].reshape(bp, lp, d)
        ys = x2[tp:].reshape(bs, ls, d)
        new_p.append(st_p)
        new_s.append(st_s)
    outs = [yp, ys]
    for k in range(6):
        outs.append(jnp.stack([s[k] for s in new_p]))
        outs.append(jnp.stack([s[k] for s in new_s]))
    return tuple(outs)


def kernel(x_prompt, x_sample, state_dn_conv, state_dn, state_ssm_conv, state_ssm, cache_swa_k, cache_swa_v, c_prompt, c_sample, w_ada, b_ada, w_in, dn_conv_w, dn_a_log, dn_dt_bias, dn_norm_w, ssm_conv_w, ssm_conv_b, ssm_a_log, ssm_dt_bias, ssm_d, ssm_norm_w, swa_sinks, w_branch, w_out, ln1_g, ln1_b, w_router, router_bias, w_exp_gate, w_exp_up, w_exp_down, w_sh_gate, w_sh_up, w_sh_down, ln2_g, ln2_b):
    return _forward(x_prompt, x_sample, state_dn_conv, state_dn, state_ssm_conv, state_ssm, cache_swa_k, cache_swa_v, c_prompt, c_sample, w_ada, b_ada, w_in, dn_conv_w, dn_a_log, dn_dt_bias, dn_norm_w, ssm_conv_w, ssm_conv_b, ssm_a_log, ssm_dt_bias, ssm_d, ssm_norm_w, swa_sinks, w_branch, w_out, ln1_g, ln1_b, w_router, router_bias, w_exp_gate, w_exp_up, w_exp_down, w_sh_gate, w_sh_up, w_sh_down, ln2_g, ln2_b)
```

```python
import functools
import math

import jax
import jax.numpy as jnp
from jax import lax
from jax.experimental import pallas as pl
from jax.experimental.pallas import tpu as pltpu

F32 = jnp.float32
BF16 = jnp.bfloat16
I32 = jnp.int32

D_MODEL = 2048
CONV_W = 4
DN_HEADS = 8
DN_D = 128
DN_KEY = DN_HEADS * DN_D
DN_CONV_DIM = 3 * DN_KEY
SSM_HEADS = 16
SSM_P = 64
SSM_INNER = SSM_HEADS * SSM_P
SSM_GROUPS = 2
SSM_N = 128
SSM_CONV_DIM = SSM_INNER + 2 * SSM_GROUPS * SSM_N
SWA_HEADS = 16
SWA_KV = 4
SWA_D = 64
SWA_GRP = SWA_HEADS // SWA_KV
WINDOW = 128
ROPE_DIM = SWA_D // 4
ROPE_THETA = 500000.0
N_BRANCH = 3
BRANCH_W = 1024
N_EXPERTS = 64
EXPERT_FF = 512
TOP_K = 8
N_GROUPS = 8
GROUP_SIZE = N_EXPERTS // N_GROUPS
TOPK_GROUPS = 4
ROUTED_SCALE = 2.5
LN_EPS = 1e-5
NORM_EPS = 1e-6
CHUNK = 64

LANES = 128
LANE_BITS = 7
SUBLANES = 8
VMEM_LIMIT = 56 * 1024 * 1024

MAIN_W = 14336
COL_MG, COL_QKV, COL_XBC, COL_SWK, COL_SWV, COL_DNG, COL_SSZ, COL_SWQ = (
    0, 6144, 9216, 10752, 11008, 11264, 12288, 13312)
SMALL_W = LANES
SM_A, SM_B, SM_DT = 0, 8, 16

MOE_TM = 256
SLAB = D_MODEL // LANES
MOD_GROUP = 4


def _cparams(sem, vmem=VMEM_LIMIT):
    return pltpu.CompilerParams(dimension_semantics=sem, vmem_limit_bytes=vmem)


def _dot(a, b):
    return jnp.dot(a, b, preferred_element_type=F32)


def _dot_nt(a, b):
    return lax.dot_general(a, b, (((1,), (1,)), ((), ())), preferred_element_type=F32)


def _dot_tn(a, b):
    return lax.dot_general(a, b, (((0,), (0,)), ((), ())), preferred_element_type=F32)


def _split(a):
    hi = a.astype(BF16)
    lo = (a - hi.astype(F32)).astype(BF16)
    return hi, lo


def _dot3(a, b, dot=_dot):
    ah, al = _split(a)
    bh, bl = _split(b)
    return dot(ah, bh) + dot(ah, bl) + dot(al, bh)


def _bdot(a, b, dot=_dot):
    return dot(a.astype(BF16), b.astype(BF16))


def _silu(x):
    return x * (1.0 / (1.0 + jnp.exp(-x)))


def _sigmoid(x):
    return 1.0 / (1.0 + jnp.exp(-x))


def _softplus(x):
    return jnp.maximum(x, 0.0) + jnp.log(1.0 + jnp.exp(-jnp.abs(x)))


def _ada_kernel(c_ref, w_ref, b_ref, o_ref):
    c = c_ref[...]
    o_ref[...] = _bdot(_silu(c), w_ref[...]) + b_ref[...]


def _ada(c_all, w_ada, b_ada):
    depth, d, n = w_ada.shape
    rows = c_all.shape[0]
    tn = 1024
    return pl.pallas_call(
        _ada_kernel,
        grid=(depth, n // tn),
        in_specs=[pl.BlockSpec((rows, d), lambda l, j: (0, 0)),
                  pl.BlockSpec((None, d, tn), lambda l, j: (l, 0, j)),
                  pl.BlockSpec((None, 1, tn), lambda l, j: (l, 0, j))],
        out_specs=pl.BlockSpec((None, rows, tn), lambda l, j: (l, 0, j)),
        out_shape=jax.ShapeDtypeStruct((depth, rows, n), F32),
        compiler_params=_cparams(("arbitrary", "arbitrary")),
        name="ada",
    )(c_all, w_ada, b_ada.reshape(depth, 1, n))


def _mod_spec(mod, tm, tiles_per_group):
    r = mod.shape[1]
    if r == 1:
        return pl.BlockSpec((None, 1, mod.shape[2]), lambda i, *_: (i // tiles_per_group, 0, 0))
    return pl.BlockSpec((None, r, mod.shape[2]), lambda i, *_: (i, 0, 0))


def _in_proj_kernel(x_ref, sc_ref, sh_ref, wm_ref, ws_ref, proj_ref, small_ref, hb_ref):
    @pl.when(pl.program_id(1) == 0)
    def _():
        h = x_ref[...] * (1.0 + sc_ref[...]) + sh_ref[...]
        hb_ref[...] = h.astype(BF16)
        small_ref[...] = _dot(hb_ref[...], ws_ref[...])

    proj_ref[...] = _dot(hb_ref[...], wm_ref[...])


def _in_proj(x, sc, sh, w_main, w_small, tm, tiles_per_group):
    t, d = x.shape
    tn = 1024
    return pl.pallas_call(
        _in_proj_kernel,
        grid=(t // tm, MAIN_W // tn),
        in_specs=[pl.BlockSpec((tm, d), lambda i, j: (i, 0)),
                  _mod_spec(sc, tm, tiles_per_group),
                  _mod_spec(sh, tm, tiles_per_group),
                  pl.BlockSpec((d, tn), lambda i, j: (0, j)),
                  pl.BlockSpec((d, SMALL_W), lambda i, j: (0, 0))],
        out_specs=[pl.BlockSpec((tm, tn), lambda i, j: (i, j)),
                   pl.BlockSpec((tm, SMALL_W), lambda i, j: (i, 0))],
        out_shape=[jax.ShapeDtypeStruct((t, MAIN_W), F32),
                   jax.ShapeDtypeStruct((t, SMALL_W), F32)],
        scratch_shapes=[pltpu.VMEM((tm, d), BF16)],
        compiler_params=_cparams(("arbitrary", "arbitrary")),
        name="in_proj",
    )(x, sc, sh, w_main, w_small)


def _iota2(shape, dim):
    return lax.broadcasted_iota(I32, shape, dim)


def _conv_silu(xp_ref, x_ref, w_ref, bias, c):
    xp_ref[SUBLANES:SUBLANES + c, :] = x_ref[...]
    base = SUBLANES - (CONV_W - 1)
    y = xp_ref[base:base + c, :] * w_ref[0:1, :]
    for j in range(1, CONV_W):
        y = y + xp_ref[base + j:base + j + c, :] * w_ref[j:j + 1, :]
    if bias is not None:
        y = y + bias
    return _silu(y)


def _dn_kernel(qkv_ref, gate_ref, sm_ref, smt_ref, conv0_ref, s0_ref, cw_ref, prow_ref, pcol_ref, nw_ref,
               o_ref, conv_out_ref, s_out_ref, xp_ref, s_ref, y_ref, *, c, valid):
    n = pl.program_id(1)
    base = SUBLANES - (CONV_W - 1)

    @pl.when(n == 0)
    def _():
        xp_ref[base:SUBLANES, :] = conv0_ref[...]
        s_ref[...] = s0_ref[...]

    y = _conv_silu(xp_ref, qkv_ref, cw_ref, None, c)
    tail = xp_ref[SUBLANES + valid - (CONV_W - 1):SUBLANES + valid, :]
    conv_out_ref[...] = tail
    xp_ref[base:SUBLANES, :] = tail

    row_ok = _iota2((c, 1), 0) < valid
    col_ok = _iota2((1, c), 1) < valid
    sm = sm_ref[...]
    g_all = -jnp.exp(prow_ref[0:1, :]) * _softplus(sm + prow_ref[1:2, :])
    g_all = jnp.where(row_ok, g_all, 0.0)
    beta_all = jnp.where(row_ok, _sigmoid(sm), 0.0)
    smt = smt_ref[...]
    g_t = -jnp.exp(pcol_ref[:, 0:1]) * _softplus(smt[0:DN_HEADS, :] + pcol_ref[:, 1:2])
    g_t = jnp.where(col_ok, g_t, 0.0)
    row = _iota2((c, c), 0)
    col = _iota2((c, c), 1)
    tri = jnp.where(row >= col, 1.0, 0.0).astype(F32)
    lc_all = _dot3(tri, g_all)
    lc_t = _dot3(g_t, jnp.where(row <= col, 1.0, 0.0).astype(F32))
    causal = row >= col
    strict = row > col

    heads = range(DN_HEADS)
    y_ref[...] = y
    qs, ks, vs, lcs, betas, decays, mats, qks = [], [], [], [], [], [], [], []
    for h in heads:
        qh = y_ref[:, h * DN_D:(h + 1) * DN_D]
        kh = y_ref[:, DN_KEY + h * DN_D:DN_KEY + (h + 1) * DN_D]
        vh = y_ref[:, 2 * DN_KEY + h * DN_D:2 * DN_KEY + (h + 1) * DN_D]
        qh = qh * lax.rsqrt(jnp.sum(qh * qh, -1, keepdims=True) + NORM_EPS) * (DN_D ** -0.5)
        kh = kh * lax.rsqrt(jnp.sum(kh * kh, -1, keepdims=True) + NORM_EPS)
        qs.append(qh)
        ks.append(jnp.where(row_ok, kh, 0.0))
        vs.append(jnp.where(row_ok, vh, 0.0))
        lcs.append(lc_all[:, SM_A + h:SM_A + h + 1])
        betas.append(beta_all[:, SM_B + h:SM_B + h + 1])
        decays.append(jnp.exp(jnp.where(causal, lcs[h] - lc_t[h:h + 1, :], -jnp.inf)))
    kbs = [ks[h] * betas[h] for h in heads]
    for h in heads:
        mats.append(jnp.where(strict, _bdot(kbs[h], ks[h], _dot_nt) * decays[h], 0.0))
        qks.append(_bdot(qs[h], ks[h], _dot_nt) * decays[h])

    ts = [jnp.where(row == col, 1.0, 0.0).astype(F32)] * DN_HEADS
    s = 1
    while s < c:
        m = ((row // s) % 2 == 1) & ((col // s) % 2 == 0) & (row // (2 * s) == col // (2 * s))
        tsp = [_split(ts[h]) for h in heads]
        ams = [_split(jnp.where(m, mats[h], 0.0)) for h in heads]
        ps = [_dot(tsp[h][0], ams[h][0]) + _dot(tsp[h][0], ams[h][1]) + _dot(tsp[h][1], ams[h][0]) for h in heads]
        psp = [_split(ps[h]) for h in heads]
        ts = [ts[h] - (_dot(psp[h][0], tsp[h][0]) + _dot(psp[h][0], tsp[h][1]) + _dot(psp[h][1], tsp[h][0]))
              for h in heads]
        s *= 2

    e_lcs = [jnp.exp(lcs[h]) for h in heads]
    us = [_dot3(ts[h], vs[h] * betas[h]) for h in heads]
    ws = [_dot3(ts[h], kbs[h] * e_lcs[h]) for h in heads]
    shs = [s_ref[h] for h in heads]
    v_news = [us[h] - _bdot(ws[h], shs[h]) for h in heads]
    os_ = [_bdot(qs[h] * e_lcs[h], shs[h]) + _bdot(qks[h], v_news[h]) for h in heads]
    lasts = [lc_all[c - 1:c, SM_A + h:SM_A + h + 1] for h in heads]
    s_news = [shs[h] * jnp.exp(lasts[h]) + _bdot(ks[h] * jnp.exp(lasts[h] - lcs[h]), v_news[h], _dot_tn)
              for h in heads]
    for h in heads:
        s_ref[h] = s_news[h]
        s_out_ref[h] = s_news[h]
        o = os_[h]
        o = o * lax.rsqrt(jnp.mean(o * o, -1, keepdims=True) + NORM_EPS) * nw_ref[...]
        sl = slice(h * DN_D, (h + 1) * DN_D)
        o_ref[:, sl] = (o * _silu(gate_ref[:, sl])).astype(o_ref.dtype)


def _dn(qkv, gate, sm, smt, conv0, s0, conv_w, prow, pcol, norm_w, *, c, valid, cb_qkv, cb_gate, out_dtype):
    b, lp = qkv.shape[0], qkv.shape[1]
    nch = lp // c
    kern = functools.partial(_dn_kernel, c=c, valid=valid)
    return pl.pallas_call(
        kern,
        grid=(b, nch),
        in_specs=[pl.BlockSpec((None, c, DN_CONV_DIM), lambda i, n: (i, n, cb_qkv)),
                  pl.BlockSpec((None, c, DN_KEY), lambda i, n: (i, n, cb_gate)),
                  pl.BlockSpec((None, c, SMALL_W), lambda i, n: (i, n, 0)),
                  pl.BlockSpec((None, None, 32, c), lambda i, n: (i, n, 0, 0)),
                  pl.BlockSpec((None, CONV_W - 1, DN_CONV_DIM), lambda i, n: (i, 0, 0)),
                  pl.BlockSpec((None, DN_HEADS, DN_D, DN_D), lambda i, n: (i, 0, 0, 0)),
                  pl.BlockSpec((CONV_W, DN_CONV_DIM), lambda i, n: (0, 0)),
                  pl.BlockSpec((2, SMALL_W), lambda i, n: (0, 0)),
                  pl.BlockSpec((DN_HEADS, 2), lambda i, n: (0, 0)),
                  pl.BlockSpec((1, DN_D), lambda i, n: (0, 0))],
        out_specs=[pl.BlockSpec((None, c, DN_KEY), lambda i, n: (i, n, 0)),
                   pl.BlockSpec((None, CONV_W - 1, DN_CONV_DIM), lambda i, n: (i, 0, 0)),
                   pl.BlockSpec((None, DN_HEADS, DN_D, DN_D), lambda i, n: (i, 0, 0, 0))],
        out_shape=[jax.ShapeDtypeStruct((b, lp, DN_KEY), out_dtype),
                   jax.ShapeDtypeStruct((b, CONV_W - 1, DN_CONV_DIM), F32),
                   jax.ShapeDtypeStruct((b, DN_HEADS, DN_D, DN_D), F32)],
        scratch_shapes=[pltpu.VMEM((c + SUBLANES, DN_CONV_DIM), F32),
                        pltpu.VMEM((DN_HEADS, DN_D, DN_D), F32),
                        pltpu.VMEM((c, DN_CONV_DIM), F32)],
        compiler_params=_cparams(("arbitrary", "arbitrary")),
        name="delta_rule",
    )(qkv, gate, sm, smt, conv0, s0, conv_w, prow, pcol, norm_w)


def _ssd_kernel(z_ref, xbc_ref, sm_ref, smt_ref, conv0_ref, h0_ref, cw_ref, cb_ref, prow_ref, pcol_ref, nw_ref,
                o_ref, conv_out_ref, h_out_ref, xp_ref, h_ref, y_ref, *, c, valid):
    n = pl.program_id(1)
    base = SUBLANES - (CONV_W - 1)

    @pl.when(n == 0)
    def _():
        xp_ref[base:SUBLANES, :] = conv0_ref[...]
        h_ref[...] = h0_ref[...]

    act = _conv_silu(xp_ref, xbc_ref, cw_ref, cb_ref[...], c)
    tail = xp_ref[SUBLANES + valid - (CONV_W - 1):SUBLANES + valid, :]
    conv_out_ref[...] = tail
    xp_ref[base:SUBLANES, :] = tail

    row_ok = _iota2((c, 1), 0) < valid
    col_ok = _iota2((1, c), 1) < valid
    dt_all = _softplus(sm_ref[...] + prow_ref[0:1, :])
    la_all = jnp.where(row_ok, dt_all * -jnp.exp(prow_ref[1:2, :]), 0.0)
    dt_t = _softplus(smt_ref[SM_DT:SM_DT + SSM_HEADS, :] + pcol_ref[:, 0:1])
    la_t = jnp.where(col_ok, dt_t * -jnp.exp(pcol_ref[:, 1:2]), 0.0)
    row = _iota2((c, c), 0)
    col = _iota2((c, c), 1)
    lc_all = _dot3(jnp.where(row >= col, 1.0, 0.0).astype(F32), la_all)
    lc_t = _dot3(la_t, jnp.where(row <= col, 1.0, 0.0).astype(F32))
    causal = row >= col
    rep = SSM_HEADS // SSM_GROUPS

    for g in range(SSM_GROUPS):
        bg = act[:, SSM_INNER + g * SSM_N:SSM_INNER + (g + 1) * SSM_N]
        cg = act[:, SSM_INNER + (SSM_GROUPS + g) * SSM_N:SSM_INNER + (SSM_GROUPS + g + 1) * SSM_N]
        bg = jnp.where(row_ok, bg, 0.0)
        cbg = _bdot(cg, bg, _dot_nt)
        for hh in range(rep):
            h = g * rep + hh
            sl = slice(h * SSM_P, (h + 1) * SSM_P)
            xh = jnp.where(row_ok, act[:, sl], 0.0)
            dt_c = dt_all[:, SM_DT + h:SM_DT + h + 1]
            lc_c = lc_all[:, SM_DT + h:SM_DT + h + 1]
            lc_r = lc_t[h:h + 1, :]
            xd = xh * dt_c
            decay = jnp.exp(jnp.where(causal, lc_c - lc_r, -jnp.inf))
            st = h_ref[h]
            y = _bdot(cbg * decay, xd) + _bdot(cg * jnp.exp(lc_c), st, _dot_nt)
            last = lc_all[c - 1:c, SM_DT + h:SM_DT + h + 1]
            st_new = st * jnp.exp(last) + _bdot(xd, bg * jnp.exp(last - lc_c), _dot_tn)
            h_ref[h] = st_new
            h_out_ref[h] = st_new
            y = y + act[:, sl] * prow_ref[2:3, SM_DT + h:SM_DT + h + 1]
            y_ref[:, sl] = y * _silu(z_ref[:, sl])

    gw = SSM_INNER // SSM_GROUPS
    for g in range(SSM_GROUPS):
        yg = y_ref[:, g * gw:(g + 1) * gw]
        yg = yg * lax.rsqrt(jnp.mean(yg * yg, -1, keepdims=True) + NORM_EPS) * nw_ref[:, g * gw:(g + 1) * gw]
        o_ref[:, g * gw:(g + 1) * gw] = yg.astype(o_ref.dtype)


def _ssd(z, xbc, sm, smt, conv0, h0, conv_w, conv_b, prow, pcol, norm_w, *, c, valid, cb_z, cb_xbc, out_dtype):
    b, lp = z.shape[0], z.shape[1]
    nch = lp // c
    kern = functools.partial(_ssd_kernel, c=c, valid=valid)
    return pl.pallas_call(
        kern,
        grid=(b, nch),
        in_specs=[pl.BlockSpec((None, c, SSM_INNER), lambda i, n: (i, n, cb_z)),
                  pl.BlockSpec((None, c, SSM_CONV_DIM), lambda i, n: (i, n, cb_xbc)),
                  pl.BlockSpec((None, c, SMALL_W), lambda i, n: (i, n, 0)),
                  pl.BlockSpec((None, None, 32, c), lambda i, n: (i, n, 0, 0)),
                  pl.BlockSpec((None, CONV_W - 1, SSM_CONV_DIM), lambda i, n: (i, 0, 0)),
                  pl.BlockSpec((None, SSM_HEADS, SSM_P, SSM_N), lambda i, n: (i, 0, 0, 0)),
                  pl.BlockSpec((CONV_W, SSM_CONV_DIM), lambda i, n: (0, 0)),
                  pl.BlockSpec((1, SSM_CONV_DIM), lambda i, n: (0, 0)),
                  pl.BlockSpec((3, SMALL_W), lambda i, n: (0, 0)),
                  pl.BlockSpec((SSM_HEADS, 2), lambda i, n: (0, 0)),
                  pl.BlockSpec((1, SSM_INNER), lambda i, n: (0, 0))],
        out_specs=[pl.BlockSpec((None, c, SSM_INNER), lambda i, n: (i, n, 0)),
                   pl.BlockSpec((None, CONV_W - 1, SSM_CONV_DIM), lambda i, n: (i, 0, 0)),
                   pl.BlockSpec((None, SSM_HEADS, SSM_P, SSM_N), lambda i, n: (i, 0, 0, 0))],
        out_shape=[jax.ShapeDtypeStruct((b, lp, SSM_INNER), out_dtype),
                   jax.ShapeDtypeStruct((b, CONV_W - 1, SSM_CONV_DIM), F32),
                   jax.ShapeDtypeStruct((b, SSM_HEADS, SSM_P, SSM_N), F32)],
        scratch_shapes=[pltpu.VMEM((c + SUBLANES, SSM_CONV_DIM), F32),
                        pltpu.VMEM((SSM_HEADS, SSM_P, SSM_N), F32),
                        pltpu.VMEM((c, SSM_INNER), F32)],
        compiler_params=_cparams(("arbitrary", "arbitrary")),
        name="ssd_scan",
    )(z, xbc, sm, smt, conv0, h0, conv_w, conv_b, prow, pcol, norm_w)


def _rope_tables(pos, heads):
    half = ROPE_DIM // 2
    inv_freq = ROPE_THETA ** (-jnp.arange(half, dtype=F32) / half)
    ang = pos.astype(F32)[:, None] * inv_freq[None, :]
    cos, sin = jnp.cos(ang), jnp.sin(ang)
    n = pos.shape[0]
    pad = jnp.zeros((n, SWA_D - ROPE_DIM), F32)
    c_h = jnp.concatenate([cos, cos, pad + 1.0], -1)
    a_h = jnp.concatenate([-sin, jnp.zeros_like(sin), pad], -1)
    b_h = jnp.concatenate([jnp.zeros_like(sin), sin, pad], -1)
    return tuple(jnp.tile(t, (1, heads)) for t in (c_h, a_h, b_h))


def _rope(x, tc, ta, tb):
    half = ROPE_DIM // 2
    w = x.shape[-1]
    return x * tc + pltpu.roll(x, w - half, 1) * ta + pltpu.roll(x, half, 1) * tb


def _sink_attend(q, k, v, valid, sink):
    s = _bdot(q, k, _dot_nt) * (SWA_D ** -0.5)
    s = jnp.where(valid, s, -jnp.inf)
    m = jnp.maximum(jnp.max(s, -1, keepdims=True), sink)
    p = jnp.exp(s - m)
    den = jnp.sum(p, -1, keepdims=True) + jnp.exp(sink - m)
    return _bdot(p / den, v)


def _swa_prompt_kernel(q_ref, kp_ref, kc_ref, vp_ref, vc_ref, qt_ref, ktp_ref, ktc_ref, sink_ref,
                       o_ref, ko_ref, vo_ref):
    i = pl.program_id(1)
    w = WINDOW
    q = _rope(q_ref[...], qt_ref[0], qt_ref[1], qt_ref[2])
    kc = _rope(kc_ref[...], ktc_ref[0], ktc_ref[1], ktc_ref[2])
    kp = _rope(kp_ref[...], ktp_ref[0], ktp_ref[1], ktp_ref[2])
    vc = vc_ref[...]
    ko_ref[...] = kc
    vo_ref[...] = vc
    kk = jnp.concatenate([kp, kc], 0)
    vv = jnp.concatenate([vp_ref[...], vc], 0)
    qi = _iota2((w, 2 * w), 0) + w
    kj = _iota2((w, 2 * w), 1)
    valid = (kj <= qi) & (qi - kj < w) & ((kj >= w) | (i > 0))
    valid = jnp.concatenate([valid] * SWA_GRP, 0)
    for g in range(SWA_KV):
        kg = kk[:, g * SWA_D:(g + 1) * SWA_D]
        vg = vv[:, g * SWA_D:(g + 1) * SWA_D]
        qg = jnp.concatenate([q[:, (g * SWA_GRP + j) * SWA_D:(g * SWA_GRP + j + 1) * SWA_D]
                              for j in range(SWA_GRP)], 0)
        sk = jnp.concatenate([jnp.broadcast_to(sink_ref[0:1, g * SWA_GRP + j:g * SWA_GRP + j + 1], (w, 1))
                              for j in range(SWA_GRP)], 0)
        og = _sink_attend(qg, kg, vg, valid, sk)
        for j in range(SWA_GRP):
            hd = g * SWA_GRP + j
            o_ref[:, hd * SWA_D:(hd + 1) * SWA_D] = og[j * w:(j + 1) * w].astype(o_ref.dtype)


def _swa_prompt(proj, qt, kt, sinks):
    b, l = proj.shape[0], proj.shape[1]
    w = WINDOW
    nb = l // w
    kvw = SWA_KV * SWA_D
    qw = SWA_HEADS * SWA_D
    prev = lambda i, n: (i, jnp.maximum(n - 1, 0), COL_SWK // kvw)
    prev_v = lambda i, n: (i, jnp.maximum(n - 1, 0), COL_SWV // kvw)
    return pl.pallas_call(
        _swa_prompt_kernel,
        grid=(b, nb),
        in_specs=[pl.BlockSpec((None, w, qw), lambda i, n: (i, n, COL_SWQ // qw)),
                  pl.BlockSpec((None, w, kvw), prev),
                  pl.BlockSpec((None, w, kvw), lambda i, n: (i, n, COL_SWK // kvw)),
                  pl.BlockSpec((None, w, kvw), prev_v),
                  pl.BlockSpec((None, w, kvw), lambda i, n: (i, n, COL_SWV // kvw)),
                  pl.BlockSpec((3, w, qw), lambda i, n: (0, n, 0)),
                  pl.BlockSpec((3, w, kvw), lambda i, n: (0, jnp.maximum(n - 1, 0), 0)),
                  pl.BlockSpec((3, w, kvw), lambda i, n: (0, n, 0)),
                  pl.BlockSpec((1, SWA_HEADS), lambda i, n: (0, 0))],
        out_specs=[pl.BlockSpec((None, w, qw), lambda i, n: (i, n, 0)),
                   pl.BlockSpec((None, w, kvw), lambda i, n: (i, 0, 0)),
                   pl.BlockSpec((None, w, kvw), lambda i, n: (i, 0, 0))],
        out_shape=[jax.ShapeDtypeStruct((b, l, qw), BF16),
                   jax.ShapeDtypeStruct((b, w, kvw), F32),
                   jax.ShapeDtypeStruct((b, w, kvw), F32)],
        compiler_params=_cparams(("arbitrary", "arbitrary")),
        name="swa_prompt",
    )(proj, proj, proj, proj, proj, qt, kt, kt, sinks)


def _swa_sample_kernel(q_ref, k_ref, v_ref, kb_ref, vb_ref, qt_ref, kt_ref, sink_ref,
                       o_ref, ko_ref, vo_ref, kk_ref, vv_ref, *, l):
    w = WINDOW
    lp = SUBLANES
    q = _rope(q_ref[...], qt_ref[0], qt_ref[1], qt_ref[2])
    k = _rope(k_ref[...], kt_ref[0], kt_ref[1], kt_ref[2])
    kk_ref[0:w, :] = kb_ref[...]
    kk_ref[w:w + lp, :] = k
    vv_ref[0:w, :] = vb_ref[...]
    vv_ref[w:w + lp, :] = v_ref[...]
    ko_ref[...] = kk_ref[l:l + w, :]
    vo_ref[...] = vv_ref[l:l + w, :]
    kk = kk_ref[...]
    vv = vv_ref[...]
    qi = _iota2((lp, w + lp), 0) + w
    kj = _iota2((lp, w + lp), 1)
    valid = (kj <= qi) & (qi - kj < w) & (kj < w + l)
    valid = jnp.concatenate([valid] * SWA_GRP, 0)
    for g in range(SWA_KV):
        kg = kk[:, g * SWA_D:(g + 1) * SWA_D]
        vg = vv[:, g * SWA_D:(g + 1) * SWA_D]
        qg = jnp.concatenate([q[:, (g * SWA_GRP + j) * SWA_D:(g * SWA_GRP + j + 1) * SWA_D]
                              for j in range(SWA_GRP)], 0)
        sk = jnp.concatenate([jnp.broadcast_to(sink_ref[0:1, g * SWA_GRP + j:g * SWA_GRP + j + 1], (lp, 1))
                              for j in range(SWA_GRP)], 0)
        og = _sink_attend(qg, kg, vg, valid, sk)
        for j in range(SWA_GRP):
            hd = g * SWA_GRP + j
            o_ref[:, hd * SWA_D:(hd + 1) * SWA_D] = og[j * lp:(j + 1) * lp]


def _swa_sample(q, k, v, kbuf, vbuf, qt, kt, sinks, l):
    b = q.shape[0]
    w = WINDOW
    lp = SUBLANES
    kvw = SWA_KV * SWA_D
    qw = SWA_HEADS * SWA_D
    kern = functools.partial(_swa_sample_kernel, l=l)
    return pl.pallas_call(
        kern,
        grid=(b,),
        in_specs=[pl.BlockSpec((None, lp, qw), lambda i: (i, 0, 0)),
                  pl.BlockSpec((None, lp, kvw), lambda i: (i, 0, 0)),
                  pl.BlockSpec((None, lp, kvw), lambda i: (i, 0, 0)),
                  pl.BlockSpec((None, w, kvw), lambda i: (i, 0, 0)),
                  pl.BlockSpec((None, w, kvw), lambda i: (i, 0, 0)),
                  pl.BlockSpec((3, lp, qw), lambda i: (0, 0, 0)),
                  pl.BlockSpec((3, lp, kvw), lambda i: (0, 0, 0)),
                  pl.BlockSpec((1, SWA_HEADS), lambda i: (0, 0))],
        out_specs=[pl.BlockSpec((None, lp, qw), lambda i: (i, 0, 0)),
                   pl.BlockSpec((None, w, kvw), lambda i: (i, 0, 0)),
                   pl.BlockSpec((None, w, kvw), lambda i: (i, 0, 0))],
        out_shape=[jax.ShapeDtypeStruct((b, lp, qw), F32),
                   jax.ShapeDtypeStruct((b, w, kvw), F32),
                   jax.ShapeDtypeStruct((b, w, kvw), F32)],
        scratch_shapes=[pltpu.VMEM((w + lp, kvw), F32), pltpu.VMEM((w + lp, kvw), F32)],
        compiler_params=_cparams(("arbitrary",)),
        name="swa_sample",
    )(q, k, v, kbuf, vbuf, qt, kt, sinks)


def _merge_kernel(oa_ref, ob_ref, oc_ref, ga_ref, gb_ref, gc_ref, wa_ref, wb_ref, wc_ref, o_ref):
    acc = _sigmoid(ga_ref[...]) * _bdot(oa_ref[...], wa_ref[...])
    acc = acc + _sigmoid(gb_ref[...]) * _bdot(ob_ref[...], wb_ref[...])
    acc = acc + _sigmoid(gc_ref[...]) * _bdot(oc_ref[...], wc_ref[...])
    o_ref[...] = acc.astype(o_ref.dtype)


def _merge(oa, ob, oc, proj, wbr, tm):
    t = oa.shape[0]
    tn = 512
    nj = D_MODEL // tn
    gate = lambda k: pl.BlockSpec((tm, tn), lambda i, j: (i, COL_MG // tn + k * nj + j))
    wsp = lambda k: pl.BlockSpec((None, BRANCH_W, tn), lambda i, j: (k, 0, j))
    osp = pl.BlockSpec((tm, BRANCH_W), lambda i, j: (i, 0))
    return pl.pallas_call(
        _merge_kernel,
        grid=(t // tm, nj),
        in_specs=[osp, osp, osp, gate(0), gate(1), gate(2), wsp(0), wsp(1), wsp(2)],
        out_specs=pl.BlockSpec((tm, tn), lambda i, j: (i, j)),
        out_shape=jax.ShapeDtypeStruct((t, D_MODEL), BF16),
        compiler_params=_cparams(("arbitrary", "arbitrary")),
        name="merge",
    )(oa, ob, oc, proj, proj, proj, wbr, wbr, wbr)


def _layer_norm(r, g, b):
    mu = jnp.mean(r, -1, keepdims=True)
    rc = r - mu
    var = jnp.mean(rc * rc, -1, keepdims=True)
    return rc * lax.rsqrt(var + LN_EPS) * g + b


def _to_slab(ref, val):
    rows = val.shape[0]
    for s in range(SLAB):
        ref[pl.ds(s, rows, stride=SLAB), :] = val[:, s * LANES:(s + 1) * LANES]


def _from_slab(ref, lead, start, rows, dtype):
    parts = []
    for s in range(SLAB):
        idx = (pl.ds(start * SLAB + s, rows, stride=SLAB), slice(None))
        parts.append(ref[(lead,) + idx if lead is not None else idx].astype(dtype))
    return jnp.concatenate(parts, axis=1)


def _out_ln_kernel(m_ref, w_ref, x_ref, g1_ref, sc_ref, sh_ref, lg_ref, lb_ref, x1_ref, h2_ref, hs_ref, *, alpha):
    mix = _dot(m_ref[...], w_ref[...])
    x1 = _layer_norm(alpha * x_ref[...] + g1_ref[...] * mix, lg_ref[...], lb_ref[...])
    x1_ref[...] = x1
    h2 = x1 * (1.0 + sc_ref[...]) + sh_ref[...]
    h2_ref[...] = h2
    _to_slab(hs_ref, h2)


def _out_ln(merged, w_out, x, g1, sc2, sh2, ln_g, ln_b, tm, tiles_per_group, alpha):
    t, d = x.shape
    row = pl.BlockSpec((tm, d), lambda i: (i, 0))
    vec = pl.BlockSpec((1, d), lambda i: (0, 0))
    kern = functools.partial(_out_ln_kernel, alpha=alpha)
    return pl.pallas_call(
        kern,
        grid=(t // tm,),
        in_specs=[row, pl.BlockSpec((d, d), lambda i: (0, 0)), row,
                  _mod_spec(g1, tm, tiles_per_group), _mod_spec(sc2, tm, tiles_per_group),
                  _mod_spec(sh2, tm, tiles_per_group), vec, vec],
        out_specs=[row, row, pl.BlockSpec((tm * SLAB, LANES), lambda i: (i, 0))],
        out_shape=[jax.ShapeDtypeStruct((t, d), F32), jax.ShapeDtypeStruct((t, d), F32),
                   jax.ShapeDtypeStruct((t * SLAB, LANES), F32)],
        compiler_params=_cparams(("arbitrary",)),
        name="out_ln",
    )(merged, w_out, x, g1, sc2, sh2, ln_g, ln_b)


ROUTER_TR = 640


def _router_kernel(h_ref, w_ref, b_ref, eidx_ref, rank_ref, wgt_ref, cnt_ref, carry_ref):
    i = pl.program_id(0)
    tr = h_ref.shape[0]
    ng, gs = N_GROUPS, GROUP_SIZE

    @pl.when(i == 0)
    def _():
        carry_ref[...] = jnp.zeros_like(carry_ref)

    logits = _dot3(w_ref[...], h_ref[...], _dot_nt)
    scores = _sigmoid(logits)
    sc3 = scores.reshape(ng, gs, tr)
    ch3 = (scores + b_ref[...]).reshape(ng, gs, tr)
    io_e = _iota2((ng, gs, tr), 1)
    io_g = _iota2((ng, 1, tr), 0)
    io_x = _iota2((ng, gs, tr), 0) * gs + io_e
    ninf = -jnp.inf

    m1 = jnp.max(ch3, 1, keepdims=True)
    i1 = jnp.min(jnp.where(ch3 == m1, io_e, gs), 1, keepdims=True)
    m2 = jnp.max(jnp.where(io_e == i1, ninf, ch3), 1, keepdims=True)
    grp = m1 + m2
    keep = jnp.zeros((ng, 1, tr), jnp.bool_)
    for _ in range(TOPK_GROUPS):
        m = jnp.max(grp, 0, keepdims=True)
        first = jnp.min(jnp.where(grp == m, io_g, ng), 0, keepdims=True)
        hit = io_g == first
        keep = keep | hit
        grp = jnp.where(hit, ninf, grp)

    cm = jnp.where(keep, ch3, ninf)
    hits = []
    firsts = []
    for _ in range(TOP_K):
        m = jnp.max(jnp.max(cm, 1, keepdims=True), 0, keepdims=True)
        cand = jnp.where(cm == m, io_x, N_EXPERTS)
        first = jnp.min(jnp.min(cand, 1, keepdims=True), 0, keepdims=True)
        hit = io_x == first
        hits.append(hit)
        firsts.append(first)
        cm = jnp.where(hit, ninf, cm)
    sel = hits[0]
    for hit in hits[1:]:
        sel = sel | hit
    self32 = jnp.where(sel, 1.0, 0.0).astype(F32)
    wsel = sc3 * self32
    den = jnp.sum(jnp.sum(wsel, 1, keepdims=True), 0, keepdims=True)
    comb = wsel / den * ROUTED_SCALE

    sel2 = self32.reshape(N_EXPERTS, tr)
    upper = jnp.where(_iota2((tr, tr), 0) <= _iota2((tr, tr), 1), 1.0, 0.0).astype(BF16)
    incl = _dot(sel2.astype(BF16), upper)
    carry = carry_ref[:, 0:1]
    rank3 = (carry + incl - sel2).reshape(ng, gs, tr)
    for r in range(TOP_K):
        hf = jnp.where(hits[r], 1.0, 0.0).astype(F32)
        rk = jnp.sum(jnp.sum(hf * rank3, 1, keepdims=True), 0, keepdims=True)
        wg = jnp.sum(jnp.sum(hf * comb, 1, keepdims=True), 0, keepdims=True)
        eidx_ref[r:r + 1, :] = firsts[r].reshape(1, tr)
        rank_ref[r:r + 1, :] = rk.reshape(1, tr).astype(I32)
        wgt_ref[r:r + 1, :] = wg.reshape(1, tr)
    new_carry = carry + incl[:, tr - 1:tr]
    carry_ref[...] = jnp.broadcast_to(new_carry, carry_ref.shape)
    cnt_ref[...] = jnp.broadcast_to(new_carry, cnt_ref.shape).astype(I32)


def _router(h2, wr_t, bias_col):
    t, d = h2.shape
    tr = ROUTER_TR
    out = pl.BlockSpec((TOP_K, tr), lambda i: (0, i))
    return pl.pallas_call(
        _router_kernel,
        grid=(t // tr,),
        in_specs=[pl.BlockSpec((tr, d), lambda i: (i, 0)),
                  pl.BlockSpec((N_EXPERTS, d), lambda i: (0, 0)),
                  pl.BlockSpec((N_EXPERTS, 1), lambda i: (0, 0))],
        out_specs=[out, out, out, pl.BlockSpec((N_EXPERTS, LANES), lambda i: (0, 0))],
        out_shape=[jax.ShapeDtypeStruct((TOP_K, t), I32), jax.ShapeDtypeStruct((TOP_K, t), I32),
                   jax.ShapeDtypeStruct((TOP_K, t), F32), jax.ShapeDtypeStruct((N_EXPERTS, LANES), I32)],
        scratch_shapes=[pltpu.VMEM((N_EXPERTS, LANES), F32)],
        compiler_params=_cparams(("arbitrary",)),
        name="router",
    )(h2, wr_t, bias_col)


def _invert_kernel(slot_ref, tok_ref):
    def init(r, carry):
        for u in range(LANES):
            tok_ref[r, u] = 0
        return carry
    lax.fori_loop(0, tok_ref.shape[0], init, 0)

    def body(t, carry):
        rows = [slot_ref[k, t] for k in range(TOP_K)]
        for s in rows:
            tok_ref[lax.shift_right_logical(s, LANE_BITS), s & (LANES - 1)] = t
        return carry
    lax.fori_loop(0, slot_ref.shape[1], body, 0)


def _invert(slots, n_rows):
    return pl.pallas_call(
        _invert_kernel,
        grid_spec=pltpu.PrefetchScalarGridSpec(
            num_scalar_prefetch=1,
            grid=(1,),
            in_specs=[],
            out_specs=pl.BlockSpec(memory_space=pltpu.SMEM)),
        out_shape=jax.ShapeDtypeStruct((n_rows // LANES, LANES), I32),
        compiler_params=_cparams(("arbitrary",)),
        name="invert_slots",
    )(slots)


def _cast_weights(te_ref, wg_ref, wu_ref, wd_ref, wgb_ref, wub_ref, wdb_ref):
    i = pl.program_id(0)
    prev = te_ref[jnp.maximum(i - 1, 0)]

    @pl.when((i == 0) | (te_ref[i] != prev))
    def _():
        wgb_ref[...] = wg_ref[...].astype(BF16)
        wub_ref[...] = wu_ref[...].astype(BF16)
        wdb_ref[...] = wd_ref[...].astype(BF16)


def _swiglu(xb, wgb_ref, wub_ref, wdb_ref):
    a = _dot(xb, wgb_ref[...])
    u = _dot(xb, wub_ref[...])
    return _dot((_silu(a) * u).astype(BF16), wdb_ref[...])


def _ffn_kernel(te_ref, na_ref, x_ref, wg_ref, wu_ref, wd_ref, y_ref, wgb_ref, wub_ref, wdb_ref):
    _cast_weights(te_ref, wg_ref, wu_ref, wd_ref, wgb_ref, wub_ref, wdb_ref)
    y_ref[...] = _swiglu(x_ref[...].astype(BF16), wgb_ref, wub_ref, wdb_ref)


def _slab_copy(src_ref, row, dst_ref, buf, slot, sem):
    src = src_ref.at[pl.ds(pl.multiple_of(row * SLAB, SLAB), SLAB)]
    dst = dst_ref.at[buf, pl.ds(pl.multiple_of(slot * SLAB, SLAB), SLAB)]
    return pltpu.make_async_copy(src, dst, sem.at[buf])


GATHER_PRIORITY = 1
GFFN_BURSTS = (96, 96, 64)


def _gffn_kernel(te_ref, na_ref, tok_ref, h_ref, wg_ref, wu_ref, wd_ref, y_ref,
                 wgb_ref, wub_ref, wdb_ref, xbuf_ref, act_ref, sem, *, tm):
    i = pl.program_id(0)
    na = na_ref[0]
    nxt = i + 1
    unroll = SUBLANES
    tok_rows = tm // LANES
    starts = [sum(GFFN_BURSTS[:k]) for k in range(len(GFFN_BURSTS))]

    def issue(tile, b, r0, n):
        for r in range(r0, r0 + n):
            q, c = divmod(r, LANES)
            _slab_copy(h_ref, tok_ref[tile * tok_rows + q, c], xbuf_ref, b, r, sem).start(priority=GATHER_PRIORITY)

    def prefetch(k):
        @pl.when(nxt < na)
        def _():
            issue(nxt, nxt % 2, starts[k], GFFN_BURSTS[k])

    @pl.when(i == 0)
    def _():
        issue(0, 0, 0, tm)

    _cast_weights(te_ref, wg_ref, wu_ref, wd_ref, wgb_ref, wub_ref, wdb_ref)
    active = i < na

    @pl.when(active)
    def _():
        b = i % 2

        def wbody(j, carry):
            for u in range(unroll):
                _slab_copy(h_ref, 0, xbuf_ref, b, 0, sem).wait()
            return carry
        lax.fori_loop(0, tm // unroll, wbody, 0)

    prefetch(0)

    @pl.when(active)
    def _():
        xb = _from_slab(xbuf_ref, i % 2, 0, tm, BF16)
        a = _dot(xb, wgb_ref[...])
        u = _dot(xb, wub_ref[...])
        act_ref[...] = (_silu(a) * u).astype(BF16)

    prefetch(1)

    @pl.when(active)
    def _():
        _to_slab(y_ref, _dot(act_ref[...], wdb_ref[...]))

    prefetch(2)

    @pl.when(i >= na)
    def _():
        y_ref[...] = jnp.zeros_like(y_ref)


def _wspecs(layer, d, ff):
    wmap = lambda i, te, *_: (layer, te[i], 0, 0)
    return [pl.BlockSpec((None, None, d, ff), wmap), pl.BlockSpec((None, None, d, ff), wmap),
            pl.BlockSpec((None, None, ff, d), wmap)]


def _wscratch(d, ff):
    return [pltpu.VMEM((d, ff), BF16), pltpu.VMEM((d, ff), BF16), pltpu.VMEM((ff, d), BF16)]


def _ffn(te, na, x, wg, wu, wd, layer, tm):
    t, d = x.shape
    ff = wg.shape[-1]
    return pl.pallas_call(
        _ffn_kernel,
        grid_spec=pltpu.PrefetchScalarGridSpec(
            num_scalar_prefetch=2,
            grid=(t // tm,),
            in_specs=[pl.BlockSpec((tm, d), lambda i, *_: (i, 0))] + _wspecs(layer, d, ff),
            out_specs=pl.BlockSpec((tm, d), lambda i, *_: (i, 0)),
            scratch_shapes=_wscratch(d, ff)),
        out_shape=jax.ShapeDtypeStruct((t, d), F32),
        compiler_params=_cparams(("arbitrary",)),
        name="shared_ffn",
    )(te, na, x, wg, wu, wd)


def _gffn(te, na, tok, h2s, wg, wu, wd, layer, tm):
    d = D_MODEL
    ff = wg.shape[-1]
    n_rows = tok.shape[0] * tok.shape[1]
    assert sum(GFFN_BURSTS) == tm and tm % LANES == 0
    kern = functools.partial(_gffn_kernel, tm=tm)
    return pl.pallas_call(
        kern,
        grid_spec=pltpu.PrefetchScalarGridSpec(
            num_scalar_prefetch=3,
            grid=(n_rows // tm,),
            in_specs=[pl.BlockSpec(memory_space=pl.ANY)] + _wspecs(layer, d, ff),
            out_specs=pl.BlockSpec((tm * SLAB, LANES), lambda i, *_: (i, 0)),
            scratch_shapes=_wscratch(d, ff) + [pltpu.VMEM((2, tm * SLAB, LANES), F32),
                                               pltpu.VMEM((tm, ff), BF16),
                                               pltpu.SemaphoreType.DMA((2,))]),
        out_shape=jax.ShapeDtypeStruct((n_rows * SLAB, LANES), F32),
        compiler_params=_cparams(("arbitrary",)),
        name="expert_ffn",
    )(te, na, tok, h2s, wg, wu, wd)


COMBINE_TC = 32


def _combine_kernel(slot_ref, ys_ref, w_ref, sh_ref, x_ref, g2_ref, lg_ref, lb_ref, o_ref, buf_ref, g2x_ref, sem,
                    *, alpha):
    i = pl.program_id(0)
    n = pl.num_programs(0)
    tc = COMBINE_TC

    def issue(tile, b):
        def body(j, carry):
            for k in range(TOP_K):
                _slab_copy(ys_ref, slot_ref[k, tile * tc + j], buf_ref, b, k * tc + j, sem).start(priority=k % 2)
            return carry
        lax.fori_loop(0, tc, body, 0)

    @pl.when(i == 0)
    def _():
        issue(0, 0)

    @pl.when(i + 1 < n)
    def _():
        issue(i + 1, (i + 1) % 2)

    b = i % 2

    def wbody(j, carry):
        for k in range(TOP_K):
            _slab_copy(ys_ref, 0, buf_ref, b, 0, sem).wait()
        return carry
    lax.fori_loop(0, tc, wbody, 0)

    for s in range(SLAB):
        for u in range(MOD_GROUP):
            g2x_ref[s, pl.ds(u, tc // MOD_GROUP, stride=MOD_GROUP), :] = g2_ref[:, s * LANES:(s + 1) * LANES]
    g2 = jnp.concatenate([g2x_ref[s] for s in range(SLAB)], axis=1)

    w = w_ref[...]
    acc = sh_ref[...]
    for k in range(TOP_K):
        acc = acc + w[:, k:k + 1] * _from_slab(buf_ref, b, k * tc, tc, F32)
    o_ref[...] = _layer_norm(alpha * x_ref[...] + g2 * acc, lg_ref[...], lb_ref[...])


def _combine(slots, ys, wgt, shared, x1, g2grp, ln_g, ln_b, alpha):
    t, d = x1.shape
    tc = COMBINE_TC
    row = pl.BlockSpec((tc, d), lambda i, *_: (i, 0))
    vec = pl.BlockSpec((1, d), lambda i, *_: (0, 0))
    kern = functools.partial(_combine_kernel, alpha=alpha)
    return pl.pallas_call(
        kern,
        grid_spec=pltpu.PrefetchScalarGridSpec(
            num_scalar_prefetch=1,
            grid=(t // tc,),
            in_specs=[pl.BlockSpec(memory_space=pl.ANY),
                      pl.BlockSpec((tc, TOP_K), lambda i, *_: (i, 0)),
                      row, row, pl.BlockSpec((tc // MOD_GROUP, d), lambda i, *_: (i, 0)), vec, vec],
            out_specs=row,
            scratch_shapes=[pltpu.VMEM((2, TOP_K * tc * SLAB, LANES), F32), pltpu.VMEM((SLAB, tc, LANES), F32),
                            pltpu.SemaphoreType.DMA((2,))]),
        out_shape=jax.ShapeDtypeStruct((t, d), F32),
        compiler_params=_cparams(("arbitrary",)),
        name="combine_ln",
    )(slots, ys, wgt, shared, x1, g2grp, ln_g, ln_b)


def _moe(h2, h2s, x1, g2grp, p, layer, alpha):
    t, d = h2.shape
    tm = MOE_TM
    eidx, rank, wgt, cnt = _router(h2, p['w_router'].T, p['router_bias'].reshape(N_EXPERTS, 1))
    counts = cnt[:, 0]
    tiles = (counts + tm - 1) // tm
    tile_end = jnp.cumsum(tiles)
    offs = ((tile_end - tiles) * tm).astype(I32)
    n_active = tile_end[-1]
    n_tiles = (t * TOP_K + N_EXPERTS * (tm - 1)) // tm
    ids = jnp.minimum(jnp.arange(n_tiles, dtype=I32), n_active - 1)
    te = jnp.sum((tile_end[None, :] <= ids[:, None]).astype(I32), axis=1)
    onehot = eidx[..., None] == jnp.arange(N_EXPERTS, dtype=I32)
    slots = rank + jnp.sum(jnp.where(onehot, offs, 0), -1)
    tok = _invert(slots, n_tiles * tm)
    ys = _gffn(te, n_active.reshape(1).astype(I32), tok, h2s, p['w_exp_gate'], p['w_exp_up'], p['w_exp_down'],
               layer, tm)
    tsh = 640
    nsh = t // tsh
    sh4 = lambda w: w.reshape((w.shape[0], 1) + w.shape[1:])
    shared = _ffn(jnp.zeros((nsh,), I32), jnp.full((1,), nsh, I32), h2,
                  sh4(p['w_sh_gate']), sh4(p['w_sh_up']), sh4(p['w_sh_down']), layer, tsh)
    return _combine(slots, ys, wgt.T, shared, x1, g2grp, p['ln2_g'].reshape(1, d), p['ln2_b'].reshape(1, d), alpha)


def _chunk_t(sm, c):
    b, l, _ = sm.shape
    return jnp.swapaxes(sm.reshape(b, l // c, c, SMALL_W)[..., :32], -1, -2)


def _pad_rows(a, rows):
    return jnp.pad(a, ((0, 0), (0, rows - a.shape[1]), (0, 0)))


def _mixer_group(x, mod, states, p, tabs, *, prompt):
    b, l, d = x.shape
    t = b * l
    dn_conv0, dn_s0, ssm_conv0, ssm_h0, kv_buf = states
    sh1, sc1, g1, sh2, sc2, g2 = jnp.split(mod, 6, axis=-1)
    if prompt:
        tm = 1024
        tpg = l // tm
        shape = lambda m: m.reshape(b, 1, d)
    else:
        tm = t
        tpg = 1
        shape = lambda m: jnp.repeat(m, l, axis=0).reshape(1, t, d)
    proj, small = _in_proj(x.reshape(t, d), shape(sc1), shape(sh1), p['w_main'], p['w_small'], tm, tpg)
    proj3 = proj.reshape(b, l, MAIN_W)
    small3 = small.reshape(b, l, SMALL_W)
    if prompt:
        c = CHUNK
        dn_in = (proj3, proj3)
        dn_cb = dict(cb_qkv=COL_QKV // DN_CONV_DIM, cb_gate=COL_DNG // DN_KEY)
        ssd_in = (proj3, proj3)
        ssd_cb = dict(cb_z=COL_SSZ // SSM_INNER, cb_xbc=COL_XBC // SSM_CONV_DIM)
        sm_in = small3
        odt = BF16
    else:
        c = SUBLANES
        cut = lambda c0, w: _pad_rows(proj3[:, :, c0:c0 + w], c)
        dn_in = (cut(COL_QKV, DN_CONV_DIM), cut(COL_DNG, DN_KEY))
        dn_cb = dict(cb_qkv=0, cb_gate=0)
        ssd_in = (cut(COL_SSZ, SSM_INNER), cut(COL_XBC, SSM_CONV_DIM))
        ssd_cb = dict(cb_z=0, cb_xbc=0)
        sm_in = _pad_rows(small3, c)
        odt = F32
    valid = c if prompt else l
    smt = _chunk_t(sm_in, c)
    o_a, dn_conv, dn_s = _dn(dn_in[0], dn_in[1], sm_in, smt, dn_conv0, dn_s0, p['dn_conv_w'], p['dn_prow'],
                             p['dn_pcol'], p['dn_norm_w'].reshape(1, DN_D), c=c, valid=valid, out_dtype=odt, **dn_cb)
    o_b, ssm_conv, ssm_h = _ssd(ssd_in[0], ssd_in[1], sm_in, smt, ssm_conv0, ssm_h0, p['ssm_conv_w'],
                                p['ssm_conv_b'].reshape(1, SSM_CONV_DIM), p['ssm_prow'], p['ssm_pcol'],
                                p['ssm_norm_w'].reshape(1, SSM_INNER), c=c, valid=valid, out_dtype=odt, **ssd_cb)
    sinks = p['swa_sinks'].reshape(1, SWA_HEADS)
    if prompt:
        o_c, k_new, v_new = _swa_prompt(proj3, tabs[0], tabs[1], sinks)
    else:
        o_c, k_new, v_new = _swa_sample(cut(COL_SWQ, SWA_HEADS * SWA_D), cut(COL_SWK, SWA_KV * SWA_D),
                                        cut(COL_SWV, SWA_KV * SWA_D),
                                        kv_buf[0].reshape(b, WINDOW, SWA_KV * SWA_D),
                                        kv_buf[1].reshape(b, WINDOW, SWA_KV * SWA_D), tabs[0], tabs[1], sinks, l)
    flat = lambda o: o[:, :l].reshape(t, BRANCH_W).astype(BF16)
    merged = _merge(flat(o_a), flat(o_b), flat(o_c), proj, p['w_branch'], tm=min(tm, 512))
    tm2 = min(tm, 512)
    tpg2 = l // tm2 if prompt else 1
    shape2 = (lambda m: m.reshape(b, 1, d)) if prompt else shape
    x1, h2, h2s = _out_ln(merged, p['w_out'], x.reshape(t, d), shape2(g1), shape2(sc2), shape2(sh2),
                          p['ln1_g'].reshape(1, d), p['ln1_b'].reshape(1, d), tm2, tpg2, p['alpha'])
    g2grp = jnp.repeat(g2, l // MOD_GROUP, axis=0)
    k_new = k_new.reshape(b, WINDOW, SWA_KV, SWA_D)
    v_new = v_new.reshape(b, WINDOW, SWA_KV, SWA_D)
    return x1, h2, h2s, g2grp, (dn_conv, dn_s, ssm_conv, ssm_h, k_new, v_new)


def _forward(x_prompt, x_sample, state_dn_conv, state_dn, state_ssm_conv, state_ssm, cache_swa_k, cache_swa_v,
             c_prompt, c_sample, w_ada, b_ada, w_in, dn_conv_w, dn_a_log, dn_dt_bias, dn_norm_w,
             ssm_conv_w, ssm_conv_b, ssm_a_log, ssm_dt_bias, ssm_d, ssm_norm_w, swa_sinks, w_branch, w_out,
             ln1_g, ln1_b, w_router, router_bias, w_exp_gate, w_exp_up, w_exp_down, w_sh_gate, w_sh_up,
             w_sh_down, ln2_g, ln2_b):
    depth = w_in.shape[0]
    bp, lp, d = x_prompt.shape
    bs, ls, _ = x_sample.shape
    tp, ts = bp * lp, bs * ls
    alpha = (2 * depth) ** 0.25
    past_len = 16384

    c_all = jnp.concatenate([c_prompt, c_sample, jnp.zeros((4, d), F32)], 0)
    mod_all = _ada(c_all, w_ada, b_ada)
    tabs_p = tuple(jnp.stack(_rope_tables(jnp.arange(lp), h)) for h in (SWA_HEADS, SWA_KV))
    tabs_s = tuple(jnp.stack(_rope_tables(past_len + jnp.arange(SUBLANES), h)) for h in (SWA_HEADS, SWA_KV))
    init_p = (jnp.zeros((bp, CONV_W - 1, DN_CONV_DIM), F32), jnp.zeros((bp, DN_HEADS, DN_D, DN_D), F32),
              jnp.zeros((bp, CONV_W - 1, SSM_CONV_DIM), F32), jnp.zeros((bp, SSM_HEADS, SSM_P, SSM_N), F32), None)

    def pad_lanes(v, at):
        return jnp.zeros((SMALL_W,), F32).at[at:at + v.shape[0]].set(v)

    yp, ys = x_prompt, x_sample
    new_p, new_s = [], []
    for l in range(depth):
        w = w_in[l]
        seg = lambda a, n: w[:, a:a + n]
        w_main = jnp.concatenate([seg(8224, 6144), seg(0, 3072), seg(5136, 1536), seg(7712, 256), seg(7968, 256),
                                  seg(3072, 1024), seg(4112, 1024), seg(6688, 1024)], 1).astype(BF16)
        w_small = jnp.concatenate([seg(4096, 16), seg(6672, 16), jnp.zeros((d, SMALL_W - 32), F32)], 1).astype(BF16)
        p = {'w_main': w_main, 'w_small': w_small, 'alpha': alpha,
             'dn_conv_w': dn_conv_w[l], 'dn_norm_w': dn_norm_w[l],
             'dn_prow': jnp.stack([pad_lanes(dn_a_log[l], SM_A), pad_lanes(dn_dt_bias[l], SM_A)]),
             'dn_pcol': jnp.stack([dn_a_log[l], dn_dt_bias[l]], 1),
             'ssm_conv_w': ssm_conv_w[l], 'ssm_conv_b': ssm_conv_b[l], 'ssm_norm_w': ssm_norm_w[l],
             'ssm_prow': jnp.stack([pad_lanes(ssm_dt_bias[l], SM_DT), pad_lanes(ssm_a_log[l], SM_DT),
                                    pad_lanes(ssm_d[l], SM_DT)]),
             'ssm_pcol': jnp.stack([ssm_dt_bias[l], ssm_a_log[l]], 1),
             'swa_sinks': swa_sinks[l], 'w_branch': w_branch[l].astype(BF16), 'w_out': w_out[l].astype(BF16),
             'ln1_g': ln1_g[l], 'ln1_b': ln1_b[l], 'w_router': w_router[l], 'router_bias': router_bias[l],
             'w_exp_gate': w_exp_gate, 'w_exp_up': w_exp_up, 'w_exp_down': w_exp_down,
             'w_sh_gate': w_sh_gate, 'w_sh_up': w_sh_up, 'w_sh_down': w_sh_down,
             'ln2_g': ln2_g[l], 'ln2_b': ln2_b[l]}
        mod = mod_all[l]
        x1p, h2p, hsp, g2p, st_p = _mixer_group(yp, mod[:bp], init_p, p, tabs_p, prompt=True)
        st_in = (state_dn_conv[l], state_dn[l], state_ssm_conv[l], state_ssm[l], (cache_swa_k[l], cache_swa_v[l]))
        x1s, h2s, hss, g2s, st_s = _mixer_group(ys, mod[bp:bp + bs], st_in, p, tabs_s, prompt=False)
        cat = lambda a, b_: jnp.concatenate([a, b_], 0)
        x2 = _moe(cat(h2p, h2s), cat(hsp, hss), cat(x1p, x1s), cat(g2p, g2s), p, l, alpha)
        yp = x2[:tp].reshape(bp, lp, d)
        ys = x2[tp:].reshape(bs, ls, d)
        new_p.append(st_p)
        new_s.append(st_s)
    outs = [yp, ys]
    for k in range(6):
        outs.append(jnp.stack([s[k] for s in new_p]))
        outs.append(jnp.stack([s[k] for s in new_s]))
    return tuple(outs)


def kernel(x_prompt, x_sample, state_dn_conv, state_dn, state_ssm_conv, state_ssm, cache_swa_k, cache_swa_v, c_prompt, c_sample, w_ada, b_ada, w_in, dn_conv_w, dn_a_log, dn_dt_bias, dn_norm_w, ssm_conv_w, ssm_conv_b, ssm_a_log, ssm_dt_bias, ssm_d, ssm_norm_w, swa_sinks, w_branch, w_out, ln1_g, ln1_b, w_router, router_bias, w_exp_gate, w_exp_up, w_exp_down, w_sh_gate, w_sh_up, w_sh_down, ln2_g, ln2_b):
    return _forward(x_prompt, x_sample, state_dn_conv, state_dn, state_ssm_conv, state_ssm, cache_swa_k, cache_swa_v, c_prompt, c_sample, w_ada, b_ada, w_in, dn_conv_w, dn_a_log, dn_dt_bias, dn_norm_w, ssm_conv_w, ssm_conv_b, ssm_a_log, ssm_dt_bias, ssm_d, ssm_norm_w, swa_sinks, w_branch, w_out, ln1_g, ln1_b, w_router, router_bias, w_exp_gate, w_exp_up, w_exp_down, w_sh_gate, w_sh_up, w_sh_down, ln2_g, ln2_b)
```

```python
import functools
import math

import jax
import jax.numpy as jnp
from jax import lax
from jax.experimental import pallas as pl
from jax.experimental.pallas import tpu as pltpu

F32 = jnp.float32
BF16 = jnp.bfloat16
I32 = jnp.int32

D_MODEL = 2048
CONV_W = 4
DN_HEADS = 8
DN_D = 128
DN_KEY = DN_HEADS * DN_D
DN_CONV_DIM = 3 * DN_KEY
SSM_HEADS = 16
SSM_P = 64
SSM_INNER = SSM_HEADS * SSM_P
SSM_GROUPS = 2
SSM_N = 128
SSM_CONV_DIM = SSM_INNER + 2 * SSM_GROUPS * SSM_N
SWA_HEADS = 16
SWA_KV = 4
SWA_D = 64
SWA_GRP = SWA_HEADS // SWA_KV
WINDOW = 128
ROPE_DIM = SWA_D // 4
ROPE_THETA = 500000.0
N_BRANCH = 3
BRANCH_W = 1024
N_EXPERTS = 64
EXPERT_FF = 512
TOP_K = 8
N_GROUPS = 8
GROUP_SIZE = N_EXPERTS // N_GROUPS
TOPK_GROUPS = 4
ROUTED_SCALE = 2.5
LN_EPS = 1e-5
NORM_EPS = 1e-6
CHUNK = 64

LANES = 128
LANE_BITS = 7
SUBLANES = 8
VMEM_LIMIT = 56 * 1024 * 1024

MAIN_W = 14336
COL_MG, COL_QKV, COL_XBC, COL_SWK, COL_SWV, COL_DNG, COL_SSZ, COL_SWQ = (
    0, 6144, 9216, 10752, 11008, 11264, 12288, 13312)
SMALL_W = LANES
SM_A, SM_B, SM_DT = 0, 8, 16

MOE_TM = 256
SLAB = D_MODEL // LANES
MOD_GROUP = 4


def _cparams(sem, vmem=VMEM_LIMIT):
    return pltpu.CompilerParams(dimension_semantics=sem, vmem_limit_bytes=vmem)


def _dot(a, b):
    return jnp.dot(a, b, preferred_element_type=F32)


def _dot_nt(a, b):
    return lax.dot_general(a, b, (((1,), (1,)), ((), ())), preferred_element_type=F32)


def _dot_tn(a, b):
    return lax.dot_general(a, b, (((0,), (0,)), ((), ())), preferred_element_type=F32)


def _split(a):
    hi = a.astype(BF16)
    lo = (a - hi.astype(F32)).astype(BF16)
    return hi, lo


def _dot3(a, b, dot=_dot):
    ah, al = _split(a)
    bh, bl = _split(b)
    return dot(ah, bh) + dot(ah, bl) + dot(al, bh)


def _bdot(a, b, dot=_dot):
    return dot(a.astype(BF16), b.astype(BF16))


def _silu(x):
    return x * (1.0 / (1.0 + jnp.exp(-x)))


def _sigmoid(x):
    return 1.0 / (1.0 + jnp.exp(-x))


def _softplus(x):
    return jnp.maximum(x, 0.0) + jnp.log(1.0 + jnp.exp(-jnp.abs(x)))


def _ada_kernel(c_ref, w_ref, b_ref, o_ref):
    c = c_ref[...]
    o_ref[...] = _bdot(_silu(c), w_ref[...]) + b_ref[...]


def _ada(c_all, w_ada, b_ada):
    depth, d, n = w_ada.shape
    rows = c_all.shape[0]
    tn = 1024
    return pl.pallas_call(
        _ada_kernel,
        grid=(depth, n // tn),
        in_specs=[pl.BlockSpec((rows, d), lambda l, j: (0, 0)),
                  pl.BlockSpec((None, d, tn), lambda l, j: (l, 0, j)),
                  pl.BlockSpec((None, 1, tn), lambda l, j: (l, 0, j))],
        out_specs=pl.BlockSpec((None, rows, tn), lambda l, j: (l, 0, j)),
        out_shape=jax.ShapeDtypeStruct((depth, rows, n), F32),
        compiler_params=_cparams(("arbitrary", "arbitrary")),
        name="ada",
    )(c_all, w_ada, b_ada.reshape(depth, 1, n))


def _mod_spec(mod, tm, tiles_per_group):
    r = mod.shape[1]
    if r == 1:
        return pl.BlockSpec((None, 1, mod.shape[2]), lambda i, *_: (i // tiles_per_group, 0, 0))
    return pl.BlockSpec((None, r, mod.shape[2]), lambda i, *_: (i, 0, 0))


def _in_proj_kernel(x_ref, sc_ref, sh_ref, wm_ref, ws_ref, proj_ref, small_ref, hb_ref):
    @pl.when(pl.program_id(1) == 0)
    def _():
        h = x_ref[...] * (1.0 + sc_ref[...]) + sh_ref[...]
        hb_ref[...] = h.astype(BF16)
        small_ref[...] = _dot(hb_ref[...], ws_ref[...])

    proj_ref[...] = _dot(hb_ref[...], wm_ref[...])


def _in_proj(x, t, blk0, sc, sh, w_main, w_small, tm, tiles_per_group):
    d = x.shape[1]
    tn = 1024
    return pl.pallas_call(
        _in_proj_kernel,
        grid=(t // tm, MAIN_W // tn),
        in_specs=[pl.BlockSpec((tm, d), lambda i, j: (i + blk0, 0)),
                  _mod_spec(sc, tm, tiles_per_group),
                  _mod_spec(sh, tm, tiles_per_group),
                  pl.BlockSpec((d, tn), lambda i, j: (0, j)),
                  pl.BlockSpec((d, SMALL_W), lambda i, j: (0, 0))],
        out_specs=[pl.BlockSpec((tm, tn), lambda i, j: (i, j)),
                   pl.BlockSpec((tm, SMALL_W), lambda i, j: (i, 0))],
        out_shape=[jax.ShapeDtypeStruct((t, MAIN_W), F32),
                   jax.ShapeDtypeStruct((t, SMALL_W), F32)],
        scratch_shapes=[pltpu.VMEM((tm, d), BF16)],
        compiler_params=_cparams(("arbitrary", "arbitrary")),
        name="in_proj",
    )(x, sc, sh, w_main, w_small)


def _iota2(shape, dim):
    return lax.broadcasted_iota(I32, shape, dim)


def _conv_silu(xp_ref, x_ref, w_ref, bias, c):
    xp_ref[SUBLANES:SUBLANES + c, :] = x_ref[...]
    base = SUBLANES - (CONV_W - 1)
    y = xp_ref[base:base + c, :] * w_ref[0:1, :]
    for j in range(1, CONV_W):
        y = y + xp_ref[base + j:base + j + c, :] * w_ref[j:j + 1, :]
    if bias is not None:
        y = y + bias
    return _silu(y)


def _dn_kernel(qkv_ref, gate_ref, sm_ref, smt_ref, conv0_ref, s0_ref, cw_ref, prow_ref, pcol_ref, nw_ref,
               o_ref, conv_out_ref, s_out_ref, xp_ref, s_ref, y_ref, *, c, valid):
    n = pl.program_id(1)
    base = SUBLANES - (CONV_W - 1)

    @pl.when(n == 0)
    def _():
        xp_ref[base:SUBLANES, :] = conv0_ref[...]
        s_ref[...] = s0_ref[...]

    y = _conv_silu(xp_ref, qkv_ref, cw_ref, None, c)
    tail = xp_ref[SUBLANES + valid - (CONV_W - 1):SUBLANES + valid, :]
    conv_out_ref[...] = tail
    xp_ref[base:SUBLANES, :] = tail

    row_ok = _iota2((c, 1), 0) < valid
    col_ok = _iota2((1, c), 1) < valid
    sm = sm_ref[...]
    g_all = -jnp.exp(prow_ref[0:1, :]) * _softplus(sm + prow_ref[1:2, :])
    g_all = jnp.where(row_ok, g_all, 0.0)
    beta_all = jnp.where(row_ok, _sigmoid(sm), 0.0)
    smt = smt_ref[...]
    g_t = -jnp.exp(pcol_ref[:, 0:1]) * _softplus(smt[0:DN_HEADS, :] + pcol_ref[:, 1:2])
    g_t = jnp.where(col_ok, g_t, 0.0)
    row = _iota2((c, c), 0)
    col = _iota2((c, c), 1)
    tri = jnp.where(row >= col, 1.0, 0.0).astype(F32)
    lc_all = _dot3(tri, g_all)
    lc_t = _dot3(g_t, jnp.where(row <= col, 1.0, 0.0).astype(F32))
    causal = row >= col
    strict = row > col

    heads = range(DN_HEADS)
    y_ref[...] = y
    qs, ks, vs, lcs, betas, decays, mats, qks = [], [], [], [], [], [], [], []
    for h in heads:
        qh = y_ref[:, h * DN_D:(h + 1) * DN_D]
        kh = y_ref[:, DN_KEY + h * DN_D:DN_KEY + (h + 1) * DN_D]
        vh = y_ref[:, 2 * DN_KEY + h * DN_D:2 * DN_KEY + (h + 1) * DN_D]
        qh = qh * lax.rsqrt(jnp.sum(qh * qh, -1, keepdims=True) + NORM_EPS) * (DN_D ** -0.5)
        kh = kh * lax.rsqrt(jnp.sum(kh * kh, -1, keepdims=True) + NORM_EPS)
        qs.append(qh)
        ks.append(jnp.where(row_ok, kh, 0.0))
        vs.append(jnp.where(row_ok, vh, 0.0))
        lcs.append(lc_all[:, SM_A + h:SM_A + h + 1])
        betas.append(beta_all[:, SM_B + h:SM_B + h + 1])
        decays.append(jnp.exp(jnp.where(causal, lcs[h] - lc_t[h:h + 1, :], -jnp.inf)))
    kbs = [ks[h] * betas[h] for h in heads]
    for h in heads:
        mats.append(jnp.where(strict, _bdot(kbs[h], ks[h], _dot_nt) * decays[h], 0.0))
        qks.append(_bdot(qs[h], ks[h], _dot_nt) * decays[h])

    m1 = (row % 2 == 1) & (col == row - 1)
    eye = jnp.where(row == col, 1.0, 0.0).astype(F32)
    ts = [eye - jnp.where(m1, mats[h], 0.0) for h in heads]
    s = 2
    while s < c:
        m = ((row // s) % 2 == 1) & ((col // s) % 2 == 0) & (row // (2 * s) == col // (2 * s))
        tsp = [_split(ts[h]) for h in heads]
        ams = [_split(jnp.where(m, mats[h], 0.0)) for h in heads]
        ps = [_dot(tsp[h][0], ams[h][0]) + _dot(tsp[h][0], ams[h][1]) + _dot(tsp[h][1], ams[h][0]) for h in heads]
        psp = [_split(ps[h]) for h in heads]
        ts = [ts[h] - (_dot(psp[h][0], tsp[h][0]) + _dot(psp[h][0], tsp[h][1]) + _dot(psp[h][1], tsp[h][0]))
              for h in heads]
        s *= 2

    e_lcs = [jnp.exp(lcs[h]) for h in heads]
    us = [_dot3(ts[h], vs[h] * betas[h]) for h in heads]
    ws = [_dot3(ts[h], kbs[h] * e_lcs[h]) for h in heads]
    shs = [s_ref[h] for h in heads]
    v_news = [us[h] - _bdot(ws[h], shs[h]) for h in heads]
    os_ = [_bdot(qs[h] * e_lcs[h], shs[h]) + _bdot(qks[h], v_news[h]) for h in heads]
    lasts = [lc_all[c - 1:c, SM_A + h:SM_A + h + 1] for h in heads]
    s_news = [shs[h] * jnp.exp(lasts[h]) + _bdot(ks[h] * jnp.exp(lasts[h] - lcs[h]), v_news[h], _dot_tn)
              for h in heads]
    for h in heads:
        s_ref[h] = s_news[h]
        s_out_ref[h] = s_news[h]
        o = os_[h]
        o = o * lax.rsqrt(jnp.mean(o * o, -1, keepdims=True) + NORM_EPS) * nw_ref[...]
        sl = slice(h * DN_D, (h + 1) * DN_D)
        o_ref[:, sl] = (o * _silu(gate_ref[:, sl])).astype(o_ref.dtype)


def _dn(qkv, gate, sm, smt, conv0, s0, conv_w, prow, pcol, norm_w, *, c, valid, cb_qkv, cb_gate, out_dtype):
    b, lp = qkv.shape[0], qkv.shape[1]
    nch = lp // c
    kern = functools.partial(_dn_kernel, c=c, valid=valid)
    return pl.pallas_call(
        kern,
        grid=(b, nch),
        in_specs=[pl.BlockSpec((None, c, DN_CONV_DIM), lambda i, n: (i, n, cb_qkv)),
                  pl.BlockSpec((None, c, DN_KEY), lambda i, n: (i, n, cb_gate)),
                  pl.BlockSpec((None, c, SMALL_W), lambda i, n: (i, n, 0)),
                  pl.BlockSpec((None, None, 32, c), lambda i, n: (i, n, 0, 0)),
                  pl.BlockSpec((None, CONV_W - 1, DN_CONV_DIM), lambda i, n: (i, 0, 0)),
                  pl.BlockSpec((None, DN_HEADS, DN_D, DN_D), lambda i, n: (i, 0, 0, 0)),
                  pl.BlockSpec((CONV_W, DN_CONV_DIM), lambda i, n: (0, 0)),
                  pl.BlockSpec((2, SMALL_W), lambda i, n: (0, 0)),
                  pl.BlockSpec((DN_HEADS, 2), lambda i, n: (0, 0)),
                  pl.BlockSpec((1, DN_D), lambda i, n: (0, 0))],
        out_specs=[pl.BlockSpec((None, c, DN_KEY), lambda i, n: (i, n, 0)),
                   pl.BlockSpec((None, CONV_W - 1, DN_CONV_DIM), lambda i, n: (i, 0, 0)),
                   pl.BlockSpec((None, DN_HEADS, DN_D, DN_D), lambda i, n: (i, 0, 0, 0))],
        out_shape=[jax.ShapeDtypeStruct((b, lp, DN_KEY), out_dtype),
                   jax.ShapeDtypeStruct((b, CONV_W - 1, DN_CONV_DIM), F32),
                   jax.ShapeDtypeStruct((b, DN_HEADS, DN_D, DN_D), F32)],
        scratch_shapes=[pltpu.VMEM((c + SUBLANES, DN_CONV_DIM), F32),
                        pltpu.VMEM((DN_HEADS, DN_D, DN_D), F32),
                        pltpu.VMEM((c, DN_CONV_DIM), F32)],
        compiler_params=_cparams(("arbitrary", "arbitrary")),
        name="delta_rule",
    )(qkv, gate, sm, smt, conv0, s0, conv_w, prow, pcol, norm_w)


def _ssd_kernel(z_ref, xbc_ref, sm_ref, smt_ref, conv0_ref, h0_ref, cw_ref, cb_ref, prow_ref, pcol_ref, nw_ref,
                o_ref, conv_out_ref, h_out_ref, xp_ref, h_ref, y_ref, *, c, valid):
    n = pl.program_id(1)
    base = SUBLANES - (CONV_W - 1)

    @pl.when(n == 0)
    def _():
        xp_ref[base:SUBLANES, :] = conv0_ref[...]
        h_ref[...] = h0_ref[...]

    act = _conv_silu(xp_ref, xbc_ref, cw_ref, cb_ref[...], c)
    tail = xp_ref[SUBLANES + valid - (CONV_W - 1):SUBLANES + valid, :]
    conv_out_ref[...] = tail
    xp_ref[base:SUBLANES, :] = tail

    row_ok = _iota2((c, 1), 0) < valid
    col_ok = _iota2((1, c), 1) < valid
    dt_all = _softplus(sm_ref[...] + prow_ref[0:1, :])
    la_all = jnp.where(row_ok, dt_all * -jnp.exp(prow_ref[1:2, :]), 0.0)
    dt_t = _softplus(smt_ref[SM_DT:SM_DT + SSM_HEADS, :] + pcol_ref[:, 0:1])
    la_t = jnp.where(col_ok, dt_t * -jnp.exp(pcol_ref[:, 1:2]), 0.0)
    row = _iota2((c, c), 0)
    col = _iota2((c, c), 1)
    lc_all = _dot3(jnp.where(row >= col, 1.0, 0.0).astype(F32), la_all)
    lc_t = _dot3(la_t, jnp.where(row <= col, 1.0, 0.0).astype(F32))
    causal = row >= col
    rep = SSM_HEADS // SSM_GROUPS

    for g in range(SSM_GROUPS):
        bg = act[:, SSM_INNER + g * SSM_N:SSM_INNER + (g + 1) * SSM_N]
        cg = act[:, SSM_INNER + (SSM_GROUPS + g) * SSM_N:SSM_INNER + (SSM_GROUPS + g + 1) * SSM_N]
        bg = jnp.where(row_ok, bg, 0.0)
        cbg = _bdot(cg, bg, _dot_nt)
        for hh in range(rep):
            h = g * rep + hh
            sl = slice(h * SSM_P, (h + 1) * SSM_P)
            xh = jnp.where(row_ok, act[:, sl], 0.0)
            dt_c = dt_all[:, SM_DT + h:SM_DT + h + 1]
            lc_c = lc_all[:, SM_DT + h:SM_DT + h + 1]
            lc_r = lc_t[h:h + 1, :]
            xd = xh * dt_c
            decay = jnp.exp(jnp.where(causal, lc_c - lc_r, -jnp.inf))
            st = h_ref[h]
            y = _bdot(cbg * decay, xd) + _bdot(cg * jnp.exp(lc_c), st, _dot_nt)
            last = lc_all[c - 1:c, SM_DT + h:SM_DT + h + 1]
            st_new = st * jnp.exp(last) + _bdot(xd, bg * jnp.exp(last - lc_c), _dot_tn)
            h_ref[h] = st_new
            h_out_ref[h] = st_new
            y = y + act[:, sl] * prow_ref[2:3, SM_DT + h:SM_DT + h + 1]
            y_ref[:, sl] = y * _silu(z_ref[:, sl])

    gw = SSM_INNER // SSM_GROUPS
    for g in range(SSM_GROUPS):
        yg = y_ref[:, g * gw:(g + 1) * gw]
        yg = yg * lax.rsqrt(jnp.mean(yg * yg, -1, keepdims=True) + NORM_EPS) * nw_ref[:, g * gw:(g + 1) * gw]
        o_ref[:, g * gw:(g + 1) * gw] = yg.astype(o_ref.dtype)


def _ssd(z, xbc, sm, smt, conv0, h0, conv_w, conv_b, prow, pcol, norm_w, *, c, valid, cb_z, cb_xbc, out_dtype):
    b, lp = z.shape[0], z.shape[1]
    nch = lp // c
    kern = functools.partial(_ssd_kernel, c=c, valid=valid)
    return pl.pallas_call(
        kern,
        grid=(b, nch),
        in_specs=[pl.BlockSpec((None, c, SSM_INNER), lambda i, n: (i, n, cb_z)),
                  pl.BlockSpec((None, c, SSM_CONV_DIM), lambda i, n: (i, n, cb_xbc)),
                  pl.BlockSpec((None, c, SMALL_W), lambda i, n: (i, n, 0)),
                  pl.BlockSpec((None, None, 32, c), lambda i, n: (i, n, 0, 0)),
                  pl.BlockSpec((None, CONV_W - 1, SSM_CONV_DIM), lambda i, n: (i, 0, 0)),
                  pl.BlockSpec((None, SSM_HEADS, SSM_P, SSM_N), lambda i, n: (i, 0, 0, 0)),
                  pl.BlockSpec((CONV_W, SSM_CONV_DIM), lambda i, n: (0, 0)),
                  pl.BlockSpec((1, SSM_CONV_DIM), lambda i, n: (0, 0)),
                  pl.BlockSpec((3, SMALL_W), lambda i, n: (0, 0)),
                  pl.BlockSpec((SSM_HEADS, 2), lambda i, n: (0, 0)),
                  pl.BlockSpec((1, SSM_INNER), lambda i, n: (0, 0))],
        out_specs=[pl.BlockSpec((None, c, SSM_INNER), lambda i, n: (i, n, 0)),
                   pl.BlockSpec((None, CONV_W - 1, SSM_CONV_DIM), lambda i, n: (i, 0, 0)),
                   pl.BlockSpec((None, SSM_HEADS, SSM_P, SSM_N), lambda i, n: (i, 0, 0, 0))],
        out_shape=[jax.ShapeDtypeStruct((b, lp, SSM_INNER), out_dtype),
                   jax.ShapeDtypeStruct((b, CONV_W - 1, SSM_CONV_DIM), F32),
                   jax.ShapeDtypeStruct((b, SSM_HEADS, SSM_P, SSM_N), F32)],
        scratch_shapes=[pltpu.VMEM((c + SUBLANES, SSM_CONV_DIM), F32),
                        pltpu.VMEM((SSM_HEADS, SSM_P, SSM_N), F32),
                        pltpu.VMEM((c, SSM_INNER), F32)],
        compiler_params=_cparams(("arbitrary", "arbitrary")),
        name="ssd_scan",
    )(z, xbc, sm, smt, conv0, h0, conv_w, conv_b, prow, pcol, norm_w)


def _rope_tables(pos, heads):
    half = ROPE_DIM // 2
    inv_freq = ROPE_THETA ** (-jnp.arange(half, dtype=F32) / half)
    ang = pos.astype(F32)[:, None] * inv_freq[None, :]
    cos, sin = jnp.cos(ang), jnp.sin(ang)
    n = pos.shape[0]
    pad = jnp.zeros((n, SWA_D - ROPE_DIM), F32)
    c_h = jnp.concatenate([cos, cos, pad + 1.0], -1)
    a_h = jnp.concatenate([-sin, jnp.zeros_like(sin), pad], -1)
    b_h = jnp.concatenate([jnp.zeros_like(sin), sin, pad], -1)
    return tuple(jnp.tile(t, (1, heads)) for t in (c_h, a_h, b_h))


def _rope(x, tc, ta, tb):
    half = ROPE_DIM // 2
    w = x.shape[-1]
    return x * tc + pltpu.roll(x, w - half, 1) * ta + pltpu.roll(x, half, 1) * tb


def _sink_attend(q, k, v, valid, sink):
    s = _bdot(q, k, _dot_nt) * (SWA_D ** -0.5)
    s = jnp.where(valid, s, -jnp.inf)
    m = jnp.maximum(jnp.max(s, -1, keepdims=True), sink)
    p = jnp.exp(s - m)
    den = jnp.sum(p, -1, keepdims=True) + jnp.exp(sink - m)
    return _bdot(p / den, v)


def _swa_prompt_kernel(q_ref, kp_ref, kc_ref, vp_ref, vc_ref, qt_ref, ktp_ref, ktc_ref, sink_ref,
                       o_ref, ko_ref, vo_ref):
    i = pl.program_id(1)
    w = WINDOW
    q = _rope(q_ref[...], qt_ref[0], qt_ref[1], qt_ref[2])
    kc = _rope(kc_ref[...], ktc_ref[0], ktc_ref[1], ktc_ref[2])
    kp = _rope(kp_ref[...], ktp_ref[0], ktp_ref[1], ktp_ref[2])
    vc = vc_ref[...]
    ko_ref[...] = kc
    vo_ref[...] = vc
    kk = jnp.concatenate([kp, kc], 0)
    vv = jnp.concatenate([vp_ref[...], vc], 0)
    qi = _iota2((w, 2 * w), 0) + w
    kj = _iota2((w, 2 * w), 1)
    valid = (kj <= qi) & (qi - kj < w) & ((kj >= w) | (i > 0))
    valid = jnp.concatenate([valid] * SWA_GRP, 0)
    for g in range(SWA_KV):
        kg = kk[:, g * SWA_D:(g + 1) * SWA_D]
        vg = vv[:, g * SWA_D:(g + 1) * SWA_D]
        qg = jnp.concatenate([q[:, (g * SWA_GRP + j) * SWA_D:(g * SWA_GRP + j + 1) * SWA_D]
                              for j in range(SWA_GRP)], 0)
        sk = jnp.concatenate([jnp.broadcast_to(sink_ref[0:1, g * SWA_GRP + j:g * SWA_GRP + j + 1], (w, 1))
                              for j in range(SWA_GRP)], 0)
        og = _sink_attend(qg, kg, vg, valid, sk)
        for j in range(SWA_GRP):
            hd = g * SWA_GRP + j
            o_ref[:, hd * SWA_D:(hd + 1) * SWA_D] = og[j * w:(j + 1) * w].astype(o_ref.dtype)


def _swa_prompt(proj, qt, kt, sinks):
    b, l = proj.shape[0], proj.shape[1]
    w = WINDOW
    nb = l // w
    kvw = SWA_KV * SWA_D
    qw = SWA_HEADS * SWA_D
    prev = lambda i, n: (i, jnp.maximum(n - 1, 0), COL_SWK // kvw)
    prev_v = lambda i, n: (i, jnp.maximum(n - 1, 0), COL_SWV // kvw)
    return pl.pallas_call(
        _swa_prompt_kernel,
        grid=(b, nb),
        in_specs=[pl.BlockSpec((None, w, qw), lambda i, n: (i, n, COL_SWQ // qw)),
                  pl.BlockSpec((None, w, kvw), prev),
                  pl.BlockSpec((None, w, kvw), lambda i, n: (i, n, COL_SWK // kvw)),
                  pl.BlockSpec((None, w, kvw), prev_v),
                  pl.BlockSpec((None, w, kvw), lambda i, n: (i, n, COL_SWV // kvw)),
                  pl.BlockSpec((3, w, qw), lambda i, n: (0, n, 0)),
                  pl.BlockSpec((3, w, kvw), lambda i, n: (0, jnp.maximum(n - 1, 0), 0)),
                  pl.BlockSpec((3, w, kvw), lambda i, n: (0, n, 0)),
                  pl.BlockSpec((1, SWA_HEADS), lambda i, n: (0, 0))],
        out_specs=[pl.BlockSpec((None, w, qw), lambda i, n: (i, n, 0)),
                   pl.BlockSpec((None, w, kvw), lambda i, n: (i, 0, 0)),
                   pl.BlockSpec((None, w, kvw), lambda i, n: (i, 0, 0))],
        out_shape=[jax.ShapeDtypeStruct((b, l, qw), BF16),
                   jax.ShapeDtypeStruct((b, w, kvw), F32),
                   jax.ShapeDtypeStruct((b, w, kvw), F32)],
        compiler_params=_cparams(("arbitrary", "arbitrary")),
        name="swa_prompt",
    )(proj, proj, proj, proj, proj, qt, kt, kt, sinks)


def _swa_sample_kernel(q_ref, k_ref, v_ref, kb_ref, vb_ref, qt_ref, kt_ref, sink_ref,
                       o_ref, ko_ref, vo_ref, kk_ref, vv_ref, *, l):
    w = WINDOW
    lp = SUBLANES
    q = _rope(q_ref[...], qt_ref[0], qt_ref[1], qt_ref[2])
    k = _rope(k_ref[...], kt_ref[0], kt_ref[1], kt_ref[2])
    kk_ref[0:w, :] = kb_ref[...]
    kk_ref[w:w + lp, :] = k
    vv_ref[0:w, :] = vb_ref[...]
    vv_ref[w:w + lp, :] = v_ref[...]
    ko_ref[...] = kk_ref[l:l + w, :]
    vo_ref[...] = vv_ref[l:l + w, :]
    kk = kk_ref[...]
    vv = vv_ref[...]
    qi = _iota2((lp, w + lp), 0) + w
    kj = _iota2((lp, w + lp), 1)
    valid = (kj <= qi) & (qi - kj < w) & (kj < w + l)
    valid = jnp.concatenate([valid] * SWA_GRP, 0)
    for g in range(SWA_KV):
        kg = kk[:, g * SWA_D:(g + 1) * SWA_D]
        vg = vv[:, g * SWA_D:(g + 1) * SWA_D]
        qg = jnp.concatenate([q[:, (g * SWA_GRP + j) * SWA_D:(g * SWA_GRP + j + 1) * SWA_D]
                              for j in range(SWA_GRP)], 0)
        sk = jnp.concatenate([jnp.broadcast_to(sink_ref[0:1, g * SWA_GRP + j:g * SWA_GRP + j + 1], (lp, 1))
                              for j in range(SWA_GRP)], 0)
        og = _sink_attend(qg, kg, vg, valid, sk)
        for j in range(SWA_GRP):
            hd = g * SWA_GRP + j
            o_ref[:, hd * SWA_D:(hd + 1) * SWA_D] = og[j * lp:(j + 1) * lp]


def _swa_sample(q, k, v, kbuf, vbuf, qt, kt, sinks, l):
    b = q.shape[0]
    w = WINDOW
    lp = SUBLANES
    kvw = SWA_KV * SWA_D
    qw = SWA_HEADS * SWA_D
    kern = functools.partial(_swa_sample_kernel, l=l)
    return pl.pallas_call(
        kern,
        grid=(b,),
        in_specs=[pl.BlockSpec((None, lp, qw), lambda i: (i, 0, 0)),
                  pl.BlockSpec((None, lp, kvw), lambda i: (i, 0, 0)),
                  pl.BlockSpec((None, lp, kvw), lambda i: (i, 0, 0)),
                  pl.BlockSpec((None, w, kvw), lambda i: (i, 0, 0)),
                  pl.BlockSpec((None, w, kvw), lambda i: (i, 0, 0)),
                  pl.BlockSpec((3, lp, qw), lambda i: (0, 0, 0)),
                  pl.BlockSpec((3, lp, kvw), lambda i: (0, 0, 0)),
                  pl.BlockSpec((1, SWA_HEADS), lambda i: (0, 0))],
        out_specs=[pl.BlockSpec((None, lp, qw), lambda i: (i, 0, 0)),
                   pl.BlockSpec((None, w, kvw), lambda i: (i, 0, 0)),
                   pl.BlockSpec((None, w, kvw), lambda i: (i, 0, 0))],
        out_shape=[jax.ShapeDtypeStruct((b, lp, qw), F32),
                   jax.ShapeDtypeStruct((b, w, kvw), F32),
                   jax.ShapeDtypeStruct((b, w, kvw), F32)],
        scratch_shapes=[pltpu.VMEM((w + lp, kvw), F32), pltpu.VMEM((w + lp, kvw), F32)],
        compiler_params=_cparams(("arbitrary",)),
        name="swa_sample",
    )(q, k, v, kbuf, vbuf, qt, kt, sinks)


def _merge_kernel(oa_ref, ob_ref, oc_ref, ga_ref, gb_ref, gc_ref, wa_ref, wb_ref, wc_ref, o_ref):
    acc = _sigmoid(ga_ref[...]) * _bdot(oa_ref[...], wa_ref[...])
    acc = acc + _sigmoid(gb_ref[...]) * _bdot(ob_ref[...], wb_ref[...])
    acc = acc + _sigmoid(gc_ref[...]) * _bdot(oc_ref[...], wc_ref[...])
    o_ref[...] = acc.astype(o_ref.dtype)


def _merge(oa, ob, oc, proj, wbr, tm):
    t = oa.shape[0]
    tn = 512
    nj = D_MODEL // tn
    gate = lambda k: pl.BlockSpec((tm, tn), lambda i, j: (i, COL_MG // tn + k * nj + j))
    wsp = lambda k: pl.BlockSpec((None, BRANCH_W, tn), lambda i, j: (k, 0, j))
    osp = pl.BlockSpec((tm, BRANCH_W), lambda i, j: (i, 0))
    return pl.pallas_call(
        _merge_kernel,
        grid=(t // tm, nj),
        in_specs=[osp, osp, osp, gate(0), gate(1), gate(2), wsp(0), wsp(1), wsp(2)],
        out_specs=pl.BlockSpec((tm, tn), lambda i, j: (i, j)),
        out_shape=jax.ShapeDtypeStruct((t, D_MODEL), BF16),
        compiler_params=_cparams(("arbitrary", "arbitrary")),
        name="merge",
    )(oa, ob, oc, proj, proj, proj, wbr, wbr, wbr)


def _layer_norm(r, g, b):
    mu = jnp.mean(r, -1, keepdims=True)
    rc = r - mu
    var = jnp.mean(rc * rc, -1, keepdims=True)
    return rc * lax.rsqrt(var + LN_EPS) * g + b


def _to_slab(ref, val):
    rows = val.shape[0]
    for s in range(SLAB):
        ref[pl.ds(s, rows, stride=SLAB), :] = val[:, s * LANES:(s + 1) * LANES]


def _from_slab(ref, lead, start, rows, dtype):
    parts = []
    for s in range(SLAB):
        idx = (pl.ds(start * SLAB + s, rows, stride=SLAB), slice(None))
        parts.append(ref[(lead,) + idx if lead is not None else idx].astype(dtype))
    return jnp.concatenate(parts, axis=1)


def _out_ln_kernel(m_ref, w_ref, x_ref, g1_ref, sc_ref, sh_ref, lg_ref, lb_ref, *rest, alpha):
    x1_ref, h2_ref, hs_ref = rest[-3:]
    mix = _dot(m_ref[...], w_ref[...])
    x1 = _layer_norm(alpha * x_ref[...] + g1_ref[...] * mix, lg_ref[...], lb_ref[...])
    x1_ref[...] = x1
    h2 = x1 * (1.0 + sc_ref[...]) + sh_ref[...]
    h2_ref[...] = h2
    _to_slab(hs_ref, h2)


def _out_ln(merged, w_out, x, blk0, t_all, bases, g1, sc2, sh2, ln_g, ln_b, tm, tiles_per_group, alpha):
    t, d = merged.shape
    row = pl.BlockSpec((tm, d), lambda i: (i, 0))
    orow = pl.BlockSpec((tm, d), lambda i: (i + blk0, 0))
    vec = pl.BlockSpec((1, d), lambda i: (0, 0))
    kern = functools.partial(_out_ln_kernel, alpha=alpha)
    args = [merged, w_out, x, g1, sc2, sh2, ln_g, ln_b]
    in_specs = [row, pl.BlockSpec((d, d), lambda i: (0, 0)), orow,
                _mod_spec(g1, tm, tiles_per_group), _mod_spec(sc2, tm, tiles_per_group),
                _mod_spec(sh2, tm, tiles_per_group), vec, vec]
    aliases = {}
    if bases is not None:
        aliases = {len(args) + k: k for k in range(len(bases))}
        args += list(bases)
        in_specs += [pl.BlockSpec(memory_space=pl.ANY)] * len(bases)
    return pl.pallas_call(
        kern,
        grid=(t // tm,),
        in_specs=in_specs,
        out_specs=[orow, orow, pl.BlockSpec((tm * SLAB, LANES), lambda i: (i + blk0, 0))],
        out_shape=[jax.ShapeDtypeStruct((t_all, d), F32), jax.ShapeDtypeStruct((t_all, d), F32),
                   jax.ShapeDtypeStruct((t_all * SLAB, LANES), F32)],
        input_output_aliases=aliases,
        compiler_params=_cparams(("arbitrary",)),
        name="out_ln",
    )(*args)


ROUTER_TR = 640


def _router_kernel(h_ref, w_ref, b_ref, eidx_ref, rank_ref, wgt_ref, cnt_ref, carry_ref):
    i = pl.program_id(0)
    tr = h_ref.shape[0]
    ng, gs = N_GROUPS, GROUP_SIZE

    @pl.when(i == 0)
    def _():
        carry_ref[...] = jnp.zeros_like(carry_ref)

    logits = _dot3(w_ref[...], h_ref[...], _dot_nt)
    scores = _sigmoid(logits)
    sc3 = scores.reshape(ng, gs, tr)
    ch3 = (scores + b_ref[...]).reshape(ng, gs, tr)
    io_e = _iota2((ng, gs, tr), 1)
    io_g = _iota2((ng, 1, tr), 0)
    io_x = _iota2((ng, gs, tr), 0) * gs + io_e
    ninf = -jnp.inf

    m1 = jnp.max(ch3, 1, keepdims=True)
    i1 = jnp.min(jnp.where(ch3 == m1, io_e, gs), 1, keepdims=True)
    m2 = jnp.max(jnp.where(io_e == i1, ninf, ch3), 1, keepdims=True)
    grp = m1 + m2
    keep = jnp.zeros((ng, 1, tr), jnp.bool_)
    for _ in range(TOPK_GROUPS):
        m = jnp.max(grp, 0, keepdims=True)
        first = jnp.min(jnp.where(grp == m, io_g, ng), 0, keepdims=True)
        hit = io_g == first
        keep = keep | hit
        grp = jnp.where(hit, ninf, grp)

    cm = jnp.where(keep, ch3, ninf)
    hits = []
    firsts = []
    for _ in range(TOP_K):
        m = jnp.max(jnp.max(cm, 1, keepdims=True), 0, keepdims=True)
        cand = jnp.where(cm == m, io_x, N_EXPERTS)
        first = jnp.min(jnp.min(cand, 1, keepdims=True), 0, keepdims=True)
        hit = io_x == first
        hits.append(hit)
        firsts.append(first)
        cm = jnp.where(hit, ninf, cm)
    sel = hits[0]
    for hit in hits[1:]:
        sel = sel | hit
    self32 = jnp.where(sel, 1.0, 0.0).astype(F32)
    wsel = sc3 * self32
    den = jnp.sum(jnp.sum(wsel, 1, keepdims=True), 0, keepdims=True)
    comb = wsel / den * ROUTED_SCALE

    sel2 = self32.reshape(N_EXPERTS, tr)
    upper = jnp.where(_iota2((tr, tr), 0) <= _iota2((tr, tr), 1), 1.0, 0.0).astype(BF16)
    incl = _dot(sel2.astype(BF16), upper)
    carry = carry_ref[:, 0:1]
    rank3 = (carry + incl - sel2).reshape(ng, gs, tr)
    for r in range(TOP_K):
        hf = jnp.where(hits[r], 1.0, 0.0).astype(F32)
        rk = jnp.sum(jnp.sum(hf * rank3, 1, keepdims=True), 0, keepdims=True)
        wg = jnp.sum(jnp.sum(hf * comb, 1, keepdims=True), 0, keepdims=True)
        eidx_ref[r:r + 1, :] = firsts[r].reshape(1, tr)
        rank_ref[r:r + 1, :] = rk.reshape(1, tr).astype(I32)
        wgt_ref[r:r + 1, :] = wg.reshape(1, tr)
    new_carry = carry + incl[:, tr - 1:tr]
    carry_ref[...] = jnp.broadcast_to(new_carry, carry_ref.shape)
    cnt_ref[...] = jnp.broadcast_to(new_carry, cnt_ref.shape).astype(I32)


def _router(h2, wr_t, bias_col):
    t, d = h2.shape
    tr = ROUTER_TR
    out = pl.BlockSpec((TOP_K, tr), lambda i: (0, i))
    return pl.pallas_call(
        _router_kernel,
        grid=(t // tr,),
        in_specs=[pl.BlockSpec((tr, d), lambda i: (i, 0)),
                  pl.BlockSpec((N_EXPERTS, d), lambda i: (0, 0)),
                  pl.BlockSpec((N_EXPERTS, 1), lambda i: (0, 0))],
        out_specs=[out, out, out, pl.BlockSpec((N_EXPERTS, LANES), lambda i: (0, 0))],
        out_shape=[jax.ShapeDtypeStruct((TOP_K, t), I32), jax.ShapeDtypeStruct((TOP_K, t), I32),
                   jax.ShapeDtypeStruct((TOP_K, t), F32), jax.ShapeDtypeStruct((N_EXPERTS, LANES), I32)],
        scratch_shapes=[pltpu.VMEM((N_EXPERTS, LANES), F32)],
        compiler_params=_cparams(("arbitrary",)),
        name="router",
    )(h2, wr_t, bias_col)


def _invert_kernel(slot_ref, tok_ref):
    def init(r, carry):
        for u in range(LANES):
            tok_ref[r, u] = 0
        return carry
    lax.fori_loop(0, tok_ref.shape[0], init, 0)

    def body(t, carry):
        rows = [slot_ref[k, t] for k in range(TOP_K)]
        for s in rows:
            tok_ref[lax.shift_right_logical(s, LANE_BITS), s & (LANES - 1)] = t
        return carry
    lax.fori_loop(0, slot_ref.shape[1], body, 0)


def _invert(slots, n_rows):
    return pl.pallas_call(
        _invert_kernel,
        grid_spec=pltpu.PrefetchScalarGridSpec(
            num_scalar_prefetch=1,
            grid=(1,),
            in_specs=[],
            out_specs=pl.BlockSpec(memory_space=pltpu.SMEM)),
        out_shape=jax.ShapeDtypeStruct((n_rows // LANES, LANES), I32),
        compiler_params=_cparams(("arbitrary",)),
        name="invert_slots",
    )(slots)


def _cast_weights(te_ref, wg_ref, wu_ref, wd_ref, wgb_ref, wub_ref, wdb_ref):
    i = pl.program_id(0)
    prev = te_ref[jnp.maximum(i - 1, 0)]

    @pl.when((i == 0) | (te_ref[i] != prev))
    def _():
        wgb_ref[...] = wg_ref[...].astype(BF16)
        wub_ref[...] = wu_ref[...].astype(BF16)
        wdb_ref[...] = wd_ref[...].astype(BF16)


def _swiglu(xb, wgb_ref, wub_ref, wdb_ref):
    a = _dot(xb, wgb_ref[...])
    u = _dot(xb, wub_ref[...])
    return _dot((_silu(a) * u).astype(BF16), wdb_ref[...])


def _ffn_kernel(te_ref, na_ref, x_ref, wg_ref, wu_ref, wd_ref, y_ref, wgb_ref, wub_ref, wdb_ref):
    _cast_weights(te_ref, wg_ref, wu_ref, wd_ref, wgb_ref, wub_ref, wdb_ref)
    y_ref[...] = _swiglu(x_ref[...].astype(BF16), wgb_ref, wub_ref, wdb_ref)


def _slab_copy(src_ref, row, dst_ref, buf, slot, sem):
    src = src_ref.at[pl.ds(pl.multiple_of(row * SLAB, SLAB), SLAB)]
    dst = dst_ref.at[buf, pl.ds(pl.multiple_of(slot * SLAB, SLAB), SLAB)]
    return pltpu.make_async_copy(src, dst, sem.at[buf])


def _gffn_kernel(te_ref, na_ref, tok_ref, h_ref, wg_ref, wu_ref, wd_ref, y_ref,
                 wgb_ref, wub_ref, wdb_ref, xbuf_ref, sem, *, tm):
    i = pl.program_id(0)
    na = na_ref[0]
    unroll = SUBLANES
    tok_rows = tm // LANES

    def issue(tile, b):
        for q in range(tok_rows):
            def body(g, carry):
                for u in range(unroll):
                    c = g * unroll + u
                    _slab_copy(h_ref, tok_ref[tile * tok_rows + q, c], xbuf_ref, b, q * LANES + c, sem).start()
                return carry
            lax.fori_loop(0, LANES // unroll, body, 0)

    @pl.when(i == 0)
    def _():
        issue(0, 0)

    @pl.when(i + 1 < na)
    def _():
        issue(i + 1, (i + 1) % 2)

    _cast_weights(te_ref, wg_ref, wu_ref, wd_ref, wgb_ref, wub_ref, wdb_ref)

    @pl.when(i < na)
    def _():
        b = i % 2

        def wbody(j, carry):
            for u in range(unroll):
                _slab_copy(h_ref, 0, xbuf_ref, b, 0, sem).wait()
            return carry
        lax.fori_loop(0, tm // unroll, wbody, 0)
        xb = _from_slab(xbuf_ref, b, 0, tm, BF16)
        _to_slab(y_ref, _swiglu(xb, wgb_ref, wub_ref, wdb_ref))

    @pl.when(i >= na)
    def _():
        y_ref[...] = jnp.zeros_like(y_ref)


def _wspecs(layer, d, ff):
    wmap = lambda i, te, *_: (layer, te[i], 0, 0)
    return [pl.BlockSpec((None, None, d, ff), wmap), pl.BlockSpec((None, None, d, ff), wmap),
            pl.BlockSpec((None, None, ff, d), wmap)]


def _wscratch(d, ff):
    return [pltpu.VMEM((d, ff), BF16), pltpu.VMEM((d, ff), BF16), pltpu.VMEM((ff, d), BF16)]


def _ffn(te, na, x, wg, wu, wd, layer, tm):
    t, d = x.shape
    ff = wg.shape[-1]
    return pl.pallas_call(
        _ffn_kernel,
        grid_spec=pltpu.PrefetchScalarGridSpec(
            num_scalar_prefetch=2,
            grid=(t // tm,),
            in_specs=[pl.BlockSpec((tm, d), lambda i, *_: (i, 0))] + _wspecs(layer, d, ff),
            out_specs=pl.BlockSpec((tm, d), lambda i, *_: (i, 0)),
            scratch_shapes=_wscratch(d, ff)),
        out_shape=jax.ShapeDtypeStruct((t, d), F32),
        compiler_params=_cparams(("arbitrary",)),
        name="shared_ffn",
    )(te, na, x, wg, wu, wd)


def _gffn(te, na, tok, h2s, wg, wu, wd, layer, tm):
    d = D_MODEL
    ff = wg.shape[-1]
    n_rows = tok.shape[0] * tok.shape[1]
    assert tm % LANES == 0
    kern = functools.partial(_gffn_kernel, tm=tm)
    return pl.pallas_call(
        kern,
        grid_spec=pltpu.PrefetchScalarGridSpec(
            num_scalar_prefetch=3,
            grid=(n_rows // tm,),
            in_specs=[pl.BlockSpec(memory_space=pl.ANY)] + _wspecs(layer, d, ff),
            out_specs=pl.BlockSpec((tm * SLAB, LANES), lambda i, *_: (i, 0)),
            scratch_shapes=_wscratch(d, ff) + [pltpu.VMEM((2, tm * SLAB, LANES), F32),
                                               pltpu.SemaphoreType.DMA((2,))]),
        out_shape=jax.ShapeDtypeStruct((n_rows * SLAB, LANES), F32),
        compiler_params=_cparams(("arbitrary",)),
        name="expert_ffn",
    )(te, na, tok, h2s, wg, wu, wd)


COMBINE_TC = 32


def _combine_kernel(slot_ref, ys_ref, w_ref, sh_ref, x_ref, g2_ref, lg_ref, lb_ref, o_ref, buf_ref, g2x_ref, sem,
                    *, alpha):
    i = pl.program_id(0)
    n = pl.num_programs(0)
    tc = COMBINE_TC

    def issue(tile, b):
        def body(j, carry):
            for k in range(TOP_K):
                _slab_copy(ys_ref, slot_ref[k, tile * tc + j], buf_ref, b, k * tc + j, sem).start()
            return carry
        lax.fori_loop(0, tc, body, 0)

    @pl.when(i == 0)
    def _():
        issue(0, 0)

    @pl.when(i + 1 < n)
    def _():
        issue(i + 1, (i + 1) % 2)

    b = i % 2

    def wbody(j, carry):
        for k in range(TOP_K):
            _slab_copy(ys_ref, 0, buf_ref, b, 0, sem).wait()
        return carry
    lax.fori_loop(0, tc, wbody, 0)

    for s in range(SLAB):
        for u in range(MOD_GROUP):
            g2x_ref[s, pl.ds(u, tc // MOD_GROUP, stride=MOD_GROUP), :] = g2_ref[:, s * LANES:(s + 1) * LANES]
    g2 = jnp.concatenate([g2x_ref[s] for s in range(SLAB)], axis=1)

    w = w_ref[...]
    acc = sh_ref[...]
    for k in range(TOP_K):
        acc = acc + w[:, k:k + 1] * _from_slab(buf_ref, b, k * tc, tc, F32)
    o_ref[...] = _layer_norm(alpha * x_ref[...] + g2 * acc, lg_ref[...], lb_ref[...])


def _combine(slots, ys, wgt, shared, x1, g2grp, ln_g, ln_b, alpha):
    t, d = x1.shape
    tc = COMBINE_TC
    row = pl.BlockSpec((tc, d), lambda i, *_: (i, 0))
    vec = pl.BlockSpec((1, d), lambda i, *_: (0, 0))
    kern = functools.partial(_combine_kernel, alpha=alpha)
    return pl.pallas_call(
        kern,
        grid_spec=pltpu.PrefetchScalarGridSpec(
            num_scalar_prefetch=1,
            grid=(t // tc,),
            in_specs=[pl.BlockSpec(memory_space=pl.ANY),
                      pl.BlockSpec((tc, TOP_K), lambda i, *_: (i, 0)),
                      row, row, pl.BlockSpec((tc // MOD_GROUP, d), lambda i, *_: (i, 0)), vec, vec],
            out_specs=row,
            scratch_shapes=[pltpu.VMEM((2, TOP_K * tc * SLAB, LANES), F32), pltpu.VMEM((SLAB, tc, LANES), F32),
                            pltpu.SemaphoreType.DMA((2,))]),
        out_shape=jax.ShapeDtypeStruct((t, d), F32),
        compiler_params=_cparams(("arbitrary",)),
        name="combine_ln",
    )(slots, ys, wgt, shared, x1, g2grp, ln_g, ln_b)


def _moe(h2, h2s, x1, g2grp, p, layer, alpha):
    t, d = h2.shape
    tm = MOE_TM
    eidx, rank, wgt, cnt = _router(h2, p['w_router'].T, p['router_bias'].reshape(N_EXPERTS, 1))
    counts = cnt[:, 0]
    tiles = (counts + tm - 1) // tm
    tile_end = jnp.cumsum(tiles)
    offs = ((tile_end - tiles) * tm).astype(I32)
    n_active = tile_end[-1]
    n_tiles = (t * TOP_K + N_EXPERTS * (tm - 1)) // tm
    ids = jnp.minimum(jnp.arange(n_tiles, dtype=I32), n_active - 1)
    te = jnp.sum((tile_end[None, :] <= ids[:, None]).astype(I32), axis=1)
    onehot = eidx[..., None] == jnp.arange(N_EXPERTS, dtype=I32)
    slots = rank + jnp.sum(jnp.where(onehot, offs, 0), -1)
    tok = _invert(slots, n_tiles * tm)
    ys = _gffn(te, n_active.reshape(1).astype(I32), tok, h2s, p['w_exp_gate'], p['w_exp_up'], p['w_exp_down'],
               layer, tm)
    tsh = 640
    nsh = t // tsh
    sh4 = lambda w: w.reshape((w.shape[0], 1) + w.shape[1:])
    shared = _ffn(jnp.zeros((nsh,), I32), jnp.full((1,), nsh, I32), h2,
                  sh4(p['w_sh_gate']), sh4(p['w_sh_up']), sh4(p['w_sh_down']), layer, tsh)
    return _combine(slots, ys, wgt.T, shared, x1, g2grp, p['ln2_g'].reshape(1, d), p['ln2_b'].reshape(1, d), alpha)


def _chunk_t(sm, c):
    b, l, _ = sm.shape
    return jnp.swapaxes(sm.reshape(b, l // c, c, SMALL_W)[..., :32], -1, -2)


def _pad_rows(a, rows):
    return jnp.pad(a, ((0, 0), (0, rows - a.shape[1]), (0, 0)))


def _mixer_group(x_all, b, l, row0, mod, states, p, tabs, bases, *, prompt):
    d = x_all.shape[1]
    t = b * l
    dn_conv0, dn_s0, ssm_conv0, ssm_h0, kv_buf = states
    sh1, sc1, g1, sh2, sc2, g2 = jnp.split(mod, 6, axis=-1)
    if prompt:
        tm = 1024
        tpg = l // tm
        shape = lambda m: m.reshape(b, 1, d)
    else:
        tm = t
        tpg = 1
        shape = lambda m: jnp.repeat(m, l, axis=0).reshape(1, t, d)
    proj, small = _in_proj(x_all, t, row0 // tm, shape(sc1), shape(sh1), p['w_main'], p['w_small'], tm, tpg)
    proj3 = proj.reshape(b, l, MAIN_W)
    small3 = small.reshape(b, l, SMALL_W)
    if prompt:
        c = CHUNK
        dn_in = (proj3, proj3)
        dn_cb = dict(cb_qkv=COL_QKV // DN_CONV_DIM, cb_gate=COL_DNG // DN_KEY)
        ssd_in = (proj3, proj3)
        ssd_cb = dict(cb_z=COL_SSZ // SSM_INNER, cb_xbc=COL_XBC // SSM_CONV_DIM)
        sm_in = small3
        odt = BF16
    else:
        c = SUBLANES
        cut = lambda c0, w: _pad_rows(proj3[:, :, c0:c0 + w], c)
        dn_in = (cut(COL_QKV, DN_CONV_DIM), cut(COL_DNG, DN_KEY))
        dn_cb = dict(cb_qkv=0, cb_gate=0)
        ssd_in = (cut(COL_SSZ, SSM_INNER), cut(COL_XBC, SSM_CONV_DIM))
        ssd_cb = dict(cb_z=0, cb_xbc=0)
        sm_in = _pad_rows(small3, c)
        odt = F32
    valid = c if prompt else l
    smt = _chunk_t(sm_in, c)
    o_a, dn_conv, dn_s = _dn(dn_in[0], dn_in[1], sm_in, smt, dn_conv0, dn_s0, p['dn_conv_w'], p['dn_prow'],
                             p['dn_pcol'], p['dn_norm_w'].reshape(1, DN_D), c=c, valid=valid, out_dtype=odt, **dn_cb)
    o_b, ssm_conv, ssm_h = _ssd(ssd_in[0], ssd_in[1], sm_in, smt, ssm_conv0, ssm_h0, p['ssm_conv_w'],
                                p['ssm_conv_b'].reshape(1, SSM_CONV_DIM), p['ssm_prow'], p['ssm_pcol'],
                                p['ssm_norm_w'].reshape(1, SSM_INNER), c=c, valid=valid, out_dtype=odt, **ssd_cb)
    sinks = p['swa_sinks'].reshape(1, SWA_HEADS)
    if prompt:
        o_c, k_new, v_new = _swa_prompt(proj3, tabs[0], tabs[1], sinks)
    else:
        o_c, k_new, v_new = _swa_sample(cut(COL_SWQ, SWA_HEADS * SWA_D), cut(COL_SWK, SWA_KV * SWA_D),
                                        cut(COL_SWV, SWA_KV * SWA_D),
                                        kv_buf[0].reshape(b, WINDOW, SWA_KV * SWA_D),
                                        kv_buf[1].reshape(b, WINDOW, SWA_KV * SWA_D), tabs[0], tabs[1], sinks, l)
    flat = lambda o: o[:, :l].reshape(t, BRANCH_W).astype(BF16)
    merged = _merge(flat(o_a), flat(o_b), flat(o_c), proj, p['w_branch'], tm=min(tm, 512))
    tm2 = min(tm, 512)
    tpg2 = l // tm2 if prompt else 1
    shape2 = (lambda m: m.reshape(b, 1, d)) if prompt else shape
    bufs = _out_ln(merged, p['w_out'], x_all, row0 // tm2, x_all.shape[0], bases, shape2(g1), shape2(sc2),
                   shape2(sh2), p['ln1_g'].reshape(1, d), p['ln1_b'].reshape(1, d), tm2, tpg2, p['alpha'])
    g2grp = jnp.repeat(g2, l // MOD_GROUP, axis=0)
    k_new = k_new.reshape(b, WINDOW, SWA_KV, SWA_D)
    v_new = v_new.reshape(b, WINDOW, SWA_KV, SWA_D)
    return bufs, g2grp, (dn_conv, dn_s, ssm_conv, ssm_h, k_new, v_new)


def _forward(x_prompt, x_sample, state_dn_conv, state_dn, state_ssm_conv, state_ssm, cache_swa_k, cache_swa_v,
             c_prompt, c_sample, w_ada, b_ada, w_in, dn_conv_w, dn_a_log, dn_dt_bias, dn_norm_w,
             ssm_conv_w, ssm_conv_b, ssm_a_log, ssm_dt_bias, ssm_d, ssm_norm_w, swa_sinks, w_branch, w_out,
             ln1_g, ln1_b, w_router, router_bias, w_exp_gate, w_exp_up, w_exp_down, w_sh_gate, w_sh_up,
             w_sh_down, ln2_g, ln2_b):
    depth = w_in.shape[0]
    bp, lp, d = x_prompt.shape
    bs, ls, _ = x_sample.shape
    tp, ts = bp * lp, bs * ls
    alpha = (2 * depth) ** 0.25
    past_len = 16384

    c_all = jnp.concatenate([c_prompt, c_sample, jnp.zeros((4, d), F32)], 0)
    mod_all = _ada(c_all, w_ada, b_ada)
    tabs_p = tuple(jnp.stack(_rope_tables(jnp.arange(lp), h)) for h in (SWA_HEADS, SWA_KV))
    tabs_s = tuple(jnp.stack(_rope_tables(past_len + jnp.arange(SUBLANES), h)) for h in (SWA_HEADS, SWA_KV))
    init_p = (jnp.zeros((bp, CONV_W - 1, DN_CONV_DIM), F32), jnp.zeros((bp, DN_HEADS, DN_D, DN_D), F32),
              jnp.zeros((bp, CONV_W - 1, SSM_CONV_DIM), F32), jnp.zeros((bp, SSM_HEADS, SSM_P, SSM_N), F32), None)

    def pad_lanes(v, at):
        return jnp.zeros((SMALL_W,), F32).at[at:at + v.shape[0]].set(v)

    x_all = jnp.concatenate([x_prompt.reshape(tp, d), x_sample.reshape(ts, d)], 0)
    new_p, new_s = [], []
    for l in range(depth):
        w = w_in[l]
        seg = lambda a, n: w[:, a:a + n]
        w_main = jnp.concatenate([seg(8224, 6144), seg(0, 3072), seg(5136, 1536), seg(7712, 256), seg(7968, 256),
                                  seg(3072, 1024), seg(4112, 1024), seg(6688, 1024)], 1).astype(BF16)
        w_small = jnp.concatenate([seg(4096, 16), seg(6672, 16), jnp.zeros((d, SMALL_W - 32), F32)], 1).astype(BF16)
        p = {'w_main': w_main, 'w_small': w_small, 'alpha': alpha,
             'dn_conv_w': dn_conv_w[l], 'dn_norm_w': dn_norm_w[l],
             'dn_prow': jnp.stack([pad_lanes(dn_a_log[l], SM_A), pad_lanes(dn_dt_bias[l], SM_A)]),
             'dn_pcol': jnp.stack([dn_a_log[l], dn_dt_bias[l]], 1),
             'ssm_conv_w': ssm_conv_w[l], 'ssm_conv_b': ssm_conv_b[l], 'ssm_norm_w': ssm_norm_w[l],
             'ssm_prow': jnp.stack([pad_lanes(ssm_dt_bias[l], SM_DT), pad_lanes(ssm_a_log[l], SM_DT),
                                    pad_lanes(ssm_d[l], SM_DT)]),
             'ssm_pcol': jnp.stack([ssm_dt_bias[l], ssm_a_log[l]], 1),
             'swa_sinks': swa_sinks[l], 'w_branch': w_branch[l].astype(BF16), 'w_out': w_out[l].astype(BF16),
             'ln1_g': ln1_g[l], 'ln1_b': ln1_b[l], 'w_router': w_router[l], 'router_bias': router_bias[l],
             'w_exp_gate': w_exp_gate, 'w_exp_up': w_exp_up, 'w_exp_down': w_exp_down,
             'w_sh_gate': w_sh_gate, 'w_sh_up': w_sh_up, 'w_sh_down': w_sh_down,
             'ln2_g': ln2_g[l], 'ln2_b': ln2_b[l]}
        mod = mod_all[l]
        bufs, g2p, st_p = _mixer_group(x_all, bp, lp, 0, mod[:bp], init_p, p, tabs_p, None, prompt=True)
        st_in = (state_dn_conv[l], state_dn[l], state_ssm_conv[l], state_ssm[l], (cache_swa_k[l], cache_swa_v[l]))
        bufs, g2s, st_s = _mixer_group(x_all, bs, ls, tp, mod[bp:bp + bs], st_in, p, tabs_s, bufs, prompt=False)
        x1, h2, h2s = bufs
        x_all = _moe(h2, h2s, x1, jnp.concatenate([g2p, g2s], 0), p, l, alpha)
        new_p.append(st_p)
        new_s.append(st_s)
    outs = [x_all[:tp].reshape(bp, lp, d), x_all[tp:].reshape(bs, ls, d)]
    for k in range(6):
        outs.append(jnp.stack([s[k] for s in new_p]))
        outs.append(jnp.stack([s[k] for s in new_s]))
    return tuple(outs)


def kernel(x_prompt, x_sample, state_dn_conv, state_dn, state_ssm_conv, state_ssm, cache_swa_k, cache_swa_v, c_prompt, c_sample, w_ada, b_ada, w_in, dn_conv_w, dn_a_log, dn_dt_bias, dn_norm_w, ssm_conv_w, ssm_conv_b, ssm_a_log, ssm_dt_bias, ssm_d, ssm_norm_w, swa_sinks, w_branch, w_out, ln1_g, ln1_b, w_router, router_bias, w_exp_gate, w_exp_up, w_exp_down, w_sh_gate, w_sh_up, w_sh_down, ln2_g, ln2_b):
    return _forward(x_prompt, x_sample, state_dn_conv, state_dn, state_ssm_conv, state_ssm, cache_swa_k, cache_swa_v, c_prompt, c_sample, w_ada, b_ada, w_in, dn_conv_w, dn_a_log, dn_dt_bias, dn_norm_w, ssm_conv_w, ssm_conv_b, ssm_a_log, ssm_dt_bias, ssm_d, ssm_norm_w, swa_sinks, w_branch, w_out, ln1_g, ln1_b, w_router, router_bias, w_exp_gate, w_exp_up, w_exp_down, w_sh_gate, w_sh_up, w_sh_down, ln2_g, ln2_b)
```

```python
import functools
import math

import jax
import jax.numpy as jnp
from jax import lax
from jax.experimental import pallas as pl
from jax.experimental.pallas import tpu as pltpu

F32 = jnp.float32
BF16 = jnp.bfloat16
I32 = jnp.int32

D_MODEL = 2048
CONV_W = 4
DN_HEADS = 8
DN_D = 128
DN_KEY = DN_HEADS * DN_D
DN_CONV_DIM = 3 * DN_KEY
SSM_HEADS = 16
SSM_P = 64
SSM_INNER = SSM_HEADS * SSM_P
SSM_GROUPS = 2
SSM_N = 128
SSM_CONV_DIM = SSM_INNER + 2 * SSM_GROUPS * SSM_N
SWA_HEADS = 16
SWA_KV = 4
SWA_D = 64
SWA_GRP = SWA_HEADS // SWA_KV
WINDOW = 128
ROPE_DIM = SWA_D // 4
ROPE_THETA = 500000.0
N_BRANCH = 3
BRANCH_W = 1024
N_EXPERTS = 64
EXPERT_FF = 512
TOP_K = 8
N_GROUPS = 8
GROUP_SIZE = N_EXPERTS // N_GROUPS
TOPK_GROUPS = 4
ROUTED_SCALE = 2.5
LN_EPS = 1e-5
NORM_EPS = 1e-6
CHUNK = 64

LANES = 128
LANE_BITS = 7
SUBLANES = 8
VMEM_LIMIT = 56 * 1024 * 1024

MAIN_W = 14336
COL_MG, COL_QKV, COL_XBC, COL_SWK, COL_SWV, COL_DNG, COL_SSZ, COL_SWQ = (
    0, 6144, 9216, 10752, 11008, 11264, 12288, 13312)
SMALL_W = LANES
SM_A, SM_B, SM_DT = 0, 8, 16

MOE_TM = 256
SLAB = D_MODEL // LANES
MOD_GROUP = 4


def _cparams(sem, vmem=VMEM_LIMIT):
    return pltpu.CompilerParams(dimension_semantics=sem, vmem_limit_bytes=vmem)


def _dot(a, b):
    return jnp.dot(a, b, preferred_element_type=F32)


def _dot_nt(a, b):
    return lax.dot_general(a, b, (((1,), (1,)), ((), ())), preferred_element_type=F32)


def _dot_tn(a, b):
    return lax.dot_general(a, b, (((0,), (0,)), ((), ())), preferred_element_type=F32)


def _split(a):
    hi = a.astype(BF16)
    lo = (a - hi.astype(F32)).astype(BF16)
    return hi, lo


def _dot3(a, b, dot=_dot):
    ah, al = _split(a)
    bh, bl = _split(b)
    return dot(ah, bh) + dot(ah, bl) + dot(al, bh)


def _bdot(a, b, dot=_dot):
    return dot(a.astype(BF16), b.astype(BF16))


def _silu(x):
    return x * (1.0 / (1.0 + jnp.exp(-x)))


def _sigmoid(x):
    return 1.0 / (1.0 + jnp.exp(-x))


def _softplus(x):
    return jnp.maximum(x, 0.0) + jnp.log(1.0 + jnp.exp(-jnp.abs(x)))


def _ada_kernel(c_ref, w_ref, b_ref, o_ref):
    c = c_ref[...]
    o_ref[...] = _bdot(_silu(c), w_ref[...]) + b_ref[...]


def _ada(c_all, w_ada, b_ada):
    depth, d, n = w_ada.shape
    rows = c_all.shape[0]
    tn = 1024
    return pl.pallas_call(
        _ada_kernel,
        grid=(depth, n // tn),
        in_specs=[pl.BlockSpec((rows, d), lambda l, j: (0, 0)),
                  pl.BlockSpec((None, d, tn), lambda l, j: (l, 0, j)),
                  pl.BlockSpec((None, 1, tn), lambda l, j: (l, 0, j))],
        out_specs=pl.BlockSpec((None, rows, tn), lambda l, j: (l, 0, j)),
        out_shape=jax.ShapeDtypeStruct((depth, rows, n), F32),
        compiler_params=_cparams(("arbitrary", "arbitrary")),
        name="ada",
    )(c_all, w_ada, b_ada.reshape(depth, 1, n))


def _mod_spec(mod, tm, tiles_per_group, last=None):
    r = mod.shape[1]
    clamp = (lambda i: i) if last is None else (lambda i: jnp.minimum(i, last))
    if r == 1:
        return pl.BlockSpec((None, 1, mod.shape[2]), lambda i, *_: (clamp(i) // tiles_per_group, 0, 0))
    return pl.BlockSpec((None, r, mod.shape[2]), lambda i, *_: (clamp(i), 0, 0))


def _in_proj_kernel(x_ref, sc_ref, sh_ref, wm_ref, ws_ref, proj_ref, small_ref, hb_ref):
    @pl.when(pl.program_id(1) == 0)
    def _():
        h = x_ref[...] * (1.0 + sc_ref[...]) + sh_ref[...]
        hb_ref[...] = h.astype(BF16)
        small_ref[...] = _dot(hb_ref[...], ws_ref[...])

    proj_ref[...] = _dot(hb_ref[...], wm_ref[...])


def _in_proj(x, t, blk0, sc, sh, w_main, w_small, tm, tiles_per_group):
    d = x.shape[1]
    tn = 1024
    return pl.pallas_call(
        _in_proj_kernel,
        grid=(t // tm, MAIN_W // tn),
        in_specs=[pl.BlockSpec((tm, d), lambda i, j: (i + blk0, 0)),
                  _mod_spec(sc, tm, tiles_per_group),
                  _mod_spec(sh, tm, tiles_per_group),
                  pl.BlockSpec((d, tn), lambda i, j: (0, j)),
                  pl.BlockSpec((d, SMALL_W), lambda i, j: (0, 0))],
        out_specs=[pl.BlockSpec((tm, tn), lambda i, j: (i, j)),
                   pl.BlockSpec((tm, SMALL_W), lambda i, j: (i, 0))],
        out_shape=[jax.ShapeDtypeStruct((t, MAIN_W), F32),
                   jax.ShapeDtypeStruct((t, SMALL_W), F32)],
        scratch_shapes=[pltpu.VMEM((tm, d), BF16)],
        compiler_params=_cparams(("arbitrary", "arbitrary")),
        name="in_proj",
    )(x, sc, sh, w_main, w_small)


def _iota2(shape, dim):
    return lax.broadcasted_iota(I32, shape, dim)


def _conv_silu(xp_ref, x_ref, w_ref, bias, c):
    xp_ref[SUBLANES:SUBLANES + c, :] = x_ref[...]
    base = SUBLANES - (CONV_W - 1)
    y = xp_ref[base:base + c, :] * w_ref[0:1, :]
    for j in range(1, CONV_W):
        y = y + xp_ref[base + j:base + j + c, :] * w_ref[j:j + 1, :]
    if bias is not None:
        y = y + bias
    return _silu(y)


def _dn_kernel(qkv_ref, gate_ref, sm_ref, smt_ref, conv0_ref, s0_ref, cw_ref, prow_ref, pcol_ref, nw_ref,
               o_ref, conv_out_ref, s_out_ref, xp_ref, s_ref, y_ref, *, c, valid):
    n = pl.program_id(1)
    base = SUBLANES - (CONV_W - 1)

    @pl.when(n == 0)
    def _():
        xp_ref[base:SUBLANES, :] = conv0_ref[...]
        s_ref[...] = s0_ref[...]

    y = _conv_silu(xp_ref, qkv_ref, cw_ref, None, c)
    tail = xp_ref[SUBLANES + valid - (CONV_W - 1):SUBLANES + valid, :]
    conv_out_ref[...] = tail
    xp_ref[base:SUBLANES, :] = tail

    row_ok = _iota2((c, 1), 0) < valid
    col_ok = _iota2((1, c), 1) < valid
    sm = sm_ref[...]
    g_all = -jnp.exp(prow_ref[0:1, :]) * _softplus(sm + prow_ref[1:2, :])
    g_all = jnp.where(row_ok, g_all, 0.0)
    beta_all = jnp.where(row_ok, _sigmoid(sm), 0.0)
    smt = smt_ref[...]
    g_t = -jnp.exp(pcol_ref[:, 0:1]) * _softplus(smt[0:DN_HEADS, :] + pcol_ref[:, 1:2])
    g_t = jnp.where(col_ok, g_t, 0.0)
    row = _iota2((c, c), 0)
    col = _iota2((c, c), 1)
    tri = jnp.where(row >= col, 1.0, 0.0).astype(F32)
    lc_all = _dot3(tri, g_all)
    lc_t = _dot3(g_t, jnp.where(row <= col, 1.0, 0.0).astype(F32))
    causal = row >= col
    strict = row > col

    heads = range(DN_HEADS)
    y_ref[...] = y
    qs, ks, vs, lcs, betas, decays, mats, qks = [], [], [], [], [], [], [], []
    for h in heads:
        qh = y_ref[:, h * DN_D:(h + 1) * DN_D]
        kh = y_ref[:, DN_KEY + h * DN_D:DN_KEY + (h + 1) * DN_D]
        vh = y_ref[:, 2 * DN_KEY + h * DN_D:2 * DN_KEY + (h + 1) * DN_D]
        qh = qh * lax.rsqrt(jnp.sum(qh * qh, -1, keepdims=True) + NORM_EPS) * (DN_D ** -0.5)
        kh = kh * lax.rsqrt(jnp.sum(kh * kh, -1, keepdims=True) + NORM_EPS)
        qs.append(qh)
        ks.append(jnp.where(row_ok, kh, 0.0))
        vs.append(jnp.where(row_ok, vh, 0.0))
        lcs.append(lc_all[:, SM_A + h:SM_A + h + 1])
        betas.append(beta_all[:, SM_B + h:SM_B + h + 1])
        decays.append(jnp.exp(jnp.where(causal, lcs[h] - lc_t[h:h + 1, :], -jnp.inf)))
    kbs = [ks[h] * betas[h] for h in heads]
    for h in heads:
        mats.append(jnp.where(strict, _bdot(kbs[h], ks[h], _dot_nt) * decays[h], 0.0))
        qks.append(_bdot(qs[h], ks[h], _dot_nt) * decays[h])

    m1 = (row % 2 == 1) & (col == row - 1)
    eye = jnp.where(row == col, 1.0, 0.0).astype(F32)
    ts = [eye - jnp.where(m1, mats[h], 0.0) for h in heads]
    s = 2
    while s < c:
        m = ((row // s) % 2 == 1) & ((col // s) % 2 == 0) & (row // (2 * s) == col // (2 * s))
        tsp = [_split(ts[h]) for h in heads]
        ams = [_split(jnp.where(m, mats[h], 0.0)) for h in heads]
        ps = [_dot(tsp[h][0], ams[h][0]) + _dot(tsp[h][0], ams[h][1]) + _dot(tsp[h][1], ams[h][0]) for h in heads]
        psp = [_split(ps[h]) for h in heads]
        ts = [ts[h] - (_dot(psp[h][0], tsp[h][0]) + _dot(psp[h][0], tsp[h][1]) + _dot(psp[h][1], tsp[h][0]))
              for h in heads]
        s *= 2

    e_lcs = [jnp.exp(lcs[h]) for h in heads]
    us = [_dot3(ts[h], vs[h] * betas[h]) for h in heads]
    ws = [_dot3(ts[h], kbs[h] * e_lcs[h]) for h in heads]
    shs = [s_ref[h] for h in heads]
    v_news = [us[h] - _bdot(ws[h], shs[h]) for h in heads]
    os_ = [_bdot(qs[h] * e_lcs[h], shs[h]) + _bdot(qks[h], v_news[h]) for h in heads]
    lasts = [lc_all[c - 1:c, SM_A + h:SM_A + h + 1] for h in heads]
    s_news = [shs[h] * jnp.exp(lasts[h]) + _bdot(ks[h] * jnp.exp(lasts[h] - lcs[h]), v_news[h], _dot_tn)
              for h in heads]
    for h in heads:
        s_ref[h] = s_news[h]
        s_out_ref[h] = s_news[h]
        o = os_[h]
        o = o * lax.rsqrt(jnp.mean(o * o, -1, keepdims=True) + NORM_EPS) * nw_ref[...]
        sl = slice(h * DN_D, (h + 1) * DN_D)
        o_ref[:, sl] = (o * _silu(gate_ref[:, sl])).astype(o_ref.dtype)


def _dn(qkv, gate, sm, smt, conv0, s0, conv_w, prow, pcol, norm_w, *, c, valid, cb_qkv, cb_gate, out_dtype):
    b, lp = qkv.shape[0], qkv.shape[1]
    nch = lp // c
    kern = functools.partial(_dn_kernel, c=c, valid=valid)
    return pl.pallas_call(
        kern,
        grid=(b, nch),
        in_specs=[pl.BlockSpec((None, c, DN_CONV_DIM), lambda i, n: (i, n, cb_qkv)),
                  pl.BlockSpec((None, c, DN_KEY), lambda i, n: (i, n, cb_gate)),
                  pl.BlockSpec((None, c, SMALL_W), lambda i, n: (i, n, 0)),
                  pl.BlockSpec((None, None, 32, c), lambda i, n: (i, n, 0, 0)),
                  pl.BlockSpec((None, CONV_W - 1, DN_CONV_DIM), lambda i, n: (i, 0, 0)),
                  pl.BlockSpec((None, DN_HEADS, DN_D, DN_D), lambda i, n: (i, 0, 0, 0)),
                  pl.BlockSpec((CONV_W, DN_CONV_DIM), lambda i, n: (0, 0)),
                  pl.BlockSpec((2, SMALL_W), lambda i, n: (0, 0)),
                  pl.BlockSpec((DN_HEADS, 2), lambda i, n: (0, 0)),
                  pl.BlockSpec((1, DN_D), lambda i, n: (0, 0))],
        out_specs=[pl.BlockSpec((None, c, DN_KEY), lambda i, n: (i, n, 0)),
                   pl.BlockSpec((None, CONV_W - 1, DN_CONV_DIM), lambda i, n: (i, 0, 0)),
                   pl.BlockSpec((None, DN_HEADS, DN_D, DN_D), lambda i, n: (i, 0, 0, 0))],
        out_shape=[jax.ShapeDtypeStruct((b, lp, DN_KEY), out_dtype),
                   jax.ShapeDtypeStruct((b, CONV_W - 1, DN_CONV_DIM), F32),
                   jax.ShapeDtypeStruct((b, DN_HEADS, DN_D, DN_D), F32)],
        scratch_shapes=[pltpu.VMEM((c + SUBLANES, DN_CONV_DIM), F32),
                        pltpu.VMEM((DN_HEADS, DN_D, DN_D), F32),
                        pltpu.VMEM((c, DN_CONV_DIM), F32)],
        compiler_params=_cparams(("arbitrary", "arbitrary")),
        name="delta_rule",
    )(qkv, gate, sm, smt, conv0, s0, conv_w, prow, pcol, norm_w)


def _ssd_kernel(z_ref, xbc_ref, sm_ref, smt_ref, conv0_ref, h0_ref, cw_ref, cb_ref, prow_ref, pcol_ref, nw_ref,
                o_ref, conv_out_ref, h_out_ref, xp_ref, h_ref, y_ref, *, c, valid):
    n = pl.program_id(1)
    base = SUBLANES - (CONV_W - 1)

    @pl.when(n == 0)
    def _():
        xp_ref[base:SUBLANES, :] = conv0_ref[...]
        h_ref[...] = h0_ref[...]

    act = _conv_silu(xp_ref, xbc_ref, cw_ref, cb_ref[...], c)
    tail = xp_ref[SUBLANES + valid - (CONV_W - 1):SUBLANES + valid, :]
    conv_out_ref[...] = tail
    xp_ref[base:SUBLANES, :] = tail

    row_ok = _iota2((c, 1), 0) < valid
    col_ok = _iota2((1, c), 1) < valid
    dt_all = _softplus(sm_ref[...] + prow_ref[0:1, :])
    la_all = jnp.where(row_ok, dt_all * -jnp.exp(prow_ref[1:2, :]), 0.0)
    dt_t = _softplus(smt_ref[SM_DT:SM_DT + SSM_HEADS, :] + pcol_ref[:, 0:1])
    la_t = jnp.where(col_ok, dt_t * -jnp.exp(pcol_ref[:, 1:2]), 0.0)
    row = _iota2((c, c), 0)
    col = _iota2((c, c), 1)
    lc_all = _dot3(jnp.where(row >= col, 1.0, 0.0).astype(F32), la_all)
    lc_t = _dot3(la_t, jnp.where(row <= col, 1.0, 0.0).astype(F32))
    causal = row >= col
    rep = SSM_HEADS // SSM_GROUPS

    for g in range(SSM_GROUPS):
        bg = act[:, SSM_INNER + g * SSM_N:SSM_INNER + (g + 1) * SSM_N]
        cg = act[:, SSM_INNER + (SSM_GROUPS + g) * SSM_N:SSM_INNER + (SSM_GROUPS + g + 1) * SSM_N]
        bg = jnp.where(row_ok, bg, 0.0)
        cbg = _bdot(cg, bg, _dot_nt)
        for hh in range(rep):
            h = g * rep + hh
            sl = slice(h * SSM_P, (h + 1) * SSM_P)
            xh = jnp.where(row_ok, act[:, sl], 0.0)
            dt_c = dt_all[:, SM_DT + h:SM_DT + h + 1]
            lc_c = lc_all[:, SM_DT + h:SM_DT + h + 1]
            lc_r = lc_t[h:h + 1, :]
            xd = xh * dt_c
            decay = jnp.exp(jnp.where(causal, lc_c - lc_r, -jnp.inf))
            st = h_ref[h]
            y = _bdot(cbg * decay, xd) + _bdot(cg * jnp.exp(lc_c), st, _dot_nt)
            last = lc_all[c - 1:c, SM_DT + h:SM_DT + h + 1]
            st_new = st * jnp.exp(last) + _bdot(xd, bg * jnp.exp(last - lc_c), _dot_tn)
            h_ref[h] = st_new
            h_out_ref[h] = st_new
            y = y + act[:, sl] * prow_ref[2:3, SM_DT + h:SM_DT + h + 1]
            y_ref[:, sl] = y * _silu(z_ref[:, sl])

    gw = SSM_INNER // SSM_GROUPS
    for g in range(SSM_GROUPS):
        yg = y_ref[:, g * gw:(g + 1) * gw]
        yg = yg * lax.rsqrt(jnp.mean(yg * yg, -1, keepdims=True) + NORM_EPS) * nw_ref[:, g * gw:(g + 1) * gw]
        o_ref[:, g * gw:(g + 1) * gw] = yg.astype(o_ref.dtype)


def _ssd(z, xbc, sm, smt, conv0, h0, conv_w, conv_b, prow, pcol, norm_w, *, c, valid, cb_z, cb_xbc, out_dtype):
    b, lp = z.shape[0], z.shape[1]
    nch = lp // c
    kern = functools.partial(_ssd_kernel, c=c, valid=valid)
    return pl.pallas_call(
        kern,
        grid=(b, nch),
        in_specs=[pl.BlockSpec((None, c, SSM_INNER), lambda i, n: (i, n, cb_z)),
                  pl.BlockSpec((None, c, SSM_CONV_DIM), lambda i, n: (i, n, cb_xbc)),
                  pl.BlockSpec((None, c, SMALL_W), lambda i, n: (i, n, 0)),
                  pl.BlockSpec((None, None, 32, c), lambda i, n: (i, n, 0, 0)),
                  pl.BlockSpec((None, CONV_W - 1, SSM_CONV_DIM), lambda i, n: (i, 0, 0)),
                  pl.BlockSpec((None, SSM_HEADS, SSM_P, SSM_N), lambda i, n: (i, 0, 0, 0)),
                  pl.BlockSpec((CONV_W, SSM_CONV_DIM), lambda i, n: (0, 0)),
                  pl.BlockSpec((1, SSM_CONV_DIM), lambda i, n: (0, 0)),
                  pl.BlockSpec((3, SMALL_W), lambda i, n: (0, 0)),
                  pl.BlockSpec((SSM_HEADS, 2), lambda i, n: (0, 0)),
                  pl.BlockSpec((1, SSM_INNER), lambda i, n: (0, 0))],
        out_specs=[pl.BlockSpec((None, c, SSM_INNER), lambda i, n: (i, n, 0)),
                   pl.BlockSpec((None, CONV_W - 1, SSM_CONV_DIM), lambda i, n: (i, 0, 0)),
                   pl.BlockSpec((None, SSM_HEADS, SSM_P, SSM_N), lambda i, n: (i, 0, 0, 0))],
        out_shape=[jax.ShapeDtypeStruct((b, lp, SSM_INNER), out_dtype),
                   jax.ShapeDtypeStruct((b, CONV_W - 1, SSM_CONV_DIM), F32),
                   jax.ShapeDtypeStruct((b, SSM_HEADS, SSM_P, SSM_N), F32)],
        scratch_shapes=[pltpu.VMEM((c + SUBLANES, SSM_CONV_DIM), F32),
                        pltpu.VMEM((SSM_HEADS, SSM_P, SSM_N), F32),
                        pltpu.VMEM((c, SSM_INNER), F32)],
        compiler_params=_cparams(("arbitrary", "arbitrary")),
        name="ssd_scan",
    )(z, xbc, sm, smt, conv0, h0, conv_w, conv_b, prow, pcol, norm_w)


def _rope_tables(pos, heads):
    half = ROPE_DIM // 2
    inv_freq = ROPE_THETA ** (-jnp.arange(half, dtype=F32) / half)
    ang = pos.astype(F32)[:, None] * inv_freq[None, :]
    cos, sin = jnp.cos(ang), jnp.sin(ang)
    n = pos.shape[0]
    pad = jnp.zeros((n, SWA_D - ROPE_DIM), F32)
    c_h = jnp.concatenate([cos, cos, pad + 1.0], -1)
    a_h = jnp.concatenate([-sin, jnp.zeros_like(sin), pad], -1)
    b_h = jnp.concatenate([jnp.zeros_like(sin), sin, pad], -1)
    return tuple(jnp.tile(t, (1, heads)) for t in (c_h, a_h, b_h))


def _rope(x, tc, ta, tb):
    half = ROPE_DIM // 2
    w = x.shape[-1]
    return x * tc + pltpu.roll(x, w - half, 1) * ta + pltpu.roll(x, half, 1) * tb


def _sink_attend(q, k, v, valid, sink):
    s = _bdot(q, k, _dot_nt) * (SWA_D ** -0.5)
    s = jnp.where(valid, s, -jnp.inf)
    m = jnp.maximum(jnp.max(s, -1, keepdims=True), sink)
    p = jnp.exp(s - m)
    den = jnp.sum(p, -1, keepdims=True) + jnp.exp(sink - m)
    return _bdot(p / den, v)


def _swa_prompt_kernel(q_ref, kp_ref, kc_ref, vp_ref, vc_ref, qt_ref, ktp_ref, ktc_ref, sink_ref,
                       o_ref, ko_ref, vo_ref):
    i = pl.program_id(1)
    w = WINDOW
    q = _rope(q_ref[...], qt_ref[0], qt_ref[1], qt_ref[2])
    kc = _rope(kc_ref[...], ktc_ref[0], ktc_ref[1], ktc_ref[2])
    kp = _rope(kp_ref[...], ktp_ref[0], ktp_ref[1], ktp_ref[2])
    vc = vc_ref[...]
    ko_ref[...] = kc
    vo_ref[...] = vc
    kk = jnp.concatenate([kp, kc], 0)
    vv = jnp.concatenate([vp_ref[...], vc], 0)
    qi = _iota2((w, 2 * w), 0) + w
    kj = _iota2((w, 2 * w), 1)
    valid = (kj <= qi) & (qi - kj < w) & ((kj >= w) | (i > 0))
    valid = jnp.concatenate([valid] * SWA_GRP, 0)
    for g in range(SWA_KV):
        kg = kk[:, g * SWA_D:(g + 1) * SWA_D]
        vg = vv[:, g * SWA_D:(g + 1) * SWA_D]
        qg = jnp.concatenate([q[:, (g * SWA_GRP + j) * SWA_D:(g * SWA_GRP + j + 1) * SWA_D]
                              for j in range(SWA_GRP)], 0)
        sk = jnp.concatenate([jnp.broadcast_to(sink_ref[0:1, g * SWA_GRP + j:g * SWA_GRP + j + 1], (w, 1))
                              for j in range(SWA_GRP)], 0)
        og = _sink_attend(qg, kg, vg, valid, sk)
        for j in range(SWA_GRP):
            hd = g * SWA_GRP + j
            o_ref[:, hd * SWA_D:(hd + 1) * SWA_D] = og[j * w:(j + 1) * w].astype(o_ref.dtype)


def _swa_prompt(proj, qt, kt, sinks):
    b, l = proj.shape[0], proj.shape[1]
    w = WINDOW
    nb = l // w
    kvw = SWA_KV * SWA_D
    qw = SWA_HEADS * SWA_D
    prev = lambda i, n: (i, jnp.maximum(n - 1, 0), COL_SWK // kvw)
    prev_v = lambda i, n: (i, jnp.maximum(n - 1, 0), COL_SWV // kvw)
    return pl.pallas_call(
        _swa_prompt_kernel,
        grid=(b, nb),
        in_specs=[pl.BlockSpec((None, w, qw), lambda i, n: (i, n, COL_SWQ // qw)),
                  pl.BlockSpec((None, w, kvw), prev),
                  pl.BlockSpec((None, w, kvw), lambda i, n: (i, n, COL_SWK // kvw)),
                  pl.BlockSpec((None, w, kvw), prev_v),
                  pl.BlockSpec((None, w, kvw), lambda i, n: (i, n, COL_SWV // kvw)),
                  pl.BlockSpec((3, w, qw), lambda i, n: (0, n, 0)),
                  pl.BlockSpec((3, w, kvw), lambda i, n: (0, jnp.maximum(n - 1, 0), 0)),
                  pl.BlockSpec((3, w, kvw), lambda i, n: (0, n, 0)),
                  pl.BlockSpec((1, SWA_HEADS), lambda i, n: (0, 0))],
        out_specs=[pl.BlockSpec((None, w, qw), lambda i, n: (i, n, 0)),
                   pl.BlockSpec((None, w, kvw), lambda i, n: (i, 0, 0)),
                   pl.BlockSpec((None, w, kvw), lambda i, n: (i, 0, 0))],
        out_shape=[jax.ShapeDtypeStruct((b, l, qw), BF16),
                   jax.ShapeDtypeStruct((b, w, kvw), F32),
                   jax.ShapeDtypeStruct((b, w, kvw), F32)],
        compiler_params=_cparams(("arbitrary", "arbitrary")),
        name="swa_prompt",
    )(proj, proj, proj, proj, proj, qt, kt, kt, sinks)


def _swa_sample_kernel(q_ref, k_ref, v_ref, kb_ref, vb_ref, qt_ref, kt_ref, sink_ref,
                       o_ref, ko_ref, vo_ref, kk_ref, vv_ref, *, l):
    w = WINDOW
    lp = SUBLANES
    q = _rope(q_ref[...], qt_ref[0], qt_ref[1], qt_ref[2])
    k = _rope(k_ref[...], kt_ref[0], kt_ref[1], kt_ref[2])
    kk_ref[0:w, :] = kb_ref[...]
    kk_ref[w:w + lp, :] = k
    vv_ref[0:w, :] = vb_ref[...]
    vv_ref[w:w + lp, :] = v_ref[...]
    ko_ref[...] = kk_ref[l:l + w, :]
    vo_ref[...] = vv_ref[l:l + w, :]
    kk = kk_ref[...]
    vv = vv_ref[...]
    qi = _iota2((lp, w + lp), 0) + w
    kj = _iota2((lp, w + lp), 1)
    valid = (kj <= qi) & (qi - kj < w) & (kj < w + l)
    valid = jnp.concatenate([valid] * SWA_GRP, 0)
    for g in range(SWA_KV):
        kg = kk[:, g * SWA_D:(g + 1) * SWA_D]
        vg = vv[:, g * SWA_D:(g + 1) * SWA_D]
        qg = jnp.concatenate([q[:, (g * SWA_GRP + j) * SWA_D:(g * SWA_GRP + j + 1) * SWA_D]
                              for j in range(SWA_GRP)], 0)
        sk = jnp.concatenate([jnp.broadcast_to(sink_ref[0:1, g * SWA_GRP + j:g * SWA_GRP + j + 1], (lp, 1))
                              for j in range(SWA_GRP)], 0)
        og = _sink_attend(qg, kg, vg, valid, sk)
        for j in range(SWA_GRP):
            hd = g * SWA_GRP + j
            o_ref[:, hd * SWA_D:(hd + 1) * SWA_D] = og[j * lp:(j + 1) * lp]


def _swa_sample(q, k, v, kbuf, vbuf, qt, kt, sinks, l):
    b = q.shape[0]
    w = WINDOW
    lp = SUBLANES
    kvw = SWA_KV * SWA_D
    qw = SWA_HEADS * SWA_D
    kern = functools.partial(_swa_sample_kernel, l=l)
    return pl.pallas_call(
        kern,
        grid=(b,),
        in_specs=[pl.BlockSpec((None, lp, qw), lambda i: (i, 0, 0)),
                  pl.BlockSpec((None, lp, kvw), lambda i: (i, 0, 0)),
                  pl.BlockSpec((None, lp, kvw), lambda i: (i, 0, 0)),
                  pl.BlockSpec((None, w, kvw), lambda i: (i, 0, 0)),
                  pl.BlockSpec((None, w, kvw), lambda i: (i, 0, 0)),
                  pl.BlockSpec((3, lp, qw), lambda i: (0, 0, 0)),
                  pl.BlockSpec((3, lp, kvw), lambda i: (0, 0, 0)),
                  pl.BlockSpec((1, SWA_HEADS), lambda i: (0, 0))],
        out_specs=[pl.BlockSpec((None, lp, qw), lambda i: (i, 0, 0)),
                   pl.BlockSpec((None, w, kvw), lambda i: (i, 0, 0)),
                   pl.BlockSpec((None, w, kvw), lambda i: (i, 0, 0))],
        out_shape=[jax.ShapeDtypeStruct((b, lp, qw), F32),
                   jax.ShapeDtypeStruct((b, w, kvw), F32),
                   jax.ShapeDtypeStruct((b, w, kvw), F32)],
        scratch_shapes=[pltpu.VMEM((w + lp, kvw), F32), pltpu.VMEM((w + lp, kvw), F32)],
        compiler_params=_cparams(("arbitrary",)),
        name="swa_sample",
    )(q, k, v, kbuf, vbuf, qt, kt, sinks)


def _merge_kernel(oa_ref, ob_ref, oc_ref, ga_ref, gb_ref, gc_ref, wa_ref, wb_ref, wc_ref, o_ref):
    acc = _sigmoid(ga_ref[...]) * _bdot(oa_ref[...], wa_ref[...])
    acc = acc + _sigmoid(gb_ref[...]) * _bdot(ob_ref[...], wb_ref[...])
    acc = acc + _sigmoid(gc_ref[...]) * _bdot(oc_ref[...], wc_ref[...])
    o_ref[...] = acc.astype(o_ref.dtype)


def _merge(oa, ob, oc, proj, wbr, tm):
    t = oa.shape[0]
    tn = 512
    nj = D_MODEL // tn
    gate = lambda k: pl.BlockSpec((tm, tn), lambda i, j: (i, COL_MG // tn + k * nj + j))
    wsp = lambda k: pl.BlockSpec((None, BRANCH_W, tn), lambda i, j: (k, 0, j))
    osp = pl.BlockSpec((tm, BRANCH_W), lambda i, j: (i, 0))
    return pl.pallas_call(
        _merge_kernel,
        grid=(t // tm, nj),
        in_specs=[osp, osp, osp, gate(0), gate(1), gate(2), wsp(0), wsp(1), wsp(2)],
        out_specs=pl.BlockSpec((tm, tn), lambda i, j: (i, j)),
        out_shape=jax.ShapeDtypeStruct((t, D_MODEL), BF16),
        compiler_params=_cparams(("arbitrary", "arbitrary")),
        name="merge",
    )(oa, ob, oc, proj, proj, proj, wbr, wbr, wbr)


def _layer_norm(r, g, b):
    mu = jnp.mean(r, -1, keepdims=True)
    rc = r - mu
    var = jnp.mean(rc * rc, -1, keepdims=True)
    return rc * lax.rsqrt(var + LN_EPS) * g + b


def _to_slab(ref, val):
    rows = val.shape[0]
    for s in range(SLAB):
        ref[pl.ds(s, rows, stride=SLAB), :] = val[:, s * LANES:(s + 1) * LANES]


def _from_slab(ref, lead, start, rows, dtype):
    parts = []
    for s in range(SLAB):
        idx = (pl.ds(start * SLAB + s, rows, stride=SLAB), slice(None))
        parts.append(ref[(lead,) + idx if lead is not None else idx].astype(dtype))
    return jnp.concatenate(parts, axis=1)


def _out_ln_kernel(m_ref, w_ref, x_ref, g1_ref, sc_ref, sh_ref, lg_ref, lb_ref, *rest, alpha, nb):
    x1_ref, h2_ref, hs_ref = rest[-3:]
    i = pl.program_id(0)

    @pl.when(i < nb)
    def _():
        mix = _dot(m_ref[...], w_ref[...])
        x1 = _layer_norm(alpha * x_ref[...] + g1_ref[...] * mix, lg_ref[...], lb_ref[...])
        x1_ref[...] = x1
        h2 = x1 * (1.0 + sc_ref[...]) + sh_ref[...]
        h2_ref[...] = h2
        _to_slab(hs_ref, h2)

    @pl.when(i >= nb)
    def _():
        x1_ref[...] = jnp.zeros_like(x1_ref)
        h2_ref[...] = jnp.zeros_like(h2_ref)
        hs_ref[...] = jnp.zeros_like(hs_ref)


def _out_ln(merged, w_out, x, blk0, t_all, bases, g1, sc2, sh2, ln_g, ln_b, tm, tiles_per_group, alpha):
    t, d = merged.shape
    nb = t // tm
    n_clear = 0 if bases is not None else pl.cdiv(t_all - (blk0 * tm + t), tm)
    last = nb - 1
    row = pl.BlockSpec((tm, d), lambda i: (jnp.minimum(i, last), 0))
    xrow = pl.BlockSpec((tm, d), lambda i: (jnp.minimum(i, last) + blk0, 0))
    orow = pl.BlockSpec((tm, d), lambda i: (i + blk0, 0))
    vec = pl.BlockSpec((1, d), lambda i: (0, 0))
    kern = functools.partial(_out_ln_kernel, alpha=alpha, nb=nb)
    args = [merged, w_out, x, g1, sc2, sh2, ln_g, ln_b]
    in_specs = [row, pl.BlockSpec((d, d), lambda i: (0, 0)), xrow,
                _mod_spec(g1, tm, tiles_per_group, last), _mod_spec(sc2, tm, tiles_per_group, last),
                _mod_spec(sh2, tm, tiles_per_group, last), vec, vec]
    aliases = {}
    if bases is not None:
        aliases = {len(args) + k: k for k in range(len(bases))}
        args += list(bases)
        in_specs += [pl.BlockSpec(memory_space=pl.ANY)] * len(bases)
    return pl.pallas_call(
        kern,
        grid=(nb + n_clear,),
        in_specs=in_specs,
        out_specs=[orow, orow, pl.BlockSpec((tm * SLAB, LANES), lambda i: (i + blk0, 0))],
        out_shape=[jax.ShapeDtypeStruct((t_all, d), F32), jax.ShapeDtypeStruct((t_all, d), F32),
                   jax.ShapeDtypeStruct((t_all * SLAB, LANES), F32)],
        input_output_aliases=aliases,
        compiler_params=_cparams(("arbitrary",)),
        name="out_ln",
    )(*args)


ROUTER_TR = 640


def _router_kernel(h_ref, w_ref, b_ref, eidx_ref, rank_ref, wgt_ref, cnt_ref, carry_ref):
    i = pl.program_id(0)
    tr = h_ref.shape[0]
    ng, gs = N_GROUPS, GROUP_SIZE

    @pl.when(i == 0)
    def _():
        carry_ref[...] = jnp.zeros_like(carry_ref)

    logits = _dot3(w_ref[...], h_ref[...], _dot_nt)
    scores = _sigmoid(logits)
    sc3 = scores.reshape(ng, gs, tr)
    ch3 = (scores + b_ref[...]).reshape(ng, gs, tr)
    io_e = _iota2((ng, gs, tr), 1)
    io_g = _iota2((ng, 1, tr), 0)
    io_x = _iota2((ng, gs, tr), 0) * gs + io_e
    ninf = -jnp.inf

    m1 = jnp.max(ch3, 1, keepdims=True)
    i1 = jnp.min(jnp.where(ch3 == m1, io_e, gs), 1, keepdims=True)
    m2 = jnp.max(jnp.where(io_e == i1, ninf, ch3), 1, keepdims=True)
    grp = m1 + m2
    keep = jnp.zeros((ng, 1, tr), jnp.bool_)
    for _ in range(TOPK_GROUPS):
        m = jnp.max(grp, 0, keepdims=True)
        first = jnp.min(jnp.where(grp == m, io_g, ng), 0, keepdims=True)
        hit = io_g == first
        keep = keep | hit
        grp = jnp.where(hit, ninf, grp)

    cm = jnp.where(keep, ch3, ninf)
    hits = []
    firsts = []
    for _ in range(TOP_K):
        m = jnp.max(jnp.max(cm, 1, keepdims=True), 0, keepdims=True)
        cand = jnp.where(cm == m, io_x, N_EXPERTS)
        first = jnp.min(jnp.min(cand, 1, keepdims=True), 0, keepdims=True)
        hit = io_x == first
        hits.append(hit)
        firsts.append(first)
        cm = jnp.where(hit, ninf, cm)
    sel = hits[0]
    for hit in hits[1:]:
        sel = sel | hit
    self32 = jnp.where(sel, 1.0, 0.0).astype(F32)
    wsel = sc3 * self32
    den = jnp.sum(jnp.sum(wsel, 1, keepdims=True), 0, keepdims=True)
    comb = wsel / den * ROUTED_SCALE

    sel2 = self32.reshape(N_EXPERTS, tr)
    upper = jnp.where(_iota2((tr, tr), 0) <= _iota2((tr, tr), 1), 1.0, 0.0).astype(BF16)
    incl = _dot(sel2.astype(BF16), upper)
    carry = carry_ref[:, 0:1]
    rank3 = (carry + incl - sel2).reshape(ng, gs, tr)
    for r in range(TOP_K):
        hf = jnp.where(hits[r], 1.0, 0.0).astype(F32)
        rk = jnp.sum(jnp.sum(hf * rank3, 1, keepdims=True), 0, keepdims=True)
        wg = jnp.sum(jnp.sum(hf * comb, 1, keepdims=True), 0, keepdims=True)
        eidx_ref[r:r + 1, :] = firsts[r].reshape(1, tr)
        rank_ref[r:r + 1, :] = rk.reshape(1, tr).astype(I32)
        wgt_ref[r:r + 1, :] = wg.reshape(1, tr)
    new_carry = carry + incl[:, tr - 1:tr]
    carry_ref[...] = jnp.broadcast_to(new_carry, carry_ref.shape)
    cnt_ref[...] = jnp.broadcast_to(new_carry, cnt_ref.shape).astype(I32)


def _router(h2, wr_t, bias_col):
    t, d = h2.shape
    tr = ROUTER_TR
    out = pl.BlockSpec((TOP_K, tr), lambda i: (0, i))
    return pl.pallas_call(
        _router_kernel,
        grid=(t // tr,),
        in_specs=[pl.BlockSpec((tr, d), lambda i: (i, 0)),
                  pl.BlockSpec((N_EXPERTS, d), lambda i: (0, 0)),
                  pl.BlockSpec((N_EXPERTS, 1), lambda i: (0, 0))],
        out_specs=[out, out, out, pl.BlockSpec((N_EXPERTS, LANES), lambda i: (0, 0))],
        out_shape=[jax.ShapeDtypeStruct((TOP_K, t), I32), jax.ShapeDtypeStruct((TOP_K, t), I32),
                   jax.ShapeDtypeStruct((TOP_K, t), F32), jax.ShapeDtypeStruct((N_EXPERTS, LANES), I32)],
        scratch_shapes=[pltpu.VMEM((N_EXPERTS, LANES), F32)],
        compiler_params=_cparams(("arbitrary",)),
        name="router",
    )(h2, wr_t, bias_col)


def _invert_kernel(slot_ref, tok_ref):
    def init(r, carry):
        for u in range(LANES):
            tok_ref[r, u] = 0
        return carry
    lax.fori_loop(0, tok_ref.shape[0], init, 0)

    def body(t, carry):
        rows = [slot_ref[k, t] for k in range(TOP_K)]
        for s in rows:
            tok_ref[lax.shift_right_logical(s, LANE_BITS), s & (LANES - 1)] = t
        return carry
    lax.fori_loop(0, slot_ref.shape[1], body, 0)


def _invert(slots, n_rows):
    return pl.pallas_call(
        _invert_kernel,
        grid_spec=pltpu.PrefetchScalarGridSpec(
            num_scalar_prefetch=1,
            grid=(1,),
            in_specs=[],
            out_specs=pl.BlockSpec(memory_space=pltpu.SMEM)),
        out_shape=jax.ShapeDtypeStruct((n_rows // LANES, LANES), I32),
        compiler_params=_cparams(("arbitrary",)),
        name="invert_slots",
    )(slots)


def _cast_weights(te_ref, wg_ref, wu_ref, wd_ref, wgb_ref, wub_ref, wdb_ref):
    i = pl.program_id(0)
    prev = te_ref[jnp.maximum(i - 1, 0)]

    @pl.when((i == 0) | (te_ref[i] != prev))
    def _():
        wgb_ref[...] = wg_ref[...].astype(BF16)
        wub_ref[...] = wu_ref[...].astype(BF16)
        wdb_ref[...] = wd_ref[...].astype(BF16)


def _swiglu(xb, wgb_ref, wub_ref, wdb_ref):
    a = _dot(xb, wgb_ref[...])
    u = _dot(xb, wub_ref[...])
    return _dot((_silu(a) * u).astype(BF16), wdb_ref[...])


def _ffn_kernel(te_ref, na_ref, x_ref, wg_ref, wu_ref, wd_ref, y_ref, wgb_ref, wub_ref, wdb_ref):
    _cast_weights(te_ref, wg_ref, wu_ref, wd_ref, wgb_ref, wub_ref, wdb_ref)
    y_ref[...] = _swiglu(x_ref[...].astype(BF16), wgb_ref, wub_ref, wdb_ref)


def _slab_copy(src_ref, row, dst_ref, buf, slot, sem):
    src = src_ref.at[pl.ds(pl.multiple_of(row * SLAB, SLAB), SLAB)]
    dst = dst_ref.at[buf, pl.ds(pl.multiple_of(slot * SLAB, SLAB), SLAB)]
    return pltpu.make_async_copy(src, dst, sem.at[buf])


def _gffn_kernel(te_ref, na_ref, tok_ref, h_ref, wg_ref, wu_ref, wd_ref, y_ref,
                 wgb_ref, wub_ref, wdb_ref, xbuf_ref, sem, *, tm):
    i = pl.program_id(0)
    na = na_ref[0]
    unroll = SUBLANES
    tok_rows = tm // LANES

    def issue(tile, b):
        for q in range(tok_rows):
            def body(g, carry):
                for u in range(unroll):
                    c = g * unroll + u
                    _slab_copy(h_ref, tok_ref[tile * tok_rows + q, c], xbuf_ref, b, q * LANES + c, sem).start()
                return carry
            lax.fori_loop(0, LANES // unroll, body, 0)

    @pl.when(i == 0)
    def _():
        issue(0, 0)

    @pl.when(i + 1 < na)
    def _():
        issue(i + 1, (i + 1) % 2)

    _cast_weights(te_ref, wg_ref, wu_ref, wd_ref, wgb_ref, wub_ref, wdb_ref)

    @pl.when(i < na)
    def _():
        b = i % 2

        def wbody(j, carry):
            for u in range(unroll):
                _slab_copy(h_ref, 0, xbuf_ref, b, 0, sem).wait()
            return carry
        lax.fori_loop(0, tm // unroll, wbody, 0)
        xb = _from_slab(xbuf_ref, b, 0, tm, BF16)
        _to_slab(y_ref, _swiglu(xb, wgb_ref, wub_ref, wdb_ref))

    @pl.when(i >= na)
    def _():
        y_ref[...] = jnp.zeros_like(y_ref)


def _wspecs(layer, d, ff):
    wmap = lambda i, te, *_: (layer, te[i], 0, 0)
    return [pl.BlockSpec((None, None, d, ff), wmap), pl.BlockSpec((None, None, d, ff), wmap),
            pl.BlockSpec((None, None, ff, d), wmap)]


def _wscratch(d, ff):
    return [pltpu.VMEM((d, ff), BF16), pltpu.VMEM((d, ff), BF16), pltpu.VMEM((ff, d), BF16)]


def _ffn(te, na, x, wg, wu, wd, layer, tm):
    t, d = x.shape
    ff = wg.shape[-1]
    return pl.pallas_call(
        _ffn_kernel,
        grid_spec=pltpu.PrefetchScalarGridSpec(
            num_scalar_prefetch=2,
            grid=(t // tm,),
            in_specs=[pl.BlockSpec((tm, d), lambda i, *_: (i, 0))] + _wspecs(layer, d, ff),
            out_specs=pl.BlockSpec((tm, d), lambda i, *_: (i, 0)),
            scratch_shapes=_wscratch(d, ff)),
        out_shape=jax.ShapeDtypeStruct((t, d), F32),
        compiler_params=_cparams(("arbitrary",)),
        name="shared_ffn",
    )(te, na, x, wg, wu, wd)


def _gffn(te, na, tok, h2s, wg, wu, wd, layer, tm):
    d = D_MODEL
    ff = wg.shape[-1]
    n_rows = tok.shape[0] * tok.shape[1]
    assert tm % LANES == 0
    kern = functools.partial(_gffn_kernel, tm=tm)
    return pl.pallas_call(
        kern,
        grid_spec=pltpu.PrefetchScalarGridSpec(
            num_scalar_prefetch=3,
            grid=(n_rows // tm,),
            in_specs=[pl.BlockSpec(memory_space=pl.ANY)] + _wspecs(layer, d, ff),
            out_specs=pl.BlockSpec((tm * SLAB, LANES), lambda i, *_: (i, 0)),
            scratch_shapes=_wscratch(d, ff) + [pltpu.VMEM((2, tm * SLAB, LANES), F32),
                                               pltpu.SemaphoreType.DMA((2,))]),
        out_shape=jax.ShapeDtypeStruct((n_rows * SLAB, LANES), F32),
        compiler_params=_cparams(("arbitrary",)),
        name="expert_ffn",
    )(te, na, tok, h2s, wg, wu, wd)


COMBINE_TC = 64


def _combine_kernel(slot_ref, ys_ref, w_ref, sh_ref, x_ref, g2_ref, lg_ref, lb_ref, o_ref, buf_ref, g2x_ref, sem,
                    *, alpha):
    i = pl.program_id(0)
    n = pl.num_programs(0)
    tc = COMBINE_TC

    def issue(tile, b):
        def body(j, carry):
            for k in range(TOP_K):
                _slab_copy(ys_ref, slot_ref[k, tile * tc + j], buf_ref, b, k * tc + j, sem).start()
            return carry
        lax.fori_loop(0, tc, body, 0)

    @pl.when(i == 0)
    def _():
        issue(0, 0)

    @pl.when(i + 1 < n)
    def _():
        issue(i + 1, (i + 1) % 2)

    b = i % 2

    def wbody(j, carry):
        for k in range(TOP_K):
            _slab_copy(ys_ref, 0, buf_ref, b, 0, sem).wait()
        return carry
    lax.fori_loop(0, tc, wbody, 0)

    for s in range(SLAB):
        for u in range(MOD_GROUP):
            g2x_ref[s, pl.ds(u, tc // MOD_GROUP, stride=MOD_GROUP), :] = g2_ref[:, s * LANES:(s + 1) * LANES]
    g2 = jnp.concatenate([g2x_ref[s] for s in range(SLAB)], axis=1)

    w = w_ref[...]
    acc = sh_ref[...]
    for k in range(TOP_K):
        acc = acc + w[:, k:k + 1] * _from_slab(buf_ref, b, k * tc, tc, F32)
    o_ref[...] = _layer_norm(alpha * x_ref[...] + g2 * acc, lg_ref[...], lb_ref[...])


def _combine(slots, ys, wgt, shared, x1, g2grp, ln_g, ln_b, alpha):
    t, d = x1.shape
    tc = COMBINE_TC
    row = pl.BlockSpec((tc, d), lambda i, *_: (i, 0))
    vec = pl.BlockSpec((1, d), lambda i, *_: (0, 0))
    kern = functools.partial(_combine_kernel, alpha=alpha)
    return pl.pallas_call(
        kern,
        grid_spec=pltpu.PrefetchScalarGridSpec(
            num_scalar_prefetch=1,
            grid=(t // tc,),
            in_specs=[pl.BlockSpec(memory_space=pl.ANY),
                      pl.BlockSpec((tc, TOP_K), lambda i, *_: (i, 0)),
                      row, row, pl.BlockSpec((tc // MOD_GROUP, d), lambda i, *_: (i, 0)), vec, vec],
            out_specs=row,
            scratch_shapes=[pltpu.VMEM((2, TOP_K * tc * SLAB, LANES), F32), pltpu.VMEM((SLAB, tc, LANES), F32),
                            pltpu.SemaphoreType.DMA((2,))]),
        out_shape=jax.ShapeDtypeStruct((t, d), F32),
        compiler_params=_cparams(("arbitrary",)),
        name="combine_ln",
    )(slots, ys, wgt, shared, x1, g2grp, ln_g, ln_b)


def _moe(h2, h2s, x1, g2grp, p, layer, alpha):
    t, d = h2.shape
    tm = MOE_TM
    eidx, rank, wgt, cnt = _router(h2, p['w_router'].T, p['router_bias'].reshape(N_EXPERTS, 1))
    counts = cnt[:, 0]
    tiles = (counts + tm - 1) // tm
    tile_end = jnp.cumsum(tiles)
    offs = ((tile_end - tiles) * tm).astype(I32)
    n_active = tile_end[-1]
    n_tiles = (t * TOP_K + N_EXPERTS * (tm - 1)) // tm
    ids = jnp.minimum(jnp.arange(n_tiles, dtype=I32), n_active - 1)
    te = jnp.sum((tile_end[None, :] <= ids[:, None]).astype(I32), axis=1)
    onehot = eidx[..., None] == jnp.arange(N_EXPERTS, dtype=I32)
    slots = rank + jnp.sum(jnp.where(onehot, offs, 0), -1)
    tok = _invert(slots, n_tiles * tm)
    ys = _gffn(te, n_active.reshape(1).astype(I32), tok, h2s, p['w_exp_gate'], p['w_exp_up'], p['w_exp_down'],
               layer, tm)
    tsh = 640
    nsh = t // tsh
    sh4 = lambda w: w.reshape((w.shape[0], 1) + w.shape[1:])
    shared = _ffn(jnp.zeros((nsh,), I32), jnp.full((1,), nsh, I32), h2,
                  sh4(p['w_sh_gate']), sh4(p['w_sh_up']), sh4(p['w_sh_down']), layer, tsh)
    return _combine(slots, ys, wgt.T, shared, x1, g2grp, p['ln2_g'].reshape(1, d), p['ln2_b'].reshape(1, d), alpha)


def _chunk_t(sm, c):
    b, l, _ = sm.shape
    return jnp.swapaxes(sm.reshape(b, l // c, c, SMALL_W)[..., :32], -1, -2)


def _pad_rows(a, rows):
    return jnp.pad(a, ((0, 0), (0, rows - a.shape[1]), (0, 0)))


def _mixer_group(x_all, b, l, row0, mod, states, p, tabs, bases, *, prompt):
    d = x_all.shape[1]
    t = b * l
    dn_conv0, dn_s0, ssm_conv0, ssm_h0, kv_buf = states
    sh1, sc1, g1, sh2, sc2, g2 = jnp.split(mod, 6, axis=-1)
    if prompt:
        tm = 1024
        tpg = l // tm
        shape = lambda m: m.reshape(b, 1, d)
    else:
        tm = t
        tpg = 1
        shape = lambda m: jnp.repeat(m, l, axis=0).reshape(1, t, d)
    proj, small = _in_proj(x_all, t, row0 // tm, shape(sc1), shape(sh1), p['w_main'], p['w_small'], tm, tpg)
    proj3 = proj.reshape(b, l, MAIN_W)
    small3 = small.reshape(b, l, SMALL_W)
    if prompt:
        c = CHUNK
        dn_in = (proj3, proj3)
        dn_cb = dict(cb_qkv=COL_QKV // DN_CONV_DIM, cb_gate=COL_DNG // DN_KEY)
        ssd_in = (proj3, proj3)
        ssd_cb = dict(cb_z=COL_SSZ // SSM_INNER, cb_xbc=COL_XBC // SSM_CONV_DIM)
        sm_in = small3
        odt = BF16
    else:
        c = SUBLANES
        cut = lambda c0, w: _pad_rows(proj3[:, :, c0:c0 + w], c)
        dn_in = (cut(COL_QKV, DN_CONV_DIM), cut(COL_DNG, DN_KEY))
        dn_cb = dict(cb_qkv=0, cb_gate=0)
        ssd_in = (cut(COL_SSZ, SSM_INNER), cut(COL_XBC, SSM_CONV_DIM))
        ssd_cb = dict(cb_z=0, cb_xbc=0)
        sm_in = _pad_rows(small3, c)
        odt = F32
    valid = c if prompt else l
    smt = _chunk_t(sm_in, c)
    o_a, dn_conv, dn_s = _dn(dn_in[0], dn_in[1], sm_in, smt, dn_conv0, dn_s0, p['dn_conv_w'], p['dn_prow'],
                             p['dn_pcol'], p['dn_norm_w'].reshape(1, DN_D), c=c, valid=valid, out_dtype=odt, **dn_cb)
    o_b, ssm_conv, ssm_h = _ssd(ssd_in[0], ssd_in[1], sm_in, smt, ssm_conv0, ssm_h0, p['ssm_conv_w'],
                                p['ssm_conv_b'].reshape(1, SSM_CONV_DIM), p['ssm_prow'], p['ssm_pcol'],
                                p['ssm_norm_w'].reshape(1, SSM_INNER), c=c, valid=valid, out_dtype=odt, **ssd_cb)
    sinks = p['swa_sinks'].reshape(1, SWA_HEADS)
    if prompt:
        o_c, k_new, v_new = _swa_prompt(proj3, tabs[0], tabs[1], sinks)
    else:
        o_c, k_new, v_new = _swa_sample(cut(COL_SWQ, SWA_HEADS * SWA_D), cut(COL_SWK, SWA_KV * SWA_D),
                                        cut(COL_SWV, SWA_KV * SWA_D),
                                        kv_buf[0].reshape(b, WINDOW, SWA_KV * SWA_D),
                                        kv_buf[1].reshape(b, WINDOW, SWA_KV * SWA_D), tabs[0], tabs[1], sinks, l)
    flat = lambda o: o[:, :l].reshape(t, BRANCH_W).astype(BF16)
    merged = _merge(flat(o_a), flat(o_b), flat(o_c), proj, p['w_branch'], tm=min(tm, 512))
    tm2 = min(tm, 512)
    tpg2 = l // tm2 if prompt else 1
    shape2 = (lambda m: m.reshape(b, 1, d)) if prompt else shape
    bufs = _out_ln(merged, p['w_out'], x_all, row0 // tm2, x_all.shape[0], bases, shape2(g1), shape2(sc2),
                   shape2(sh2), p['ln1_g'].reshape(1, d), p['ln1_b'].reshape(1, d), tm2, tpg2, p['alpha'])
    g2grp = jnp.repeat(g2, l // MOD_GROUP, axis=0)
    k_new = k_new.reshape(b, WINDOW, SWA_KV, SWA_D)
    v_new = v_new.reshape(b, WINDOW, SWA_KV, SWA_D)
    return bufs, g2grp, (dn_conv, dn_s, ssm_conv, ssm_h, k_new, v_new)


def _forward(x_prompt, x_sample, state_dn_conv, state_dn, state_ssm_conv, state_ssm, cache_swa_k, cache_swa_v,
             c_prompt, c_sample, w_ada, b_ada, w_in, dn_conv_w, dn_a_log, dn_dt_bias, dn_norm_w,
             ssm_conv_w, ssm_conv_b, ssm_a_log, ssm_dt_bias, ssm_d, ssm_norm_w, swa_sinks, w_branch, w_out,
             ln1_g, ln1_b, w_router, router_bias, w_exp_gate, w_exp_up, w_exp_down, w_sh_gate, w_sh_up,
             w_sh_down, ln2_g, ln2_b):
    depth = w_in.shape[0]
    bp, lp, d = x_prompt.shape
    bs, ls, _ = x_sample.shape
    tp, ts = bp * lp, bs * ls
    alpha = (2 * depth) ** 0.25
    past_len = 16384

    c_all = jnp.concatenate([c_prompt, c_sample, jnp.zeros((4, d), F32)], 0)
    mod_all = _ada(c_all, w_ada, b_ada)
    tabs_p = tuple(jnp.stack(_rope_tables(jnp.arange(lp), h)) for h in (SWA_HEADS, SWA_KV))
    tabs_s = tuple(jnp.stack(_rope_tables(past_len + jnp.arange(SUBLANES), h)) for h in (SWA_HEADS, SWA_KV))
    init_p = (jnp.zeros((bp, CONV_W - 1, DN_CONV_DIM), F32), jnp.zeros((bp, DN_HEADS, DN_D, DN_D), F32),
              jnp.zeros((bp, CONV_W - 1, SSM_CONV_DIM), F32), jnp.zeros((bp, SSM_HEADS, SSM_P, SSM_N), F32), None)

    def pad_lanes(v, at):
        return jnp.zeros((SMALL_W,), F32).at[at:at + v.shape[0]].set(v)

    x_all = jnp.concatenate([x_prompt.reshape(tp, d), x_sample.reshape(ts, d)], 0)
    new_p, new_s = [], []
    for l in range(depth):
        w = w_in[l]
        seg = lambda a, n: w[:, a:a + n]
        w_main = jnp.concatenate([seg(8224, 6144), seg(0, 3072), seg(5136, 1536), seg(7712, 256), seg(7968, 256),
                                  seg(3072, 1024), seg(4112, 1024), seg(6688, 1024)], 1).astype(BF16)
        w_small = jnp.concatenate([seg(4096, 16), seg(6672, 16), jnp.zeros((d, SMALL_W - 32), F32)], 1).astype(BF16)
        p = {'w_main': w_main, 'w_small': w_small, 'alpha': alpha,
             'dn_conv_w': dn_conv_w[l], 'dn_norm_w': dn_norm_w[l],
             'dn_prow': jnp.stack([pad_lanes(dn_a_log[l], SM_A), pad_lanes(dn_dt_bias[l], SM_A)]),
             'dn_pcol': jnp.stack([dn_a_log[l], dn_dt_bias[l]], 1),
             'ssm_conv_w': ssm_conv_w[l], 'ssm_conv_b': ssm_conv_b[l], 'ssm_norm_w': ssm_norm_w[l],
             'ssm_prow': jnp.stack([pad_lanes(ssm_dt_bias[l], SM_DT), pad_lanes(ssm_a_log[l], SM_DT),
                                    pad_lanes(ssm_d[l], SM_DT)]),
             'ssm_pcol': jnp.stack([ssm_dt_bias[l], ssm_a_log[l]], 1),
             'swa_sinks': swa_sinks[l], 'w_branch': w_branch[l].astype(BF16), 'w_out': w_out[l].astype(BF16),
             'ln1_g': ln1_g[l], 'ln1_b': ln1_b[l], 'w_router': w_router[l], 'router_bias': router_bias[l],
             'w_exp_gate': w_exp_gate, 'w_exp_up': w_exp_up, 'w_exp_down': w_exp_down,
             'w_sh_gate': w_sh_gate, 'w_sh_up': w_sh_up, 'w_sh_down': w_sh_down,
             'ln2_g': ln2_g[l], 'ln2_b': ln2_b[l]}
        mod = mod_all[l]
        bufs, g2p, st_p = _mixer_group(x_all, bp, lp, 0, mod[:bp], init_p, p, tabs_p, None, prompt=True)
        st_in = (state_dn_conv[l], state_dn[l], state_ssm_conv[l], state_ssm[l], (cache_swa_k[l], cache_swa_v[l]))
        bufs, g2s, st_s = _mixer_group(x_all, bs, ls, tp, mod[bp:bp + bs], st_in, p, tabs_s, bufs, prompt=False)
        x1, h2, h2s = bufs
        x_all = _moe(h2, h2s, x1, jnp.concatenate([g2p, g2s], 0), p, l, alpha)
        new_p.append(st_p)
        new_s.append(st_s)
    outs = [x_all[:tp].reshape(bp, lp, d), x_all[tp:].reshape(bs, ls, d)]
    for k in range(6):
        outs.append(jnp.stack([s[k] for s in new_p]))
        outs.append(jnp.stack([s[k] for s in new_s]))
    return tuple(outs)


def kernel(x_prompt, x_sample, state_dn_conv, state_dn, state_ssm_conv, state_ssm, cache_swa_k, cache_swa_v, c_prompt, c_sample, w_ada, b_ada, w_in, dn_conv_w, dn_a_log, dn_dt_bias, dn_norm_w, ssm_conv_w, ssm_conv_b, ssm_a_log, ssm_dt_bias, ssm_d, ssm_norm_w, swa_sinks, w_branch, w_out, ln1_g, ln1_b, w_router, router_bias, w_exp_gate, w_exp_up, w_exp_down, w_sh_gate, w_sh_up, w_sh_down, ln2_g, ln2_b):
    return _forward(x_prompt, x_sample, state_dn_conv, state_dn, state_ssm_conv, state_ssm, cache_swa_k, cache_swa_v, c_prompt, c_sample, w_ada, b_ada, w_in, dn_conv_w, dn_a_log, dn_dt_bias, dn_norm_w, ssm_conv_w, ssm_conv_b, ssm_a_log, ssm_dt_bias, ssm_d, ssm_norm_w, swa_sinks, w_branch, w_out, ln1_g, ln1_b, w_router, router_bias, w_exp_gate, w_exp_up, w_exp_down, w_sh_gate, w_sh_up, w_sh_down, ln2_g, ln2_b)
```
